```python
import jax, jax.numpy as jnp
from jax import lax
import numpy as np

D_MODEL = 2048
BATCH = 4
SEQ = 4096
DEPTH = 2

PLE_DIM = 256
MIX_WIDTH = D_MODEL
HGRN_WIDTH = MIX_WIDTH // 2
GLA_WIDTH = MIX_WIDTH - HGRN_WIDTH
HGRN_HEAD_DIM = 128
HGRN_HEADS = HGRN_WIDTH // HGRN_HEAD_DIM
GLA_HEADS = 4
GLA_KEY_WIDTH = GLA_WIDTH // 2
GLA_HEAD_K = GLA_KEY_WIDTH // GLA_HEADS
GLA_HEAD_V = GLA_WIDTH // GLA_HEADS
GLA_GATE_RANK = 16
GLA_GATE_TEMP = 16.0
CHUNK = 64
FFN_DIM = 5632
N_EXPERTS = 8
TOP_K = 2
EXPERT_FFN_DIM = 7168
MOE_BLOCK = 512
EPS = 1e-6
N_DENSE = (DEPTH + 1) // 2
N_MOE = DEPTH // 2
IN_SIZES = (HGRN_WIDTH, HGRN_WIDTH, HGRN_WIDTH, HGRN_WIDTH,
            GLA_KEY_WIDTH, GLA_KEY_WIDTH, GLA_WIDTH, GLA_GATE_RANK, GLA_WIDTH)
IN_COLS = sum(IN_SIZES)
IN_SPLITS = tuple(int(s) for s in np.cumsum(IN_SIZES)[:-1])

kernel_name = 'hymba_hgrn2_gla_moe_ple_trunk'


def rms_norm(x, w):
    xf = x.astype(jnp.float32)
    y = xf * lax.rsqrt(jnp.mean(xf * xf, axis=-1, keepdims=True) + EPS)
    return (y * w.astype(jnp.float32)).astype(x.dtype)


def head_rms_norm(o, w):
    H, d = o.shape[-2], o.shape[-1]
    y = o * lax.rsqrt(jnp.mean(o * o, axis=-1, keepdims=True) + EPS)
    return y * w.astype(jnp.float32).reshape(H, d)


def chunk_gated_linear_attn(q, k, v, log_a):
    B, T, H, dk = q.shape
    dv = v.shape[-1]
    n = T // CHUNK

    def to_chunks(t):
        t = t.astype(jnp.float32)
        return t.reshape(B, n, CHUNK, H, t.shape[-1]).transpose(1, 0, 3, 2, 4)

    causal = jnp.tril(jnp.ones((CHUNK, CHUNK), dtype=bool))[:, :, None]

    def step(state, inp):
        qc, kc, vc, gc = inp
        g_cum = jnp.cumsum(gc, axis=2)
        diff = g_cum[:, :, :, None, :] - g_cum[:, :, None, :, :]
        decay = jnp.exp(jnp.where(causal, diff, -jnp.inf))
        scores = jnp.einsum('bhid,bhjd,bhijd->bhij', qc, kc, decay)
        out = (jnp.einsum('bhij,bhjv->bhiv', scores, vc)
               + jnp.einsum('bhid,bhdv->bhiv', qc * jnp.exp(g_cum), state))
        g_last = g_cum[:, :, -1, :]
        k_dec = kc * jnp.exp(g_last[:, :, None, :] - g_cum)
        state = jnp.exp(g_last)[..., None] * state + jnp.einsum('bhjd,bhjv->bhdv', k_dec, vc)
        return state, out

    s0 = jnp.zeros((B, H, dk, dv), jnp.float32)
    _, out = lax.scan(step, s0, (to_chunks(q), to_chunks(k), to_chunks(v), to_chunks(log_a)))
    return out.transpose(1, 0, 3, 2, 4).reshape(B, T, H, dv)


def hybrid_mixer(hn, w_in, lb, hgrn_norm_w, gla_w2, gla_b, gla_norm_w, w_out):
    B, T, _ = hn.shape
    proj = hn @ w_in
    hq, hf, hi, hg, gq, gk, gv, glr, gg = jnp.split(proj, IN_SPLITS, axis=-1)

    lbf = lb.astype(jnp.float32)
    log_f = jnp.logaddexp(jnp.log(lbf), jnp.log1p(-lbf) + jax.nn.log_sigmoid(hf.astype(jnp.float32)))
    k_h = -jnp.expm1(log_f)
    q_h = jax.nn.silu(hq)
    o_h = chunk_gated_linear_attn(q_h.reshape(B, T, HGRN_HEADS, HGRN_HEAD_DIM),
                                  k_h.reshape(B, T, HGRN_HEADS, HGRN_HEAD_DIM),
                                  hi.reshape(B, T, HGRN_HEADS, HGRN_HEAD_DIM),
                                  log_f.reshape(B, T, HGRN_HEADS, HGRN_HEAD_DIM))
    o_h = head_rms_norm(o_h, hgrn_norm_w) * jax.nn.sigmoid(
        hg.astype(jnp.float32)).reshape(B, T, HGRN_HEADS, HGRN_HEAD_DIM)

    log_a = jax.nn.log_sigmoid((glr @ gla_w2 + gla_b).astype(jnp.float32)) / GLA_GATE_TEMP
    o_g = chunk_gated_linear_attn((gq * (GLA_HEAD_K ** -0.5)).reshape(B, T, GLA_HEADS, GLA_HEAD_K),
                                  gk.reshape(B, T, GLA_HEADS, GLA_HEAD_K),
                                  gv.reshape(B, T, GLA_HEADS, GLA_HEAD_V),
                                  log_a.reshape(B, T, GLA_HEADS, GLA_HEAD_K))
    o_g = head_rms_norm(o_g, gla_norm_w) * jax.nn.silu(
        gg.astype(jnp.float32)).reshape(B, T, GLA_HEADS, GLA_HEAD_V)

    o = jnp.concatenate([o_h.reshape(B, T, HGRN_WIDTH), o_g.reshape(B, T, GLA_WIDTH)], axis=-1)
    return o.astype(hn.dtype) @ w_out


def swiglu(x, w_gate, w_up, w_down):
    return (jax.nn.silu(x @ w_gate) * (x @ w_up)) @ w_down


def moe_swiglu(hn, router_w, w_gate, w_up, w_down):
    B, T, D = hn.shape
    tokens = hn.reshape(-1, D)
    n = tokens.shape[0]
    logits = (tokens @ router_w).astype(jnp.float32)
    top_logit, top_e = lax.top_k(logits, TOP_K)
    top_w = jax.nn.softmax(top_logit, axis=-1)
    flat_e = top_e.reshape(-1)
    flat_tok = jnp.repeat(jnp.arange(n, dtype=jnp.int32), TOP_K)
    flat_w = top_w.reshape(-1)
    n_assign = n * TOP_K
    order = jnp.argsort(flat_e)
    sorted_e = flat_e[order]
    counts = jnp.bincount(flat_e, length=N_EXPERTS)
    padded = (counts + MOE_BLOCK - 1) // MOE_BLOCK * MOE_BLOCK
    padded_end = jnp.cumsum(padded)
    padded_start = padded_end - padded
    start = jnp.cumsum(counts) - counts
    dest = padded_start[sorted_e] + jnp.arange(n_assign, dtype=jnp.int32) - start[sorted_e]
    n_blocks = -(-n_assign // MOE_BLOCK) + N_EXPERTS
    cap = n_blocks * MOE_BLOCK
    slot_tok = jnp.full((cap,), n, jnp.int32).at[dest].set(flat_tok[order])
    slot_w = jnp.zeros((cap,), jnp.float32).at[dest].set(flat_w[order])
    block_e = jnp.minimum(jnp.searchsorted(padded_end, jnp.arange(n_blocks) * MOE_BLOCK, side='right'),
                          N_EXPERTS - 1)

    def run_block(args):
        tok_ids, e = args
        xb = jnp.take(tokens, tok_ids, axis=0, mode='clip')
        return swiglu(xb, w_gate[e], w_up[e], w_down[e])

    y = lax.map(run_block, (slot_tok.reshape(n_blocks, MOE_BLOCK), block_e))
    y = y.reshape(cap, D) * slot_w[:, None].astype(y.dtype)
    out = jnp.zeros_like(tokens).at[slot_tok].add(y, mode='drop')
    return out.reshape(B, T, D)


def setup_inputs(seed: int = 0) -> dict:
    key = jax.random.key(seed)
    ks = jax.random.split(key, 24)
    f32 = jnp.float32

    def nrm(k, shape, fan_in):
        return jax.random.normal(k, shape, f32) * (fan_in ** -0.5)

    def gain(k, shape):
        return 1.0 + 0.02 * jax.random.normal(k, shape, f32)

    return {
        'x': jax.random.normal(ks[0], (BATCH, SEQ, D_MODEL), f32),
        'p': jax.random.normal(ks[1], (DEPTH, BATCH, SEQ, PLE_DIM), f32),
        'norm_mix_w': gain(ks[2], (DEPTH, D_MODEL)),
        'w_in': nrm(ks[3], (DEPTH, D_MODEL, IN_COLS), D_MODEL),
        'hgrn_lb_logits': 0.5 * jax.random.normal(ks[4], (DEPTH, HGRN_WIDTH), f32),
        'hgrn_norm_w': gain(ks[5], (DEPTH, HGRN_WIDTH)),
        'gla_gate_w2': nrm(ks[6], (DEPTH, GLA_GATE_RANK, GLA_KEY_WIDTH), GLA_GATE_RANK),
        'gla_gate_b': 0.1 * jax.random.normal(ks[7], (DEPTH, GLA_KEY_WIDTH), f32),
        'gla_norm_w': gain(ks[8], (DEPTH, GLA_WIDTH)),
        'w_out': nrm(ks[9], (DEPTH, MIX_WIDTH, D_MODEL), MIX_WIDTH),
        'norm_ffn_w': gain(ks[10], (DEPTH, D_MODEL)),
        'dense_w_gate': nrm(ks[11], (N_DENSE, D_MODEL, FFN_DIM), D_MODEL),
        'dense_w_up': nrm(ks[12], (N_DENSE, D_MODEL, FFN_DIM), D_MODEL),
        'dense_w_down': nrm(ks[13], (N_DENSE, FFN_DIM, D_MODEL), FFN_DIM),
        'moe_router': nrm(ks[14], (N_MOE, D_MODEL, N_EXPERTS), D_MODEL),
        'moe_w_gate': nrm(ks[15], (N_MOE, N_EXPERTS, D_MODEL, EXPERT_FFN_DIM), D_MODEL),
        'moe_w_up': nrm(ks[16], (N_MOE, N_EXPERTS, D_MODEL, EXPERT_FFN_DIM), D_MODEL),
        'moe_w_down': nrm(ks[17], (N_MOE, N_EXPERTS, EXPERT_FFN_DIM, D_MODEL), EXPERT_FFN_DIM),
        'norm_ple_w': gain(ks[18], (DEPTH, D_MODEL)),
        'ple_w_gate': nrm(ks[19], (DEPTH, D_MODEL, D_MODEL), D_MODEL),
        'ple_w_proj': nrm(ks[20], (DEPTH, PLE_DIM, D_MODEL), PLE_DIM),
        'final_norm_w': gain(ks[21], (D_MODEL,)),
    }


def reference(x, p, norm_mix_w, w_in, hgrn_lb_logits, hgrn_norm_w, gla_gate_w2, gla_gate_b,
              gla_norm_w, w_out, norm_ffn_w, dense_w_gate, dense_w_up, dense_w_down,
              moe_router, moe_w_gate, moe_w_up, moe_w_down, norm_ple_w, ple_w_gate,
              ple_w_proj, final_norm_w):
    lbs = jnp.cumsum(jax.nn.softmax(hgrn_lb_logits.astype(jnp.float32), axis=0), axis=0)
    lbs = lbs - lbs[0]
    h = x
    for i in range(DEPTH):
        h = h + hybrid_mixer(rms_norm(h, norm_mix_w[i]), w_in[i], lbs[i], hgrn_norm_w[i],
                             gla_gate_w2[i], gla_gate_b[i], gla_norm_w[i], w_out[i])
        hn = rms_norm(h, norm_ffn_w[i])
        j = i // 2
        if i % 2 == 0:
            h = h + swiglu(hn, dense_w_gate[j], dense_w_up[j], dense_w_down[j])
        else:
            h = h + moe_swiglu(hn, moe_router[j], moe_w_gate[j], moe_w_up[j], moe_w_down[j])
        gate = jax.nn.sigmoid(rms_norm(h, norm_ple_w[i]) @ ple_w_gate[i])
        h = h + gate * (p[i] @ ple_w_proj[i])
    return rms_norm(h, final_norm_w)
```

```python
import functools

import numpy as np
import jax
import jax.numpy as jnp
from jax import lax
from jax.experimental import pallas as pl
from jax.experimental.pallas import tpu as pltpu

F32 = jnp.float32
BF16 = jnp.bfloat16

EPS = 1e-6
HGRN_HEADS = 8
HGRN_HEAD_DIM = 128
GLA_HEADS = 4
GLA_HEAD_K = 128
GLA_HEAD_V = 256
GLA_GATE_RANK = 16
GLA_GATE_TEMP = 16.0
N_EXPERTS = 8
TOP_K = 2

LANES = 128
CHUNK = 128
N_LEVELS = 7
VMEM_LIMIT = 56 * 1024 * 1024

_N_MAIN = 6144
_COL_GG = 6144
_COL_GLR = 7168
_PROJ_COLS = 7680


def _cparams(sem):
    return pltpu.CompilerParams(dimension_semantics=sem, vmem_limit_bytes=VMEM_LIMIT)


def _sigmoid(x):
    return 1.0 / (1.0 + jnp.exp(-x))


def _rms_rows(x, w):
    ms = jnp.mean(x * x, axis=-1, keepdims=True)
    return x * lax.rsqrt(ms + EPS) * w


def _in_proj_kernel(x_ref, nw_ref, w1_ref, w2_ref, o_ref, xn_ref, *, n_main, row_chunk):
    j = pl.program_id(1)
    tm = x_ref.shape[0]

    @pl.when(j == 0)
    def _():
        def body(c, carry):
            r = pl.multiple_of(c * row_chunk, row_chunk)
            xn_ref[pl.ds(r, row_chunk), :] = _rms_rows(x_ref[pl.ds(r, row_chunk), :], nw_ref[...]).astype(BF16)
            return carry
        lax.fori_loop(0, tm // row_chunk, body, 0)

    @pl.when(j < n_main)
    def _():
        o_ref[...] = jnp.dot(xn_ref[...], w1_ref[...].astype(BF16),
                             preferred_element_type=F32).astype(o_ref.dtype)

    @pl.when(j >= n_main)
    def _():
        o_ref[...] = jnp.dot(xn_ref[...], w2_ref[...].astype(BF16),
                             preferred_element_type=F32).astype(o_ref.dtype)


def _in_proj(x, nw, w_in, layer, w_tail, *, tm=1024, tn=512):
    n, d = x.shape
    n_main = _N_MAIN // tn
    n_tail = w_tail.shape[1] // tn
    kern = functools.partial(_in_proj_kernel, n_main=n_main, row_chunk=128)
    return pl.pallas_call(
        kern,
        out_shape=jax.ShapeDtypeStruct((n, (n_main + n_tail) * tn), BF16),
        grid=(n // tm, n_main + n_tail),
        in_specs=[
            pl.BlockSpec((tm, d), lambda i, j: (i, 0)),
            pl.BlockSpec((1, d), lambda i, j: (0, 0)),
            pl.BlockSpec((None, d, tn), lambda i, j: (layer, 0, jnp.minimum(j, n_main - 1))),
            pl.BlockSpec((d, tn), lambda i, j: (0, jnp.maximum(j - n_main, 0))),
        ],
        out_specs=pl.BlockSpec((tm, tn), lambda i, j: (i, j)),
        scratch_shapes=[pltpu.VMEM((tm, d), BF16)],
        compiler_params=_cparams(("parallel", "arbitrary")),
        name="in_proj",
    )(x, nw.reshape(1, d), w_in, w_tail)


def _level_tables():
    c = CHUNK
    idx = np.arange(c)
    sums, masks = [], []
    s = c // 2
    while s >= 1:
        blk = idx // (2 * s)
        mid = blk * 2 * s + s - 1
        w = np.zeros((c, c), np.float32)
        for i in range(c):
            if i > mid[i]:
                w[i, mid[i] + 1:i + 1] = 1.0
            else:
                w[i, i + 1:mid[i] + 1] = 1.0
        sums.append(w)
        upper = (idx % (2 * s)) >= s
        masks.append(((blk[:, None] == blk[None, :]) & upper[:, None] & (~upper)[None, :]).astype(np.float32))
        s //= 2
    masks.append(np.eye(c, dtype=np.float32))
    sums.append(np.tril(np.ones((c, c), np.float32)))
    sums.append(np.triu(np.ones((c, c), np.float32), 1))
    sums = np.concatenate(sums, axis=0)
    return np.concatenate([sums, sums], axis=1), np.stack(masks)


def _gla_chunk(q, k, v_bf, g, sums_ref, masks_ref, st_ref):
    c = CHUNK
    g_hi = g.astype(BF16)
    g_lo = (g - g_hi.astype(F32)).astype(BF16)
    x = jnp.dot(sums_ref[...], jnp.concatenate([g_hi, g_lo], axis=0), preferred_element_type=F32)
    nt = (((1,), (1,)), ((), ()))
    a = masks_ref[N_LEVELS] * lax.dot_general(q.astype(BF16), k.astype(BF16), nt, preferred_element_type=F32)
    for l in range(N_LEVELS):
        e = jnp.exp(x[l * c:(l + 1) * c])
        a = a + masks_ref[l] * lax.dot_general((q * e).astype(BF16), (k * e).astype(BF16), nt,
                                               preferred_element_type=F32)
    g_cum = x[N_LEVELS * c:(N_LEVELS + 1) * c]
    g_rest = x[(N_LEVELS + 1) * c:(N_LEVELS + 2) * c]
    st = st_ref[...]
    o = jnp.dot(a.astype(BF16), v_bf, preferred_element_type=F32)
    o = o + lax.dot_general((q * jnp.exp(g_cum)).astype(BF16), st.astype(BF16), nt, preferred_element_type=F32)
    k_dec = (k * jnp.exp(g_rest)).astype(BF16)
    tn = (((0,), (0,)), ((), ()))
    st_ref[...] = st * jnp.exp(g_cum[c - 1:c, :]) + lax.dot_general(v_bf, k_dec, tn, preferred_element_type=F32)
    return o


def _head_norm(o, w):
    ms = jnp.mean(o * o, axis=-1, keepdims=True)
    return o * lax.rsqrt(ms + EPS) * w


def _hgrn_kernel(q_ref, f_ref, i_ref, g_ref, lb_ref, nw_ref, sums_ref, masks_ref, o_ref, st_ref):
    @pl.when(pl.program_id(2) == 0)
    def _():
        st_ref[...] = jnp.zeros_like(st_ref)

    lb = lb_ref[...]

    def body(ci, carry):
        r = pl.multiple_of(ci * CHUNK, CHUNK)
        rows = pl.ds(r, CHUNK)
        z = f_ref[rows, :].astype(F32)
        e = jnp.exp(-jnp.abs(z))
        rcp = 1.0 / (1.0 + e)
        pos = z >= 0.0
        sig = jnp.where(pos, rcp, e * rcp)
        nsig = jnp.where(pos, e * rcp, rcp)
        fgate = lb + (1.0 - lb) * sig
        g = jnp.log(jnp.maximum(fgate, 1e-38))
        k = (1.0 - lb) * nsig
        hq = q_ref[rows, :].astype(F32)
        q = hq * _sigmoid(hq)
        o = _gla_chunk(q, k, i_ref[rows, :], g, sums_ref, masks_ref, st_ref)
        gate = _sigmoid(g_ref[rows, :].astype(F32))
        o_ref[rows, :] = (_head_norm(o, nw_ref[...]) * gate).astype(o_ref.dtype)
        return carry

    lax.fori_loop(0, q_ref.shape[0] // CHUNK, body, 0)


def _gla_kernel(q_ref, k_ref, v_ref, g_ref, lr_ref, w2_ref, b_ref, nw_ref, sums_ref, masks_ref, o_ref, st_ref):
    @pl.when(pl.program_id(2) == 0)
    def _():
        st_ref[...] = jnp.zeros_like(st_ref)

    def body(ci, carry):
        r = pl.multiple_of(ci * CHUNK, CHUNK)
        rows = pl.ds(r, CHUNK)
        u = jnp.dot(lr_ref[rows, :], w2_ref[...], preferred_element_type=F32) + b_ref[...]
        g = (jnp.minimum(u, 0.0) - jnp.log(1.0 + jnp.exp(-jnp.abs(u)))) * (1.0 / GLA_GATE_TEMP)
        q = q_ref[rows, :].astype(F32) * (GLA_HEAD_K ** -0.5)
        k = k_ref[rows, :].astype(F32)
        o = _gla_chunk(q, k, v_ref[rows, :], g, sums_ref, masks_ref, st_ref)
        gg = g_ref[rows, :].astype(F32)
        o_ref[rows, :] = (_head_norm(o, nw_ref[...]) * (gg * _sigmoid(gg))).astype(o_ref.dtype)
        return carry

    lax.fori_loop(0, q_ref.shape[0] // CHUNK, body, 0)


def _const_spec(shape):
    nd = len(shape)
    return pl.BlockSpec(shape, lambda b, h, t: (0,) * nd)


def _hgrn(proj, lb, nw, sums, masks, *, batch, tb=512):
    n = proj.shape[0]
    nt = n // batch // tb
    dk = HGRN_HEAD_DIM
    hh = HGRN_HEADS

    def col(off):
        return pl.BlockSpec((tb, dk), lambda b, h, t: (b * nt + t, off + h))

    return pl.pallas_call(
        _hgrn_kernel,
        out_shape=jax.ShapeDtypeStruct((n, hh * dk), BF16),
        grid=(batch, hh, nt),
        in_specs=[col(0), col(hh), col(2 * hh), col(3 * hh),
                  pl.BlockSpec((1, dk), lambda b, h, t: (0, h)),
                  pl.BlockSpec((1, dk), lambda b, h, t: (0, h)),
                  _const_spec(sums.shape), _const_spec(masks.shape)],
        out_specs=pl.BlockSpec((tb, dk), lambda b, h, t: (b * nt + t, h)),
        scratch_shapes=[pltpu.VMEM((dk, dk), F32)],
        compiler_params=_cparams(("parallel", "parallel", "arbitrary")),
        name="hgrn",
    )(proj, proj, proj, proj, lb.reshape(1, -1), nw.reshape(1, -1), sums, masks)


def _gla(proj, w2p, b, nw, sums, masks, *, batch, tb=512):
    n = proj.shape[0]
    nt = n // batch // tb
    dk, dv, hh = GLA_HEAD_K, GLA_HEAD_V, GLA_HEADS
    q0 = 4096 // dk
    k0 = q0 + hh
    v0 = 5120 // dv
    g0 = _COL_GG // dv
    lr0 = _COL_GLR // LANES
    return pl.pallas_call(
        _gla_kernel,
        out_shape=jax.ShapeDtypeStruct((n, hh * dv), BF16),
        grid=(batch, hh, nt),
        in_specs=[pl.BlockSpec((tb, dk), lambda b, h, t: (b * nt + t, q0 + h)),
                  pl.BlockSpec((tb, dk), lambda b, h, t: (b * nt + t, k0 + h)),
                  pl.BlockSpec((tb, dv), lambda b, h, t: (b * nt + t, v0 + h)),
                  pl.BlockSpec((tb, dv), lambda b, h, t: (b * nt + t, g0 + h)),
                  pl.BlockSpec((tb, LANES), lambda b, h, t: (b * nt + t, lr0)),
                  pl.BlockSpec((LANES, dk), lambda b, h, t: (0, h)),
                  pl.BlockSpec((1, dk), lambda b, h, t: (0, h)),
                  pl.BlockSpec((1, dv), lambda b, h, t: (0, h)),
                  _const_spec(sums.shape), _const_spec(masks.shape)],
        out_specs=pl.BlockSpec((tb, dv), lambda b, h, t: (b * nt + t, h)),
        scratch_shapes=[pltpu.VMEM((dv, dk), F32)],
        compiler_params=_cparams(("parallel", "parallel", "arbitrary")),
        name="gla",
    )(proj, proj, proj, proj, proj, w2p, b.reshape(1, -1), nw.reshape(1, -1), sums, masks)


def _cast_rows(src_ref, dst_ref, row_chunk):
    def body(c, carry):
        r = pl.multiple_of(c * row_chunk, row_chunk)
        dst_ref[pl.ds(r, row_chunk), :] = src_ref[pl.ds(r, row_chunk), :].astype(dst_ref.dtype)
        return carry
    lax.fori_loop(0, src_ref.shape[0] // row_chunk, body, 0)


def _out_proj_kernel(oh_ref, og_ref, res_ref, w_ref, nw_ref, *rest, with_router):
    if with_router:
        rw_ref, h_ref, hn_ref, lg_ref, wb_ref = rest
    else:
        h_ref, hn_ref, wb_ref = rest

    @pl.when(pl.program_id(0) == 0)
    def _():
        _cast_rows(w_ref, wb_ref, 256)

    kh = oh_ref.shape[1]
    acc = jnp.dot(oh_ref[...], wb_ref[0:kh, :], preferred_element_type=F32)
    acc = acc + jnp.dot(og_ref[...], wb_ref[kh:, :], preferred_element_type=F32)
    h = res_ref[...] + acc
    h_ref[...] = h
    hn = _rms_rows(h, nw_ref[...])
    hn_hi = hn.astype(BF16)
    hn_ref[...] = hn_hi
    if with_router:
        hn_lo = (hn - hn_hi.astype(F32)).astype(BF16)
        rw = rw_ref[...]
        rw_hi = rw.astype(BF16)
        rw_lo = (rw - rw_hi.astype(F32)).astype(BF16)
        lg = jnp.dot(hn_hi, rw_hi, preferred_element_type=F32)
        lg = lg + jnp.dot(hn_lo, rw_hi, preferred_element_type=F32)
        lg = lg + jnp.dot(hn_hi, rw_lo, preferred_element_type=F32)
        lg_ref[...] = lg


def _out_proj(o_h, o_g, res, w_out, layer, nw, router_pad=None, *, tm=256):
    n, d = res.shape
    kh, kg = o_h.shape[1], o_g.shape[1]
    with_router = router_pad is not None
    row = lambda i: (i, 0)
    fixed = lambda i: (0, 0)
    in_specs = [pl.BlockSpec((tm, kh), row), pl.BlockSpec((tm, kg), row), pl.BlockSpec((tm, d), row),
                pl.BlockSpec((None, kh + kg, d), lambda i: (layer, 0, 0), pipeline_mode=pl.Buffered(1)),
                pl.BlockSpec((1, d), fixed)]
    args = [o_h, o_g, res, w_out, nw.reshape(1, d)]
    out_shape = [jax.ShapeDtypeStruct((n, d), F32), jax.ShapeDtypeStruct((n, d), BF16)]
    out_specs = [pl.BlockSpec((tm, d), row), pl.BlockSpec((tm, d), row)]
    if with_router:
        in_specs.append(pl.BlockSpec((d, LANES), fixed))
        args.append(router_pad)
        out_shape.append(jax.ShapeDtypeStruct((n, LANES), F32))
        out_specs.append(pl.BlockSpec((tm, LANES), row))
    return pl.pallas_call(
        functools.partial(_out_proj_kernel, with_router=with_router),
        out_shape=out_shape,
        grid=(n // tm,),
        in_specs=in_specs,
        out_specs=out_specs,
        scratch_shapes=[pltpu.VMEM((kh + kg, d), BF16)],
        compiler_params=_cparams(("arbitrary",)),
        name="out_proj",
    )(*args)


def _ffn_kernel(be_ref, nused_ref, x_ref, wg_ref, wu_ref, wd_ref, o_ref, acc_ref):
    i = pl.program_id(0)
    f = pl.program_id(1)

    @pl.when(f == 0)
    def _():
        acc_ref[...] = jnp.zeros_like(acc_ref)

    @pl.when(i < nused_ref[0])
    def _():
        x = x_ref[...]
        g = jnp.dot(x, wg_ref[0].astype(BF16), preferred_element_type=F32)
        u = jnp.dot(x, wu_ref[0].astype(BF16), preferred_element_type=F32)
        a = (g * _sigmoid(g) * u).astype(BF16)
        acc_ref[...] += jnp.dot(a, wd_ref[0].astype(BF16), preferred_element_type=F32)

    @pl.when(f == pl.num_programs(1) - 1)
    def _():
        o_ref[...] = acc_ref[...].astype(o_ref.dtype)


def _ffn(x, wg, wu, wd, block_e, nused, *, tm=1024, tf=256):
    r, d = x.shape
    ff = wg.shape[2]
    nf = ff // tf

    def f_eff(i, f, nu):
        return jnp.where(i < nu[0], f, nf - 1)

    grid_spec = pltpu.PrefetchScalarGridSpec(
        num_scalar_prefetch=2,
        grid=(r // tm, nf),
        in_specs=[
            pl.BlockSpec((tm, d), lambda i, f, be, nu: (i, 0)),
            pl.BlockSpec((1, d, tf), lambda i, f, be, nu: (be[i], 0, f_eff(i, f, nu))),
            pl.BlockSpec((1, d, tf), lambda i, f, be, nu: (be[i], 0, f_eff(i, f, nu))),
            pl.BlockSpec((1, tf, d), lambda i, f, be, nu: (be[i], f_eff(i, f, nu), 0)),
        ],
        out_specs=pl.BlockSpec((tm, d), lambda i, f, be, nu: (i, 0)),
        scratch_shapes=[pltpu.VMEM((tm, d), F32)],
    )
    return pl.pallas_call(
        _ffn_kernel,
        out_shape=jax.ShapeDtypeStruct((r, d), BF16),
        grid_spec=grid_spec,
        compiler_params=_cparams(("parallel", "arbitrary")),
        name="ffn",
    )(block_e, nused, x, wg, wu, wd)


def _ple_kernel(h_ref, y_ref, p_ref, nw_ref, wg_ref, wp_ref, fw_ref, o_ref, wgb_ref, wpb_ref, *, final):
    @pl.when(pl.program_id(0) == 0)
    def _():
        _cast_rows(wg_ref, wgb_ref, 256)
        _cast_rows(wp_ref, wpb_ref, 256)

    h2 = h_ref[...] + y_ref[...].astype(F32)
    hn = _rms_rows(h2, nw_ref[...]).astype(BF16)
    gate = _sigmoid(jnp.dot(hn, wgb_ref[...], preferred_element_type=F32))
    pp = jnp.dot(p_ref[...].astype(BF16), wpb_ref[...], preferred_element_type=F32)
    h3 = h2 + gate * pp
    if final:
        h3 = _rms_rows(h3, fw_ref[...])
    o_ref[...] = h3


def _ple(h1, y, p, layer, nw, w_gate, w_proj, final_w, *, final, tm=256):
    n, d = h1.shape
    pd = p.shape[2]
    row = lambda i: (i, 0)
    fixed = lambda i: (0, 0)
    return pl.pallas_call(
        functools.partial(_ple_kernel, final=final),
        out_shape=jax.ShapeDtypeStruct((n, d), F32),
        grid=(n // tm,),
        in_specs=[pl.BlockSpec((tm, d), row), pl.BlockSpec((tm, d), row),
                  pl.BlockSpec((None, tm, pd), lambda i: (layer, i, 0)),
                  pl.BlockSpec((1, d), fixed),
                  pl.BlockSpec((None, d, d), lambda i: (layer, 0, 0), pipeline_mode=pl.Buffered(1)),
                  pl.BlockSpec((None, pd, d), lambda i: (layer, 0, 0), pipeline_mode=pl.Buffered(1)),
                  pl.BlockSpec((1, d), fixed)],
        out_specs=pl.BlockSpec((tm, d), row),
        scratch_shapes=[pltpu.VMEM((d, d), BF16), pltpu.VMEM((pd, d), BF16)],
        compiler_params=_cparams(("arbitrary",)),
        name="ple",
    )(h1, y, p, nw.reshape(1, d), w_gate, w_proj, final_w.reshape(1, d))


def _route(logits, block):
    n = logits.shape[0]
    top_logit, top_e = lax.top_k(logits, TOP_K)
    top_w = jax.nn.softmax(top_logit, axis=-1)
    flat_e = top_e.reshape(-1)
    onehot = (flat_e[:, None] == jnp.arange(N_EXPERTS, dtype=flat_e.dtype)[None, :]).astype(jnp.int32)
    csum = jnp.cumsum(onehot, axis=0)
    counts = csum[-1]
    rank = jnp.sum((csum - onehot) * onehot, axis=1)
    padded = (counts + block - 1) // block * block
    padded_end = jnp.cumsum(padded)
    padded_start = padded_end - padded
    dest = padded_start[flat_e] + rank
    n_blocks = -(-(n * TOP_K) // block) + N_EXPERTS
    flat_tok = jnp.repeat(jnp.arange(n, dtype=jnp.int32), TOP_K)
    slot_tok = jnp.zeros((n_blocks * block,), jnp.int32).at[dest].set(flat_tok)
    block_e = jnp.minimum(jnp.searchsorted(padded_end, jnp.arange(n_blocks, dtype=jnp.int32) * block, side='right'),
                          N_EXPERTS - 1).astype(jnp.int32)
    nused = (padded_end[-1] // block).astype(jnp.int32).reshape(1)
    block_e = jnp.where(jnp.arange(n_blocks) < nused[0], block_e, block_e[jnp.maximum(nused[0] - 1, 0)])
    return slot_tok, dest.reshape(n, TOP_K), top_w, block_e, nused


def kernel(x, p, norm_mix_w, w_in, hgrn_lb_logits, hgrn_norm_w, gla_gate_w2, gla_gate_b, gla_norm_w, w_out,
           norm_ffn_w, dense_w_gate, dense_w_up, dense_w_down, moe_router, moe_w_gate, moe_w_up, moe_w_down,
           norm_ple_w, ple_w_gate, ple_w_proj, final_norm_w):
    batch, seq, d = x.shape
    depth = w_in.shape[0]
    n = batch * seq
    moe_block = 1024

    lbs = jnp.cumsum(jax.nn.softmax(hgrn_lb_logits.astype(F32), axis=0), axis=0)
    lbs = lbs - lbs[0]
    sums_np, masks_np = _level_tables()
    sums = jnp.asarray(sums_np, BF16)
    masks = jnp.asarray(masks_np, F32)

    h = x.reshape(n, d)
    p3 = p.reshape(depth, n, -1)
    moe_wg = moe_w_gate.reshape((-1,) + moe_w_gate.shape[2:])
    moe_wu = moe_w_up.reshape((-1,) + moe_w_up.shape[2:])
    moe_wd = moe_w_down.reshape((-1,) + moe_w_down.shape[2:])
    for i in range(depth):
        gg0 = _COL_GG + GLA_GATE_RANK
        w_tail = jnp.concatenate(
            [w_in[i][:, gg0:], w_in[i][:, _COL_GG:gg0],
             jnp.zeros((d, _PROJ_COLS - _COL_GLR - GLA_GATE_RANK), F32)], axis=1)
        proj = _in_proj(h, norm_mix_w[i], w_in, i, w_tail)
        o_h = _hgrn(proj, lbs[i], hgrn_norm_w[i], sums, masks, batch=batch)
        w2p = jnp.zeros((LANES, gla_gate_w2.shape[2]), F32).at[:GLA_GATE_RANK].set(gla_gate_w2[i]).astype(BF16)
        o_g = _gla(proj, w2p, gla_gate_b[i], gla_norm_w[i], sums, masks, batch=batch)
        j = i // 2
        if i % 2 == 0:
            h1, hn = _out_proj(o_h, o_g, h, w_out, i, norm_ffn_w[i])
            n_blk = n // moe_block
            y = _ffn(hn, dense_w_gate, dense_w_up, dense_w_down,
                     jnp.full((n_blk,), j, jnp.int32), jnp.full((1,), n_blk, jnp.int32), tm=moe_block)
        else:
            router_pad = jnp.zeros((d, LANES), F32).at[:, :N_EXPERTS].set(moe_router[j])
            h1, hn, logits = _out_proj(o_h, o_g, h, w_out, i, norm_ffn_w[i], router_pad)
            slot_tok, dest, top_w, block_e, nused = _route(logits[:, :N_EXPERTS], moe_block)
            xs = jnp.take(hn, slot_tok, axis=0)
            ys = _ffn(xs, moe_wg, moe_wu, moe_wd, block_e + j * N_EXPERTS, nused, tm=moe_block)
            y = (top_w[:, 0:1] * jnp.take(ys, dest[:, 0], axis=0).astype(F32)
                 + top_w[:, 1:2] * jnp.take(ys, dest[:, 1], axis=0).astype(F32)).astype(BF16)
        h = _ple(h1, y, p3, i, norm_ple_w[i], ple_w_gate, ple_w_proj, final_norm_w,
                 final=(i == depth - 1))
    return h.reshape(batch, seq, d)
```

```python
import functools

import numpy as np
import jax
import jax.numpy as jnp
from jax import lax
from jax.experimental import pallas as pl
from jax.experimental.pallas import tpu as pltpu

F32 = jnp.float32
BF16 = jnp.bfloat16

EPS = 1e-6
LOG2_E = 1.4426950408889634
HGRN_HEADS = 8
HGRN_HEAD_DIM = 128
GLA_HEADS = 4
GLA_HEAD_K = 128
GLA_HEAD_V = 256
GLA_GATE_RANK = 16
GLA_GATE_TEMP = 16.0
N_EXPERTS = 8
TOP_K = 2

LANES = 128
CHUNK = 128
N_LEVELS = 7
VMEM_LIMIT = 56 * 1024 * 1024

_N_MAIN = 6144
_COL_GG = 6144
_COL_GLR = 7168
_PROJ_COLS = 7680


def _cparams(sem):
    return pltpu.CompilerParams(dimension_semantics=sem, vmem_limit_bytes=VMEM_LIMIT)


def _sigmoid(x):
    return 1.0 / (1.0 + jnp.exp(-x))


def _rms_rows(x, w):
    ms = jnp.mean(x * x, axis=-1, keepdims=True)
    return x * lax.rsqrt(ms + EPS) * w


def _in_proj_kernel(x_ref, nw_ref, w1_ref, w2_ref, o_ref, xn_ref, *, n_main, row_chunk):
    j = pl.program_id(1)
    tm = x_ref.shape[0]

    @pl.when(j == 0)
    def _():
        def body(c, carry):
            r = pl.multiple_of(c * row_chunk, row_chunk)
            xn_ref[pl.ds(r, row_chunk), :] = _rms_rows(x_ref[pl.ds(r, row_chunk), :], nw_ref[...]).astype(BF16)
            return carry
        lax.fori_loop(0, tm // row_chunk, body, 0)

    @pl.when(j < n_main)
    def _():
        o_ref[...] = jnp.dot(xn_ref[...], w1_ref[...].astype(BF16),
                             preferred_element_type=F32).astype(o_ref.dtype)

    @pl.when(j >= n_main)
    def _():
        o_ref[...] = jnp.dot(xn_ref[...], w2_ref[...].astype(BF16),
                             preferred_element_type=F32).astype(o_ref.dtype)


def _in_proj(x, nw, w_in, layer, w_tail, *, tm=1024, tn=512):
    n, d = x.shape
    n_main = _N_MAIN // tn
    n_tail = w_tail.shape[1] // tn
    kern = functools.partial(_in_proj_kernel, n_main=n_main, row_chunk=128)
    return pl.pallas_call(
        kern,
        out_shape=jax.ShapeDtypeStruct((n, (n_main + n_tail) * tn), BF16),
        grid=(n // tm, n_main + n_tail),
        in_specs=[
            pl.BlockSpec((tm, d), lambda i, j: (i, 0)),
            pl.BlockSpec((1, d), lambda i, j: (0, 0)),
            pl.BlockSpec((None, d, tn), lambda i, j: (layer, 0, jnp.minimum(j, n_main - 1))),
            pl.BlockSpec((d, tn), lambda i, j: (0, jnp.maximum(j - n_main, 0))),
        ],
        out_specs=pl.BlockSpec((tm, tn), lambda i, j: (i, j)),
        scratch_shapes=[pltpu.VMEM((tm, d), BF16)],
        compiler_params=_cparams(("parallel", "arbitrary")),
        name="in_proj",
    )(x, nw.reshape(1, d), w_in, w_tail)


def _level_tables():
    c = CHUNK
    idx = np.arange(c)
    masks = []
    s = c // 2
    while s >= 1:
        blk = idx // (2 * s)
        upper = (idx % (2 * s)) >= s
        masks.append(((blk[:, None] == blk[None, :]) & upper[:, None] & (~upper)[None, :]).astype(np.float32))
        s //= 2
    masks.append(np.eye(c, dtype=np.float32))
    tri = np.tril(np.ones((c, c), np.float32))
    return np.concatenate([tri, tri], axis=1), np.stack(masks)


def _level_exponents(g_cum):
    c, dk = g_cum.shape
    row = lax.broadcasted_iota(jnp.int32, (c, dk), 0)
    out = []
    s = c // 2
    while s >= 4:
        nb = c // (2 * s)
        mid = g_cum.reshape(nb, 2 * s, dk)[:, s - 1:s, :]
        mid = jnp.broadcast_to(mid, (nb, 2 * s, dk)).reshape(c, dk)
        out.append(-jnp.abs(g_cum - mid))
        s //= 2
    up1 = pltpu.roll(g_cum, 1, 0)
    up2 = pltpu.roll(g_cum, 2, 0)
    dn1 = pltpu.roll(g_cum, c - 1, 0)
    r4 = row % 4
    mid2 = jnp.where(r4 == 0, dn1, jnp.where(r4 == 1, g_cum, jnp.where(r4 == 2, up1, up2)))
    out.append(-jnp.abs(g_cum - mid2))
    mid1 = jnp.where(row % 2 == 1, up1, g_cum)
    out.append(-jnp.abs(g_cum - mid1))
    return out


_NT = (((1,), (1,)), ((), ()))
_TN = (((0,), (0,)), ((), ()))


def _chunk_local(q, k, v_bf, g, tri_ref, masks_ref):
    c = CHUNK
    g_hi = g.astype(BF16)
    g_lo = (g - g_hi.astype(F32)).astype(BF16)
    g_cum = jnp.dot(tri_ref[...], jnp.concatenate([g_hi, g_lo], axis=0), preferred_element_type=F32) * LOG2_E
    a = masks_ref[N_LEVELS] * lax.dot_general(q.astype(BF16), k.astype(BF16), _NT, preferred_element_type=F32)
    for l, ex in enumerate(_level_exponents(g_cum)):
        e = jnp.exp2(ex)
        a = a + masks_ref[l] * lax.dot_general((q * e).astype(BF16), (k * e).astype(BF16), _NT,
                                               preferred_element_type=F32)
    g_last = g_cum[c - 1:c, :]
    o_intra = jnp.dot(a.astype(BF16), v_bf, preferred_element_type=F32)
    q_dec = (q * jnp.exp2(g_cum)).astype(BF16)
    k_dec = (k * jnp.exp2(g_last - g_cum)).astype(BF16)
    kv = lax.dot_general(v_bf, k_dec, _TN, preferred_element_type=F32)
    return o_intra, q_dec, kv, jnp.exp2(g_last)


def _head_norm(o, w):
    ms = jnp.mean(o * o, axis=-1, keepdims=True)
    return o * lax.rsqrt(ms + EPS) * w


def _scan_block(n_chunks, local_fn, gate_fn, nw_ref, o_ref, st_ref):
    parts = [local_fn(ci) for ci in range(n_chunks)]
    st = st_ref[...]
    for ci, (o_intra, q_dec, kv, decay) in enumerate(parts):
        o = o_intra + lax.dot_general(q_dec, st.astype(BF16), _NT, preferred_element_type=F32)
        rows = pl.ds(ci * CHUNK, CHUNK)
        o_ref[rows, :] = (_head_norm(o, nw_ref[...]) * gate_fn(rows)).astype(o_ref.dtype)
        st = st * decay + kv
    st_ref[...] = st


def _hgrn_kernel(q_ref, f_ref, i_ref, g_ref, lb_ref, nw_ref, tri_ref, masks_ref, o_ref, st_ref):
    @pl.when(pl.program_id(2) == 0)
    def _():
        st_ref[...] = jnp.zeros_like(st_ref)

    lb = lb_ref[...]

    def local(ci):
        rows = pl.ds(ci * CHUNK, CHUNK)
        fgate = lb + (1.0 - lb) * _sigmoid(f_ref[rows, :].astype(F32))
        g = jnp.log(jnp.maximum(fgate, 1e-38))
        k = 1.0 - fgate
        hq = q_ref[rows, :].astype(F32)
        return _chunk_local(hq * _sigmoid(hq), k, i_ref[rows, :], g, tri_ref, masks_ref)

    def gate(rows):
        return _sigmoid(g_ref[rows, :].astype(F32))

    _scan_block(q_ref.shape[0] // CHUNK, local, gate, nw_ref, o_ref, st_ref)


def _gla_kernel(q_ref, k_ref, v_ref, g_ref, lr_ref, w2_ref, b_ref, nw_ref, tri_ref, masks_ref, o_ref, st_ref):
    @pl.when(pl.program_id(2) == 0)
    def _():
        st_ref[...] = jnp.zeros_like(st_ref)

    def local(ci):
        rows = pl.ds(ci * CHUNK, CHUNK)
        u = jnp.dot(lr_ref[rows, :], w2_ref[...], preferred_element_type=F32) + b_ref[...]
        g = (jnp.minimum(u, 0.0) - jnp.log(1.0 + jnp.exp(-jnp.abs(u)))) * (1.0 / GLA_GATE_TEMP)
        q = q_ref[rows, :].astype(F32) * (GLA_HEAD_K ** -0.5)
        return _chunk_local(q, k_ref[rows, :].astype(F32), v_ref[rows, :], g, tri_ref, masks_ref)

    def gate(rows):
        gg = g_ref[rows, :].astype(F32)
        return gg * _sigmoid(gg)

    _scan_block(q_ref.shape[0] // CHUNK, local, gate, nw_ref, o_ref, st_ref)


def _const_spec(shape):
    nd = len(shape)
    return pl.BlockSpec(shape, lambda b, h, t: (0,) * nd)


def _hgrn(proj, lb, nw, sums, masks, *, batch, tb=512):
    n = proj.shape[0]
    nt = n // batch // tb
    dk = HGRN_HEAD_DIM
    hh = HGRN_HEADS

    def col(off):
        return pl.BlockSpec((tb, dk), lambda b, h, t: (b * nt + t, off + h))

    return pl.pallas_call(
        _hgrn_kernel,
        out_shape=jax.ShapeDtypeStruct((n, hh * dk), BF16),
        grid=(batch, hh, nt),
        in_specs=[col(0), col(hh), col(2 * hh), col(3 * hh),
                  pl.BlockSpec((1, dk), lambda b, h, t: (0, h)),
                  pl.BlockSpec((1, dk), lambda b, h, t: (0, h)),
                  _const_spec(sums.shape), _const_spec(masks.shape)],
        out_specs=pl.BlockSpec((tb, dk), lambda b, h, t: (b * nt + t, h)),
        scratch_shapes=[pltpu.VMEM((dk, dk), F32)],
        compiler_params=_cparams(("parallel", "parallel", "arbitrary")),
        name="hgrn",
    )(proj, proj, proj, proj, lb.reshape(1, -1), nw.reshape(1, -1), sums, masks)


def _gla(proj, w2p, b, nw, sums, masks, *, batch, tb=512):
    n = proj.shape[0]
    nt = n // batch // tb
    dk, dv, hh = GLA_HEAD_K, GLA_HEAD_V, GLA_HEADS
    q0 = 4096 // dk
    k0 = q0 + hh
    v0 = 5120 // dv
    g0 = _COL_GG // dv
    lr0 = _COL_GLR // LANES
    return pl.pallas_call(
        _gla_kernel,
        out_shape=jax.ShapeDtypeStruct((n, hh * dv), BF16),
        grid=(batch, hh, nt),
        in_specs=[pl.BlockSpec((tb, dk), lambda b, h, t: (b * nt + t, q0 + h)),
                  pl.BlockSpec((tb, dk), lambda b, h, t: (b * nt + t, k0 + h)),
                  pl.BlockSpec((tb, dv), lambda b, h, t: (b * nt + t, v0 + h)),
                  pl.BlockSpec((tb, dv), lambda b, h, t: (b * nt + t, g0 + h)),
                  pl.BlockSpec((tb, LANES), lambda b, h, t: (b * nt + t, lr0)),
                  pl.BlockSpec((LANES, dk), lambda b, h, t: (0, h)),
                  pl.BlockSpec((1, dk), lambda b, h, t: (0, h)),
                  pl.BlockSpec((1, dv), lambda b, h, t: (0, h)),
                  _const_spec(sums.shape), _const_spec(masks.shape)],
        out_specs=pl.BlockSpec((tb, dv), lambda b, h, t: (b * nt + t, h)),
        scratch_shapes=[pltpu.VMEM((dv, dk), F32)],
        compiler_params=_cparams(("parallel", "parallel", "arbitrary")),
        name="gla",
    )(proj, proj, proj, proj, proj, w2p, b.reshape(1, -1), nw.reshape(1, -1), sums, masks)


def _cast_rows(src_ref, dst_ref, row_chunk):
    def body(c, carry):
        r = pl.multiple_of(c * row_chunk, row_chunk)
        dst_ref[pl.ds(r, row_chunk), :] = src_ref[pl.ds(r, row_chunk), :].astype(dst_ref.dtype)
        return carry
    lax.fori_loop(0, src_ref.shape[0] // row_chunk, body, 0)


def _out_proj_kernel(oh_ref, og_ref, res_ref, w_ref, nw_ref, *rest, with_router):
    if with_router:
        rw_ref, h_ref, hn_ref, lg_ref, wb_ref = rest
    else:
        h_ref, hn_ref, wb_ref = rest

    @pl.when(pl.program_id(0) == 0)
    def _():
        _cast_rows(w_ref, wb_ref, 256)

    kh = oh_ref.shape[1]
    acc = jnp.dot(oh_ref[...], wb_ref[0:kh, :], preferred_element_type=F32)
    acc = acc + jnp.dot(og_ref[...], wb_ref[kh:, :], preferred_element_type=F32)
    h = res_ref[...] + acc
    h_ref[...] = h
    hn = _rms_rows(h, nw_ref[...])
    hn_hi = hn.astype(BF16)
    hn_ref[...] = hn_hi
    if with_router:
        hn_lo = (hn - hn_hi.astype(F32)).astype(BF16)
        rw = rw_ref[...]
        rw_hi = rw.astype(BF16)
        rw_lo = (rw - rw_hi.astype(F32)).astype(BF16)
        lg = jnp.dot(hn_hi, rw_hi, preferred_element_type=F32)
        lg = lg + jnp.dot(hn_lo, rw_hi, preferred_element_type=F32)
        lg = lg + jnp.dot(hn_hi, rw_lo, preferred_element_type=F32)
        lg_ref[...] = lg


def _out_proj(o_h, o_g, res, w_out, layer, nw, router_pad=None, *, tm=256):
    n, d = res.shape
    kh, kg = o_h.shape[1], o_g.shape[1]
    with_router = router_pad is not None
    row = lambda i: (i, 0)
    fixed = lambda i: (0, 0)
    in_specs = [pl.BlockSpec((tm, kh), row), pl.BlockSpec((tm, kg), row), pl.BlockSpec((tm, d), row),
                pl.BlockSpec((None, kh + kg, d), lambda i: (layer, 0, 0), pipeline_mode=pl.Buffered(1)),
                pl.BlockSpec((1, d), fixed)]
    args = [o_h, o_g, res, w_out, nw.reshape(1, d)]
    out_shape = [jax.ShapeDtypeStruct((n, d), F32), jax.ShapeDtypeStruct((n, d), BF16)]
    out_specs = [pl.BlockSpec((tm, d), row), pl.BlockSpec((tm, d), row)]
    if with_router:
        in_specs.append(pl.BlockSpec((d, LANES), fixed))
        args.append(router_pad)
        out_shape.append(jax.ShapeDtypeStruct((n, LANES), F32))
        out_specs.append(pl.BlockSpec((tm, LANES), row))
    return pl.pallas_call(
        functools.partial(_out_proj_kernel, with_router=with_router),
        out_shape=out_shape,
        grid=(n // tm,),
        in_specs=in_specs,
        out_specs=out_specs,
        scratch_shapes=[pltpu.VMEM((kh + kg, d), BF16)],
        compiler_params=_cparams(("arbitrary",)),
        name="out_proj",
    )(*args)


def _ffn_kernel(be_ref, nused_ref, x_ref, wg_ref, wu_ref, wd_ref, o_ref, acc_ref):
    i = pl.program_id(0)
    f = pl.program_id(1)

    @pl.when(f == 0)
    def _():
        acc_ref[...] = jnp.zeros_like(acc_ref)

    @pl.when(i < nused_ref[0])
    def _():
        x = x_ref[...]
        g = jnp.dot(x, wg_ref[0].astype(BF16), preferred_element_type=F32)
        u = jnp.dot(x, wu_ref[0].astype(BF16), preferred_element_type=F32)
        a = (g * _sigmoid(g) * u).astype(BF16)
        acc_ref[...] += jnp.dot(a, wd_ref[0].astype(BF16), preferred_element_type=F32)

    @pl.when(f == pl.num_programs(1) - 1)
    def _():
        o_ref[...] = acc_ref[...].astype(o_ref.dtype)


def _ffn(x, wg, wu, wd, block_e, nused, *, tm=1024, tf=256):
    r, d = x.shape
    ff = wg.shape[2]
    nf = ff // tf

    def f_eff(i, f, nu):
        return jnp.where(i < nu[0], f, nf - 1)

    grid_spec = pltpu.PrefetchScalarGridSpec(
        num_scalar_prefetch=2,
        grid=(r // tm, nf),
        in_specs=[
            pl.BlockSpec((tm, d), lambda i, f, be, nu: (i, 0)),
            pl.BlockSpec((1, d, tf), lambda i, f, be, nu: (be[i], 0, f_eff(i, f, nu))),
            pl.BlockSpec((1, d, tf), lambda i, f, be, nu: (be[i], 0, f_eff(i, f, nu))),
            pl.BlockSpec((1, tf, d), lambda i, f, be, nu: (be[i], f_eff(i, f, nu), 0)),
        ],
        out_specs=pl.BlockSpec((tm, d), lambda i, f, be, nu: (i, 0)),
        scratch_shapes=[pltpu.VMEM((tm, d), F32)],
    )
    return pl.pallas_call(
        _ffn_kernel,
        out_shape=jax.ShapeDtypeStruct((r, d), BF16),
        grid_spec=grid_spec,
        compiler_params=_cparams(("parallel", "arbitrary")),
        name="ffn",
    )(block_e, nused, x, wg, wu, wd)


def _ple_kernel(h_ref, y_ref, p_ref, nw_ref, wg_ref, wp_ref, fw_ref, o_ref, wgb_ref, wpb_ref, *, final):
    @pl.when(pl.program_id(0) == 0)
    def _():
        _cast_rows(wg_ref, wgb_ref, 256)
        _cast_rows(wp_ref, wpb_ref, 256)

    h2 = h_ref[...] + y_ref[...].astype(F32)
    hn = _rms_rows(h2, nw_ref[...]).astype(BF16)
    gate = _sigmoid(jnp.dot(hn, wgb_ref[...], preferred_element_type=F32))
    pp = jnp.dot(p_ref[...].astype(BF16), wpb_ref[...], preferred_element_type=F32)
    h3 = h2 + gate * pp
    if final:
        h3 = _rms_rows(h3, fw_ref[...])
    o_ref[...] = h3


def _ple(h1, y, p, layer, nw, w_gate, w_proj, final_w, *, final, tm=256):
    n, d = h1.shape
    pd = p.shape[2]
    row = lambda i: (i, 0)
    fixed = lambda i: (0, 0)
    return pl.pallas_call(
        functools.partial(_ple_kernel, final=final),
        out_shape=jax.ShapeDtypeStruct((n, d), F32),
        grid=(n // tm,),
        in_specs=[pl.BlockSpec((tm, d), row), pl.BlockSpec((tm, d), row),
                  pl.BlockSpec((None, tm, pd), lambda i: (layer, i, 0)),
                  pl.BlockSpec((1, d), fixed),
                  pl.BlockSpec((None, d, d), lambda i: (layer, 0, 0), pipeline_mode=pl.Buffered(1)),
                  pl.BlockSpec((None, pd, d), lambda i: (layer, 0, 0), pipeline_mode=pl.Buffered(1)),
                  pl.BlockSpec((1, d), fixed)],
        out_specs=pl.BlockSpec((tm, d), row),
        scratch_shapes=[pltpu.VMEM((d, d), BF16), pltpu.VMEM((pd, d), BF16)],
        compiler_params=_cparams(("arbitrary",)),
        name="ple",
    )(h1, y, p, nw.reshape(1, d), w_gate, w_proj, final_w.reshape(1, d))


def _route(logits, block):
    n = logits.shape[0]
    top_logit, top_e = lax.top_k(logits, TOP_K)
    top_w = jax.nn.softmax(top_logit, axis=-1)
    flat_e = top_e.reshape(-1)
    onehot = (flat_e[:, None] == jnp.arange(N_EXPERTS, dtype=flat_e.dtype)[None, :]).astype(jnp.int32)
    csum = jnp.cumsum(onehot, axis=0)
    counts = csum[-1]
    rank = jnp.sum((csum - onehot) * onehot, axis=1)
    padded = (counts + block - 1) // block * block
    padded_end = jnp.cumsum(padded)
    padded_start = padded_end - padded
    dest = padded_start[flat_e] + rank
    n_blocks = -(-(n * TOP_K) // block) + N_EXPERTS
    flat_tok = jnp.repeat(jnp.arange(n, dtype=jnp.int32), TOP_K)
    slot_tok = jnp.zeros((n_blocks * block,), jnp.int32).at[dest].set(flat_tok)
    block_e = jnp.minimum(jnp.searchsorted(padded_end, jnp.arange(n_blocks, dtype=jnp.int32) * block, side='right'),
                          N_EXPERTS - 1).astype(jnp.int32)
    nused = (padded_end[-1] // block).astype(jnp.int32).reshape(1)
    block_e = jnp.where(jnp.arange(n_blocks) < nused[0], block_e, block_e[jnp.maximum(nused[0] - 1, 0)])
    return slot_tok, dest.reshape(n, TOP_K), top_w, block_e, nused


def kernel(x, p, norm_mix_w, w_in, hgrn_lb_logits, hgrn_norm_w, gla_gate_w2, gla_gate_b, gla_norm_w, w_out,
           norm_ffn_w, dense_w_gate, dense_w_up, dense_w_down, moe_router, moe_w_gate, moe_w_up, moe_w_down,
           norm_ple_w, ple_w_gate, ple_w_proj, final_norm_w):
    batch, seq, d = x.shape
    depth = w_in.shape[0]
    n = batch * seq
    moe_block = 1024

    lbs = jnp.cumsum(jax.nn.softmax(hgrn_lb_logits.astype(F32), axis=0), axis=0)
    lbs = lbs - lbs[0]
    sums_np, masks_np = _level_tables()
    sums = jnp.asarray(sums_np, BF16)
    masks = jnp.asarray(masks_np, F32)

    h = x.reshape(n, d)
    p3 = p.reshape(depth, n, -1)
    moe_wg = moe_w_gate.reshape((-1,) + moe_w_gate.shape[2:])
    moe_wu = moe_w_up.reshape((-1,) + moe_w_up.shape[2:])
    moe_wd = moe_w_down.reshape((-1,) + moe_w_down.shape[2:])
    for i in range(depth):
        gg0 = _COL_GG + GLA_GATE_RANK
        w_tail = jnp.concatenate(
            [w_in[i][:, gg0:], w_in[i][:, _COL_GG:gg0],
             jnp.zeros((d, _PROJ_COLS - _COL_GLR - GLA_GATE_RANK), F32)], axis=1)
        proj = _in_proj(h, norm_mix_w[i], w_in, i, w_tail)
        o_h = _hgrn(proj, lbs[i], hgrn_norm_w[i], sums, masks, batch=batch)
        w2p = jnp.zeros((LANES, gla_gate_w2.shape[2]), F32).at[:GLA_GATE_RANK].set(gla_gate_w2[i]).astype(BF16)
        o_g = _gla(proj, w2p, gla_gate_b[i], gla_norm_w[i], sums, masks, batch=batch)
        j = i // 2
        if i % 2 == 0:
            h1, hn = _out_proj(o_h, o_g, h, w_out, i, norm_ffn_w[i])
            n_blk = n // moe_block
            y = _ffn(hn, dense_w_gate, dense_w_up, dense_w_down,
                     jnp.full((n_blk,), j, jnp.int32), jnp.full((1,), n_blk, jnp.int32), tm=moe_block)
        else:
            router_pad = jnp.zeros((d, LANES), F32).at[:, :N_EXPERTS].set(moe_router[j])
            h1, hn, logits = _out_proj(o_h, o_g, h, w_out, i, norm_ffn_w[i], router_pad)
            slot_tok, dest, top_w, block_e, nused = _route(logits[:, :N_EXPERTS], moe_block)
            xs = jnp.take(hn, slot_tok, axis=0)
            ys = _ffn(xs, moe_wg, moe_wu, moe_wd, block_e + j * N_EXPERTS, nused, tm=moe_block)
            y = (top_w[:, 0:1] * jnp.take(ys, dest[:, 0], axis=0).astype(F32)
                 + top_w[:, 1:2] * jnp.take(ys, dest[:, 1], axis=0).astype(F32)).astype(BF16)
        h = _ple(h1, y, p3, i, norm_ple_w[i], ple_w_gate, ple_w_proj, final_norm_w,
                 final=(i == depth - 1))
    return h.reshape(batch, seq, d)
```

```python
import functools

import numpy as np
import jax
import jax.numpy as jnp
from jax import lax
from jax.experimental import pallas as pl
from jax.experimental.pallas import tpu as pltpu

F32 = jnp.float32
BF16 = jnp.bfloat16
U32 = jnp.uint32

EPS = 1e-6
LOG2_E = 1.4426950408889634
HGRN_HEADS = 8
HGRN_HEAD_DIM = 128
GLA_HEADS = 4
GLA_HEAD_K = 128
GLA_HEAD_V = 256
GLA_GATE_RANK = 16
GLA_GATE_TEMP = 16.0
N_EXPERTS = 8
TOP_K = 2

LANES = 128
CHUNK = 128
N_LEVELS = 7
VMEM_LIMIT = 56 * 1024 * 1024

_N_MAIN = 6144
_COL_GG = 6144
_COL_GLR = 7168
_PROJ_COLS = 7680


def _cparams(sem):
    return pltpu.CompilerParams(dimension_semantics=sem, vmem_limit_bytes=VMEM_LIMIT)


def _sigmoid(x):
    return 1.0 / (1.0 + jnp.exp(-x))


def _rms_rows(x, w):
    ms = jnp.mean(x * x, axis=-1, keepdims=True)
    return x * lax.rsqrt(ms + EPS) * w


def _in_proj_kernel(x_ref, nw_ref, w1_ref, w2_ref, o_ref, xn_ref, *, n_main, row_chunk):
    j = pl.program_id(1)
    tm = x_ref.shape[0]

    @pl.when(j == 0)
    def _():
        def body(c, carry):
            r = pl.multiple_of(c * row_chunk, row_chunk)
            xn_ref[pl.ds(r, row_chunk), :] = _rms_rows(x_ref[pl.ds(r, row_chunk), :], nw_ref[...]).astype(BF16)
            return carry
        lax.fori_loop(0, tm // row_chunk, body, 0)

    @pl.when(j < n_main)
    def _():
        o_ref[...] = jnp.dot(xn_ref[...], w1_ref[...].astype(BF16),
                             preferred_element_type=F32).astype(o_ref.dtype)

    @pl.when(j >= n_main)
    def _():
        o_ref[...] = jnp.dot(xn_ref[...], w2_ref[...].astype(BF16),
                             preferred_element_type=F32).astype(o_ref.dtype)


def _in_proj(x, nw, w_in, layer, w_tail, *, tm=1024, tn=512):
    n, d = x.shape
    n_main = _N_MAIN // tn
    n_tail = w_tail.shape[1] // tn
    kern = functools.partial(_in_proj_kernel, n_main=n_main, row_chunk=128)
    return pl.pallas_call(
        kern,
        out_shape=jax.ShapeDtypeStruct((n, (n_main + n_tail) * tn), BF16),
        grid=(n // tm, n_main + n_tail),
        in_specs=[
            pl.BlockSpec((tm, d), lambda i, j: (i, 0)),
            pl.BlockSpec((1, d), lambda i, j: (0, 0)),
            pl.BlockSpec((None, d, tn), lambda i, j: (layer, 0, jnp.minimum(j, n_main - 1))),
            pl.BlockSpec((d, tn), lambda i, j: (0, jnp.maximum(j - n_main, 0))),
        ],
        out_specs=pl.BlockSpec((tm, tn), lambda i, j: (i, j)),
        scratch_shapes=[pltpu.VMEM((tm, d), BF16)],
        compiler_params=_cparams(("parallel", "arbitrary")),
        name="in_proj",
    )(x, nw.reshape(1, d), w_in, w_tail)


def _level_tables():
    c = CHUNK
    idx = np.arange(c)
    masks = []
    s = c // 2
    while s >= 1:
        blk = idx // (2 * s)
        upper = (idx % (2 * s)) >= s
        masks.append(((blk[:, None] == blk[None, :]) & upper[:, None] & (~upper)[None, :]).astype(np.float32))
        s //= 2
    masks.append(np.eye(c, dtype=np.float32))
    tri = np.tril(np.ones((c, c), np.float32))
    return np.concatenate([tri, tri], axis=1), np.stack(masks)


def _level_exponents(g_cum):
    c, dk = g_cum.shape
    row = lax.broadcasted_iota(jnp.int32, (c, dk), 0)
    out = []
    s = c // 2
    while s >= 4:
        nb = c // (2 * s)
        mid = g_cum.reshape(nb, 2 * s, dk)[:, s - 1:s, :]
        mid = jnp.broadcast_to(mid, (nb, 2 * s, dk)).reshape(c, dk)
        out.append(-jnp.abs(g_cum - mid))
        s //= 2
    up1 = pltpu.roll(g_cum, 1, 0)
    up2 = pltpu.roll(g_cum, 2, 0)
    dn1 = pltpu.roll(g_cum, c - 1, 0)
    r4 = row % 4
    mid2 = jnp.where(r4 == 0, dn1, jnp.where(r4 == 1, g_cum, jnp.where(r4 == 2, up1, up2)))
    out.append(-jnp.abs(g_cum - mid2))
    mid1 = jnp.where(row % 2 == 1, up1, g_cum)
    out.append(-jnp.abs(g_cum - mid1))
    return out


_NT = (((1,), (1,)), ((), ()))
_TN = (((0,), (0,)), ((), ()))


def _chunk_local(q, k, v_bf, g, tri_ref, masks_ref):
    c = CHUNK
    g_hi = g.astype(BF16)
    g_lo = (g - g_hi.astype(F32)).astype(BF16)
    g_cum = jnp.dot(tri_ref[...], jnp.concatenate([g_hi, g_lo], axis=0), preferred_element_type=F32) * LOG2_E
    a = masks_ref[N_LEVELS] * lax.dot_general(q.astype(BF16), k.astype(BF16), _NT, preferred_element_type=F32)
    for l, ex in enumerate(_level_exponents(g_cum)):
        e = jnp.exp2(ex)
        a = a + masks_ref[l] * lax.dot_general((q * e).astype(BF16), (k * e).astype(BF16), _NT,
                                               preferred_element_type=F32)
    g_last = g_cum[c - 1:c, :]
    o_intra = jnp.dot(a.astype(BF16), v_bf, preferred_element_type=F32)
    q_dec = (q * jnp.exp2(g_cum)).astype(BF16)
    k_dec = (k * jnp.exp2(g_last - g_cum)).astype(BF16)
    kv = lax.dot_general(v_bf, k_dec, _TN, preferred_element_type=F32)
    return o_intra, q_dec, kv, jnp.exp2(g_last)


def _head_norm(o, w):
    ms = jnp.mean(o * o, axis=-1, keepdims=True)
    return o * lax.rsqrt(ms + EPS) * w


def _scan_block(n_chunks, local_fn, gate_fn, nw_ref, o_ref, st_ref):
    parts = [local_fn(ci) for ci in range(n_chunks)]
    st = st_ref[...]
    for ci, (o_intra, q_dec, kv, decay) in enumerate(parts):
        o = o_intra + lax.dot_general(q_dec, st.astype(BF16), _NT, preferred_element_type=F32)
        rows = pl.ds(ci * CHUNK, CHUNK)
        o_ref[rows, :] = (_head_norm(o, nw_ref[...]) * gate_fn(rows)).astype(o_ref.dtype)
        st = st * decay + kv
    st_ref[...] = st


def _hgrn_kernel(q_ref, f_ref, i_ref, g_ref, lb_ref, nw_ref, tri_ref, masks_ref, o_ref, st_ref):
    @pl.when(pl.program_id(2) == 0)
    def _():
        st_ref[...] = jnp.zeros_like(st_ref)

    lb = lb_ref[...]

    def local(ci):
        rows = pl.ds(ci * CHUNK, CHUNK)
        fgate = lb + (1.0 - lb) * _sigmoid(f_ref[rows, :].astype(F32))
        g = jnp.log(jnp.maximum(fgate, 1e-38))
        k = 1.0 - fgate
        hq = q_ref[rows, :].astype(F32)
        return _chunk_local(hq * _sigmoid(hq), k, i_ref[rows, :], g, tri_ref, masks_ref)

    def gate(rows):
        return _sigmoid(g_ref[rows, :].astype(F32))

    _scan_block(q_ref.shape[0] // CHUNK, local, gate, nw_ref, o_ref, st_ref)


def _gla_kernel(q_ref, k_ref, v_ref, g_ref, lr_ref, w2_ref, b_ref, nw_ref, tri_ref, masks_ref, o_ref, st_ref):
    @pl.when(pl.program_id(2) == 0)
    def _():
        st_ref[...] = jnp.zeros_like(st_ref)

    def local(ci):
        rows = pl.ds(ci * CHUNK, CHUNK)
        u = jnp.dot(lr_ref[rows, :], w2_ref[...], preferred_element_type=F32) + b_ref[...]
        g = (jnp.minimum(u, 0.0) - jnp.log(1.0 + jnp.exp(-jnp.abs(u)))) * (1.0 / GLA_GATE_TEMP)
        q = q_ref[rows, :].astype(F32) * (GLA_HEAD_K ** -0.5)
        return _chunk_local(q, k_ref[rows, :].astype(F32), v_ref[rows, :], g, tri_ref, masks_ref)

    def gate(rows):
        gg = g_ref[rows, :].astype(F32)
        return gg * _sigmoid(gg)

    _scan_block(q_ref.shape[0] // CHUNK, local, gate, nw_ref, o_ref, st_ref)


def _const_spec(shape):
    nd = len(shape)
    return pl.BlockSpec(shape, lambda b, h, t: (0,) * nd)


def _hgrn(proj, lb, nw, tri, masks, *, batch, tb=512):
    n = proj.shape[0]
    nt = n // batch // tb
    dk = HGRN_HEAD_DIM
    hh = HGRN_HEADS

    def col(off):
        return pl.BlockSpec((tb, dk), lambda b, h, t: (b * nt + t, off + h))

    return pl.pallas_call(
        _hgrn_kernel,
        out_shape=jax.ShapeDtypeStruct((n, hh * dk), BF16),
        grid=(batch, hh, nt),
        in_specs=[col(0), col(hh), col(2 * hh), col(3 * hh),
                  pl.BlockSpec((1, dk), lambda b, h, t: (0, h)),
                  pl.BlockSpec((1, dk), lambda b, h, t: (0, h)),
                  _const_spec(tri.shape), _const_spec(masks.shape)],
        out_specs=pl.BlockSpec((tb, dk), lambda b, h, t: (b * nt + t, h)),
        scratch_shapes=[pltpu.VMEM((dk, dk), F32)],
        compiler_params=_cparams(("parallel", "parallel", "arbitrary")),
        name="hgrn",
    )(proj, proj, proj, proj, lb.reshape(1, -1), nw.reshape(1, -1), tri, masks)


def _gla(proj, w2p, b, nw, tri, masks, *, batch, tb=512):
    n = proj.shape[0]
    nt = n // batch // tb
    dk, dv, hh = GLA_HEAD_K, GLA_HEAD_V, GLA_HEADS
    q0 = 4096 // dk
    k0 = q0 + hh
    v0 = 5120 // dv
    g0 = _COL_GG // dv
    lr0 = _COL_GLR // LANES
    return pl.pallas_call(
        _gla_kernel,
        out_shape=jax.ShapeDtypeStruct((n, hh * dv), BF16),
        grid=(batch, hh, nt),
        in_specs=[pl.BlockSpec((tb, dk), lambda b, h, t: (b * nt + t, q0 + h)),
                  pl.BlockSpec((tb, dk), lambda b, h, t: (b * nt + t, k0 + h)),
                  pl.BlockSpec((tb, dv), lambda b, h, t: (b * nt + t, v0 + h)),
                  pl.BlockSpec((tb, dv), lambda b, h, t: (b * nt + t, g0 + h)),
                  pl.BlockSpec((tb, LANES), lambda b, h, t: (b * nt + t, lr0)),
                  pl.BlockSpec((LANES, dk), lambda b, h, t: (0, h)),
                  pl.BlockSpec((1, dk), lambda b, h, t: (0, h)),
                  pl.BlockSpec((1, dv), lambda b, h, t: (0, h)),
                  _const_spec(tri.shape), _const_spec(masks.shape)],
        out_specs=pl.BlockSpec((tb, dv), lambda b, h, t: (b * nt + t, h)),
        scratch_shapes=[pltpu.VMEM((dv, dk), F32)],
        compiler_params=_cparams(("parallel", "parallel", "arbitrary")),
        name="gla",
    )(proj, proj, proj, proj, proj, w2p, b.reshape(1, -1), nw.reshape(1, -1), tri, masks)


def _cast_rows(src_ref, dst_ref, row_chunk):
    def body(c, carry):
        r = pl.multiple_of(c * row_chunk, row_chunk)
        dst_ref[pl.ds(r, row_chunk), :] = src_ref[pl.ds(r, row_chunk), :].astype(dst_ref.dtype)
        return carry
    lax.fori_loop(0, src_ref.shape[0] // row_chunk, body, 0)


def _pack_bf16_pairs(a):
    c = a.shape[1] // 2
    bits = lax.bitcast_convert_type(a.astype(BF16).astype(F32), U32)
    return (bits[:, :c] >> 16) | (bits[:, c:] & U32(0xFFFF0000))


def _unpack_bf16_pairs(w):
    lo = lax.bitcast_convert_type(w << 16, F32)
    hi = lax.bitcast_convert_type(w & U32(0xFFFF0000), F32)
    return jnp.concatenate([lo, hi], axis=1)


def _out_proj_kernel(oh_ref, og_ref, res_ref, w_ref, nw_ref, *rest, with_router):
    if with_router:
        rw_ref, h_ref, hn_ref, lg_ref, wb_ref = rest
    else:
        h_ref, hn_ref, wb_ref = rest

    @pl.when(pl.program_id(0) == 0)
    def _():
        _cast_rows(w_ref, wb_ref, 256)

    kh = oh_ref.shape[1]
    acc = jnp.dot(oh_ref[...], wb_ref[0:kh, :], preferred_element_type=F32)
    acc = acc + jnp.dot(og_ref[...], wb_ref[kh:, :], preferred_element_type=F32)
    h = res_ref[...] + acc
    h_ref[...] = h
    hn = _rms_rows(h, nw_ref[...])
    hn_ref[...] = _pack_bf16_pairs(hn)
    if with_router:
        hn_hi = hn.astype(BF16)
        hn_lo = (hn - hn_hi.astype(F32)).astype(BF16)
        rw = rw_ref[...]
        rw_hi = rw.astype(BF16)
        rw_lo = (rw - rw_hi.astype(F32)).astype(BF16)
        lg = jnp.dot(hn_hi, rw_hi, preferred_element_type=F32)
        lg = lg + jnp.dot(hn_lo, rw_hi, preferred_element_type=F32)
        lg = lg + jnp.dot(hn_hi, rw_lo, preferred_element_type=F32)
        lg_ref[...] = lg


def _out_proj(o_h, o_g, res, w_out, layer, nw, router_pad=None, *, tm=256):
    n, d = res.shape
    kh, kg = o_h.shape[1], o_g.shape[1]
    with_router = router_pad is not None
    row = lambda i: (i, 0)
    fixed = lambda i: (0, 0)
    in_specs = [pl.BlockSpec((tm, kh), row), pl.BlockSpec((tm, kg), row), pl.BlockSpec((tm, d), row),
                pl.BlockSpec((None, kh + kg, d), lambda i: (layer, 0, 0), pipeline_mode=pl.Buffered(1)),
                pl.BlockSpec((1, d), fixed)]
    args = [o_h, o_g, res, w_out, nw.reshape(1, d)]
    out_shape = [jax.ShapeDtypeStruct((n, d), F32), jax.ShapeDtypeStruct((n, d // 2), U32)]
    out_specs = [pl.BlockSpec((tm, d), row), pl.BlockSpec((tm, d // 2), row)]
    if with_router:
        in_specs.append(pl.BlockSpec((d, LANES), fixed))
        args.append(router_pad)
        out_shape.append(jax.ShapeDtypeStruct((n, LANES), F32))
        out_specs.append(pl.BlockSpec((tm, LANES), row))
    return pl.pallas_call(
        functools.partial(_out_proj_kernel, with_router=with_router),
        out_shape=out_shape,
        grid=(n // tm,),
        in_specs=in_specs,
        out_specs=out_specs,
        scratch_shapes=[pltpu.VMEM((kh + kg, d), BF16)],
        compiler_params=_cparams(("arbitrary",)),
        name="out_proj",
    )(*args)


def _dispatch_kernel(zs_ref, dest_ref, hn_ref, xs_ref, zero_ref, sem, *, tb, block):
    i = pl.program_id(0)

    def zero_copy(e):
        start = pl.multiple_of(zs_ref[e] * block, block)
        return pltpu.make_async_copy(zero_ref, xs_ref.at[pl.ds(start, block)], sem.at[1])

    @pl.when(i == 0)
    def _():
        zero_ref[...] = jnp.zeros_like(zero_ref)
        for e in range(zs_ref.shape[0]):
            @pl.when(zs_ref[e] >= 0)
            def _():
                zero_copy(e).start()
        for e in range(zs_ref.shape[0]):
            @pl.when(zs_ref[e] >= 0)
            def _():
                zero_copy(e).wait()

    def body(r, carry):
        src = hn_ref.at[pl.ds(i * tb + r, 1)]
        for k in range(TOP_K):
            pltpu.make_async_copy(src, xs_ref.at[pl.ds(dest_ref[0, TOP_K * r + k], 1)], sem.at[0]).start()
        return carry

    lax.fori_loop(0, tb, body, 0, unroll=8)
    for k in range(TOP_K):
        pltpu.make_async_copy(hn_ref.at[pl.ds(0, tb)], xs_ref.at[pl.ds(0, tb)], sem.at[0]).wait()


def _dispatch(hn_packed, dest, zero_start, cap, *, block, tb=2048):
    n, dw = hn_packed.shape
    tb = min(tb, n)
    grid_spec = pltpu.PrefetchScalarGridSpec(
        num_scalar_prefetch=1,
        grid=(n // tb,),
        in_specs=[pl.BlockSpec((None, 1, TOP_K * tb), lambda i, zs: (i, 0, 0), memory_space=pltpu.SMEM),
                  pl.BlockSpec(memory_space=pl.ANY)],
        out_specs=pl.BlockSpec(memory_space=pl.ANY),
        scratch_shapes=[pltpu.VMEM((block, dw), U32), pltpu.SemaphoreType.DMA((2,))],
    )
    return pl.pallas_call(
        functools.partial(_dispatch_kernel, tb=tb, block=block),
        out_shape=jax.ShapeDtypeStruct((cap, dw), U32),
        grid_spec=grid_spec,
        compiler_params=_cparams(("arbitrary",)),
        name="dispatch",
    )(zero_start, dest.reshape(n // tb, 1, TOP_K * tb), hn_packed)


def _ffn_kernel(be_ref, nused_ref, x_ref, wg_ref, wu_ref, wd_ref, o_ref, xb_ref, acc_ref, *, row_chunk):
    i = pl.program_id(0)
    f = pl.program_id(1)
    used = i < nused_ref[0]
    n_chunks = x_ref.shape[0] // row_chunk

    @pl.when(f == 0)
    def _():
        acc_ref[...] = jnp.zeros_like(acc_ref)

    @pl.when(jnp.logical_and(f == 0, used))
    def _():
        def body(c, carry):
            rows = pl.ds(pl.multiple_of(c * row_chunk, row_chunk), row_chunk)
            xb_ref[rows, :] = _unpack_bf16_pairs(x_ref[rows, :]).astype(BF16)
            return carry
        lax.fori_loop(0, n_chunks, body, 0)

    @pl.when(used)
    def _():
        x = xb_ref[...]
        g = jnp.dot(x, wg_ref[0].astype(BF16), preferred_element_type=F32)
        u = jnp.dot(x, wu_ref[0].astype(BF16), preferred_element_type=F32)
        a = (g * _sigmoid(g) * u).astype(BF16)
        acc_ref[...] += jnp.dot(a, wd_ref[0].astype(BF16), preferred_element_type=F32)

    @pl.when(f == pl.num_programs(1) - 1)
    def _():
        def body(c, carry):
            rows = pl.ds(pl.multiple_of(c * row_chunk, row_chunk), row_chunk)
            o_ref[rows, :] = _pack_bf16_pairs(acc_ref[rows, :])
            return carry
        lax.fori_loop(0, n_chunks, body, 0)


def _ffn(x_packed, wg, wu, wd, block_e, nused, *, tm=1024, tf=256):
    r, dw = x_packed.shape
    d = 2 * dw
    ff = wg.shape[2]
    nf = ff // tf

    def f_eff(i, f, nu):
        return jnp.where(i < nu[0], f, nf - 1)

    grid_spec = pltpu.PrefetchScalarGridSpec(
        num_scalar_prefetch=2,
        grid=(r // tm, nf),
        in_specs=[
            pl.BlockSpec((tm, dw), lambda i, f, be, nu: (i, 0)),
            pl.BlockSpec((1, d, tf), lambda i, f, be, nu: (be[i], 0, f_eff(i, f, nu))),
            pl.BlockSpec((1, d, tf), lambda i, f, be, nu: (be[i], 0, f_eff(i, f, nu))),
            pl.BlockSpec((1, tf, d), lambda i, f, be, nu: (be[i], f_eff(i, f, nu), 0)),
        ],
        out_specs=pl.BlockSpec((tm, dw), lambda i, f, be, nu: (i, 0)),
        scratch_shapes=[pltpu.VMEM((tm, d), BF16), pltpu.VMEM((tm, d), F32)],
    )
    return pl.pallas_call(
        functools.partial(_ffn_kernel, row_chunk=256),
        out_shape=jax.ShapeDtypeStruct((r, dw), U32),
        grid_spec=grid_spec,
        compiler_params=_cparams(("parallel", "arbitrary")),
        name="ffn",
    )(block_e, nused, x_packed, wg, wu, wd)


def _ple_tail(h2, p_ref, nw_ref, fw_ref, o_ref, wgb_ref, wpb_ref, final):
    hn = _rms_rows(h2, nw_ref[...]).astype(BF16)
    gate = _sigmoid(jnp.dot(hn, wgb_ref[...], preferred_element_type=F32))
    pp = jnp.dot(p_ref[...].astype(BF16), wpb_ref[...], preferred_element_type=F32)
    h3 = h2 + gate * pp
    if final:
        h3 = _rms_rows(h3, fw_ref[...])
    o_ref[...] = h3


def _ple_dense_kernel(h_ref, y_ref, p_ref, nw_ref, wg_ref, wp_ref, fw_ref, o_ref, wgb_ref, wpb_ref, *, final):
    @pl.when(pl.program_id(0) == 0)
    def _():
        _cast_rows(wg_ref, wgb_ref, 256)
        _cast_rows(wp_ref, wpb_ref, 256)

    h2 = h_ref[...] + _unpack_bf16_pairs(y_ref[...])
    _ple_tail(h2, p_ref, nw_ref, fw_ref, o_ref, wgb_ref, wpb_ref, final)


def _ple_moe_kernel(dcur_ref, dnext_ref, h_ref, ys_ref, tw_ref, p_ref, nw_ref, wg_ref, wp_ref, fw_ref, o_ref,
                    wgb_ref, wpb_ref, gbuf_ref, sem, *, final):
    i = pl.program_id(0)
    n_steps = pl.num_programs(0)
    tm = h_ref.shape[0]
    slot = i % 2

    def issue(d_ref, s):
        def body(r, carry):
            for k in range(TOP_K):
                pltpu.make_async_copy(ys_ref.at[pl.ds(d_ref[0, TOP_K * r + k], 1)],
                                      gbuf_ref.at[s, k, pl.ds(r, 1)], sem.at[s]).start()
            return carry
        lax.fori_loop(0, tm, body, 0, unroll=8)

    @pl.when(i == 0)
    def _():
        issue(dcur_ref, 0)
        _cast_rows(wg_ref, wgb_ref, 256)
        _cast_rows(wp_ref, wpb_ref, 256)

    @pl.when(i + 1 < n_steps)
    def _():
        issue(dnext_ref, 1 - slot)

    for k in range(TOP_K):
        pltpu.make_async_copy(ys_ref.at[pl.ds(0, tm)], gbuf_ref.at[slot, k], sem.at[slot]).wait()
    tw = tw_ref[...]
    y = tw[:, 0:1] * _unpack_bf16_pairs(gbuf_ref[slot, 0])
    for k in range(1, TOP_K):
        y = y + tw[:, k:k + 1] * _unpack_bf16_pairs(gbuf_ref[slot, k])
    _ple_tail(h_ref[...] + y, p_ref, nw_ref, fw_ref, o_ref, wgb_ref, wpb_ref, final)


def _ple(h1, y_packed, p, layer, nw, w_gate, w_proj, final_w, *, final, dest=None, top_w=None, tm=256):
    n, d = h1.shape
    pd = p.shape[2]
    dw = d // 2
    n_steps = n // tm
    row = lambda i: (i, 0)
    fixed = lambda i: (0, 0)
    tail_specs = [pl.BlockSpec((None, tm, pd), lambda i: (layer, i, 0)),
                  pl.BlockSpec((1, d), fixed),
                  pl.BlockSpec((None, d, d), lambda i: (layer, 0, 0), pipeline_mode=pl.Buffered(1)),
                  pl.BlockSpec((None, pd, d), lambda i: (layer, 0, 0), pipeline_mode=pl.Buffered(1)),
                  pl.BlockSpec((1, d), fixed)]
    tail_args = [p, nw.reshape(1, d), w_gate, w_proj, final_w.reshape(1, d)]
    scratch = [pltpu.VMEM((d, d), BF16), pltpu.VMEM((pd, d), BF16)]
    if dest is None:
        kern = functools.partial(_ple_dense_kernel, final=final)
        in_specs = [pl.BlockSpec((tm, d), row), pl.BlockSpec((tm, dw), row)] + tail_specs
        args = [h1, y_packed] + tail_args
    else:
        kern = functools.partial(_ple_moe_kernel, final=final)
        dest3 = dest.reshape(n_steps, 1, TOP_K * tm)
        smem = lambda imap: pl.BlockSpec((None, 1, TOP_K * tm), imap, memory_space=pltpu.SMEM)
        in_specs = [smem(lambda i: (i, 0, 0)), smem(lambda i: (jnp.minimum(i + 1, n_steps - 1), 0, 0)),
                    pl.BlockSpec((tm, d), row), pl.BlockSpec(memory_space=pl.ANY),
                    pl.BlockSpec((tm, TOP_K), row)] + tail_specs
        args = [dest3, dest3, h1, y_packed, top_w] + tail_args
        scratch = scratch + [pltpu.VMEM((2, TOP_K, tm, dw), U32), pltpu.SemaphoreType.DMA((2,))]
    return pl.pallas_call(
        kern,
        out_shape=jax.ShapeDtypeStruct((n, d), F32),
        grid=(n_steps,),
        in_specs=in_specs,
        out_specs=pl.BlockSpec((tm, d), row),
        scratch_shapes=scratch,
        compiler_params=_cparams(("arbitrary",)),
        name="ple",
    )(*args)


def _route(logits, block):
    n = logits.shape[0]
    top_logit, top_e = lax.top_k(logits, TOP_K)
    top_w = jax.nn.softmax(top_logit, axis=-1)
    flat_e = top_e.reshape(-1)
    onehot = (flat_e[:, None] == jnp.arange(N_EXPERTS, dtype=flat_e.dtype)[None, :]).astype(jnp.int32)
    csum = jnp.cumsum(onehot, axis=0)
    counts = csum[-1]
    rank = jnp.sum((csum - onehot) * onehot, axis=1)
    padded = (counts + block - 1) // block * block
    padded_end = jnp.cumsum(padded)
    padded_start = padded_end - padded
    dest = (padded_start[flat_e] + rank).astype(jnp.int32)
    n_blocks = -(-(n * TOP_K) // block) + N_EXPERTS
    block_e = jnp.minimum(jnp.searchsorted(padded_end, jnp.arange(n_blocks, dtype=jnp.int32) * block, side='right'),
                          N_EXPERTS - 1).astype(jnp.int32)
    nused = (padded_end[-1] // block).astype(jnp.int32).reshape(1)
    block_e = jnp.where(jnp.arange(n_blocks) < nused[0], block_e, block_e[jnp.maximum(nused[0] - 1, 0)])
    tail = nused[0] + jnp.arange(N_EXPERTS, dtype=jnp.int32)
    zero_start = jnp.concatenate([jnp.where(padded > 0, padded_end // block - 1, -1),
                                  jnp.where(tail < n_blocks, tail, -1)]).astype(jnp.int32)
    return dest.reshape(n, TOP_K), top_w, block_e, nused, zero_start, n_blocks * block


def kernel(x, p, norm_mix_w, w_in, hgrn_lb_logits, hgrn_norm_w, gla_gate_w2, gla_gate_b, gla_norm_w, w_out,
           norm_ffn_w, dense_w_gate, dense_w_up, dense_w_down, moe_router, moe_w_gate, moe_w_up, moe_w_down,
           norm_ple_w, ple_w_gate, ple_w_proj, final_norm_w):
    batch, seq, d = x.shape
    depth = w_in.shape[0]
    n = batch * seq
    moe_block = 1024

    lbs = jnp.cumsum(jax.nn.softmax(hgrn_lb_logits.astype(F32), axis=0), axis=0)
    lbs = lbs - lbs[0]
    tri_np, masks_np = _level_tables()
    tri = jnp.asarray(tri_np, BF16)
    masks = jnp.asarray(masks_np, F32)

    h = x.reshape(n, d)
    p3 = p.reshape(depth, n, -1)
    moe_wg = moe_w_gate.reshape((-1,) + moe_w_gate.shape[2:])
    moe_wu = moe_w_up.reshape((-1,) + moe_w_up.shape[2:])
    moe_wd = moe_w_down.reshape((-1,) + moe_w_down.shape[2:])
    gg0 = _COL_GG + GLA_GATE_RANK
    w_tails = jnp.concatenate(
        [w_in[:, :, gg0:], w_in[:, :, _COL_GG:gg0],
         jnp.zeros((depth, d, _PROJ_COLS - _COL_GLR - GLA_GATE_RANK), F32)], axis=2)
    for i in range(depth):
        proj = _in_proj(h, norm_mix_w[i], w_in, i, w_tails[i])
        o_h = _hgrn(proj, lbs[i], hgrn_norm_w[i], tri, masks, batch=batch)
        w2p = jnp.zeros((LANES, gla_gate_w2.shape[2]), F32).at[:GLA_GATE_RANK].set(gla_gate_w2[i]).astype(BF16)
        o_g = _gla(proj, w2p, gla_gate_b[i], gla_norm_w[i], tri, masks, batch=batch)
        j = i // 2
        last = i == depth - 1
        if i % 2 == 0:
            h1, hn = _out_proj(o_h, o_g, h, w_out, i, norm_ffn_w[i])
            n_blk = n // moe_block
            y = _ffn(hn, dense_w_gate, dense_w_up, dense_w_down,
                     jnp.full((n_blk,), j, jnp.int32), jnp.full((1,), n_blk, jnp.int32), tm=moe_block)
            h = _ple(h1, y, p3, i, norm_ple_w[i], ple_w_gate, ple_w_proj, final_norm_w, final=last)
        else:
            router_pad = jnp.zeros((d, LANES), F32).at[:, :N_EXPERTS].set(moe_router[j])
            h1, hn, logits = _out_proj(o_h, o_g, h, w_out, i, norm_ffn_w[i], router_pad)
            dest, top_w, block_e, nused, zero_start, cap = _route(logits[:, :N_EXPERTS], moe_block)
            xs = _dispatch(hn, dest, zero_start, cap, block=moe_block)
            ys = _ffn(xs, moe_wg, moe_wu, moe_wd, block_e + j * N_EXPERTS, nused, tm=moe_block)
            h = _ple(h1, ys, p3, i, norm_ple_w[i], ple_w_gate, ple_w_proj, final_norm_w, final=last,
                     dest=dest, top_w=top_w)
    return h.reshape(batch, seq, d)
```

```python
import functools

import numpy as np
import jax
import jax.numpy as jnp
from jax import lax
from jax.experimental import pallas as pl
from jax.experimental.pallas import tpu as pltpu

F32 = jnp.float32
BF16 = jnp.bfloat16
U32 = jnp.uint32

EPS = 1e-6
LOG2_E = 1.4426950408889634
HGRN_HEADS = 8
HGRN_HEAD_DIM = 128
GLA_HEADS = 4
GLA_HEAD_K = 128
GLA_HEAD_V = 256
GLA_GATE_RANK = 16
GLA_GATE_TEMP = 16.0
N_EXPERTS = 8
TOP_K = 2

LANES = 128
CHUNK = 128
N_LEVELS = 7
VMEM_LIMIT = 56 * 1024 * 1024

_N_MAIN = 6144
_COL_GG = 6144
_COL_GLR = 7168
_PROJ_COLS = 7680


def _cparams(sem):
    return pltpu.CompilerParams(dimension_semantics=sem, vmem_limit_bytes=VMEM_LIMIT)


def _sigmoid(x):
    return 1.0 / (1.0 + jnp.exp(-x))


def _rms_rows(x, w):
    ms = jnp.mean(x * x, axis=-1, keepdims=True)
    return x * lax.rsqrt(ms + EPS) * w


def _in_proj_kernel(x_ref, nw_ref, w1_ref, w2_ref, o_ref, xn_ref, *, n_main, row_chunk):
    j = pl.program_id(1)
    tm = x_ref.shape[0]

    @pl.when(j == 0)
    def _():
        def body(c, carry):
            r = pl.multiple_of(c * row_chunk, row_chunk)
            xn_ref[pl.ds(r, row_chunk), :] = _rms_rows(x_ref[pl.ds(r, row_chunk), :], nw_ref[...]).astype(BF16)
            return carry
        lax.fori_loop(0, tm // row_chunk, body, 0)

    @pl.when(j < n_main)
    def _():
        o_ref[...] = jnp.dot(xn_ref[...], w1_ref[...].astype(BF16),
                             preferred_element_type=F32).astype(o_ref.dtype)

    @pl.when(j >= n_main)
    def _():
        o_ref[...] = jnp.dot(xn_ref[...], w2_ref[...].astype(BF16),
                             preferred_element_type=F32).astype(o_ref.dtype)


def _in_proj(x, nw, w_in, layer, w_tail, *, tm=1024, tn=512):
    n, d = x.shape
    n_main = _N_MAIN // tn
    n_tail = w_tail.shape[1] // tn
    kern = functools.partial(_in_proj_kernel, n_main=n_main, row_chunk=128)
    return pl.pallas_call(
        kern,
        out_shape=jax.ShapeDtypeStruct((n, (n_main + n_tail) * tn), BF16),
        grid=(n // tm, n_main + n_tail),
        in_specs=[
            pl.BlockSpec((tm, d), lambda i, j: (i, 0)),
            pl.BlockSpec((1, d), lambda i, j: (0, 0)),
            pl.BlockSpec((None, d, tn), lambda i, j: (layer, 0, jnp.minimum(j, n_main - 1))),
            pl.BlockSpec((d, tn), lambda i, j: (0, jnp.maximum(j - n_main, 0))),
        ],
        out_specs=pl.BlockSpec((tm, tn), lambda i, j: (i, j)),
        scratch_shapes=[pltpu.VMEM((tm, d), BF16)],
        compiler_params=_cparams(("parallel", "arbitrary")),
        name="in_proj",
    )(x, nw.reshape(1, d), w_in, w_tail)


def _level_tables():
    c = CHUNK
    idx = np.arange(c)
    masks = []
    s = c // 2
    while s >= 1:
        blk = idx // (2 * s)
        upper = (idx % (2 * s)) >= s
        masks.append(((blk[:, None] == blk[None, :]) & upper[:, None] & (~upper)[None, :]).astype(np.float32))
        s //= 2
    masks.append(np.eye(c, dtype=np.float32))
    tri = np.tril(np.ones((c, c), np.float32))
    return np.concatenate([tri, tri], axis=1), np.stack(masks)


def _level_exponents(g_cum):
    c, dk = g_cum.shape
    row = lax.broadcasted_iota(jnp.int32, (c, dk), 0)
    out = []
    s = c // 2
    while s >= 4:
        nb = c // (2 * s)
        mid = g_cum.reshape(nb, 2 * s, dk)[:, s - 1:s, :]
        mid = jnp.broadcast_to(mid, (nb, 2 * s, dk)).reshape(c, dk)
        out.append(-jnp.abs(g_cum - mid))
        s //= 2
    up1 = pltpu.roll(g_cum, 1, 0)
    up2 = pltpu.roll(g_cum, 2, 0)
    dn1 = pltpu.roll(g_cum, c - 1, 0)
    r4 = row % 4
    mid2 = jnp.where(r4 == 0, dn1, jnp.where(r4 == 1, g_cum, jnp.where(r4 == 2, up1, up2)))
    out.append(-jnp.abs(g_cum - mid2))
    mid1 = jnp.where(row % 2 == 1, up1, g_cum)
    out.append(-jnp.abs(g_cum - mid1))
    return out


_NT = (((1,), (1,)), ((), ()))
_TN = (((0,), (0,)), ((), ()))


def _chunk_local(q, k, v_bf, g, tri_ref, masks_ref):
    c = CHUNK
    g_hi = g.astype(BF16)
    g_lo = (g - g_hi.astype(F32)).astype(BF16)
    g_cum = jnp.dot(tri_ref[...], jnp.concatenate([g_hi, g_lo], axis=0), preferred_element_type=F32) * LOG2_E
    a = masks_ref[N_LEVELS] * lax.dot_general(q.astype(BF16), k.astype(BF16), _NT, preferred_element_type=F32)
    for l, ex in enumerate(_level_exponents(g_cum)):
        e = jnp.exp2(ex)
        a = a + masks_ref[l] * lax.dot_general((q * e).astype(BF16), (k * e).astype(BF16), _NT,
                                               preferred_element_type=F32)
    g_last = g_cum[c - 1:c, :]
    o_intra = jnp.dot(a.astype(BF16), v_bf, preferred_element_type=F32)
    q_dec = (q * jnp.exp2(g_cum)).astype(BF16)
    k_dec = (k * jnp.exp2(g_last - g_cum)).astype(BF16)
    kv = lax.dot_general(v_bf, k_dec, _TN, preferred_element_type=F32)
    return o_intra, q_dec, kv, jnp.exp2(g_last)


def _head_norm(o, w):
    ms = jnp.mean(o * o, axis=-1, keepdims=True)
    return o * lax.rsqrt(ms + EPS) * w


def _scan_block(n_chunks, local_fn, gate_fn, nw_ref, o_ref, st_ref):
    parts = [local_fn(ci) for ci in range(n_chunks)]
    st = st_ref[...]
    for ci, (o_intra, q_dec, kv, decay) in enumerate(parts):
        o = o_intra + lax.dot_general(q_dec, st.astype(BF16), _NT, preferred_element_type=F32)
        rows = pl.ds(ci * CHUNK, CHUNK)
        o_ref[rows, :] = (_head_norm(o, nw_ref[...]) * gate_fn(rows)).astype(o_ref.dtype)
        st = st * decay + kv
    st_ref[...] = st


def _hgrn_kernel(q_ref, f_ref, i_ref, g_ref, lb_ref, nw_ref, tri_ref, masks_ref, o_ref, st_ref):
    @pl.when(pl.program_id(2) == 0)
    def _():
        st_ref[...] = jnp.zeros_like(st_ref)

    lb = lb_ref[...]

    def local(ci):
        rows = pl.ds(ci * CHUNK, CHUNK)
        fgate = lb + (1.0 - lb) * _sigmoid(f_ref[rows, :].astype(F32))
        g = jnp.log(jnp.maximum(fgate, 1e-38))
        k = 1.0 - fgate
        hq = q_ref[rows, :].astype(F32)
        return _chunk_local(hq * _sigmoid(hq), k, i_ref[rows, :], g, tri_ref, masks_ref)

    def gate(rows):
        return _sigmoid(g_ref[rows, :].astype(F32))

    _scan_block(q_ref.shape[0] // CHUNK, local, gate, nw_ref, o_ref, st_ref)


def _gla_kernel(q_ref, k_ref, v_ref, g_ref, lr_ref, w2_ref, b_ref, nw_ref, tri_ref, masks_ref, o_ref, st_ref):
    @pl.when(pl.program_id(2) == 0)
    def _():
        st_ref[...] = jnp.zeros_like(st_ref)

    def local(ci):
        rows = pl.ds(ci * CHUNK, CHUNK)
        u = jnp.dot(lr_ref[rows, :], w2_ref[...], preferred_element_type=F32) + b_ref[...]
        g = (jnp.minimum(u, 0.0) - jnp.log(1.0 + jnp.exp(-jnp.abs(u)))) * (1.0 / GLA_GATE_TEMP)
        q = q_ref[rows, :].astype(F32) * (GLA_HEAD_K ** -0.5)
        return _chunk_local(q, k_ref[rows, :].astype(F32), v_ref[rows, :], g, tri_ref, masks_ref)

    def gate(rows):
        gg = g_ref[rows, :].astype(F32)
        return gg * _sigmoid(gg)

    _scan_block(q_ref.shape[0] // CHUNK, local, gate, nw_ref, o_ref, st_ref)


def _const_spec(shape):
    nd = len(shape)
    return pl.BlockSpec(shape, lambda b, h, t: (0,) * nd)


def _hgrn(proj, lb, nw, tri, masks, *, batch, tb=512):
    n = proj.shape[0]
    nt = n // batch // tb
    dk = HGRN_HEAD_DIM
    hh = HGRN_HEADS

    def col(off):
        return pl.BlockSpec((tb, dk), lambda b, h, t: (b * nt + t, off + h))

    return pl.pallas_call(
        _hgrn_kernel,
        out_shape=jax.ShapeDtypeStruct((n, hh * dk), BF16),
        grid=(batch, hh, nt),
        in_specs=[col(0), col(hh), col(2 * hh), col(3 * hh),
                  pl.BlockSpec((1, dk), lambda b, h, t: (0, h)),
                  pl.BlockSpec((1, dk), lambda b, h, t: (0, h)),
                  _const_spec(tri.shape), _const_spec(masks.shape)],
        out_specs=pl.BlockSpec((tb, dk), lambda b, h, t: (b * nt + t, h)),
        scratch_shapes=[pltpu.VMEM((dk, dk), F32)],
        compiler_params=_cparams(("parallel", "parallel", "arbitrary")),
        name="hgrn",
    )(proj, proj, proj, proj, lb.reshape(1, -1), nw.reshape(1, -1), tri, masks)


def _gla(proj, w2p, b, nw, tri, masks, *, batch, tb=512):
    n = proj.shape[0]
    nt = n // batch // tb
    dk, dv, hh = GLA_HEAD_K, GLA_HEAD_V, GLA_HEADS
    q0 = 4096 // dk
    k0 = q0 + hh
    v0 = 5120 // dv
    g0 = _COL_GG // dv
    lr0 = _COL_GLR // LANES
    return pl.pallas_call(
        _gla_kernel,
        out_shape=jax.ShapeDtypeStruct((n, hh * dv), BF16),
        grid=(batch, hh, nt),
        in_specs=[pl.BlockSpec((tb, dk), lambda b, h, t: (b * nt + t, q0 + h)),
                  pl.BlockSpec((tb, dk), lambda b, h, t: (b * nt + t, k0 + h)),
                  pl.BlockSpec((tb, dv), lambda b, h, t: (b * nt + t, v0 + h)),
                  pl.BlockSpec((tb, dv), lambda b, h, t: (b * nt + t, g0 + h)),
                  pl.BlockSpec((tb, LANES), lambda b, h, t: (b * nt + t, lr0)),
                  pl.BlockSpec((LANES, dk), lambda b, h, t: (0, h)),
                  pl.BlockSpec((1, dk), lambda b, h, t: (0, h)),
                  pl.BlockSpec((1, dv), lambda b, h, t: (0, h)),
                  _const_spec(tri.shape), _const_spec(masks.shape)],
        out_specs=pl.BlockSpec((tb, dv), lambda b, h, t: (b * nt + t, h)),
        scratch_shapes=[pltpu.VMEM((dv, dk), F32)],
        compiler_params=_cparams(("parallel", "parallel", "arbitrary")),
        name="gla",
    )(proj, proj, proj, proj, proj, w2p, b.reshape(1, -1), nw.reshape(1, -1), tri, masks)


def _cast_rows(src_ref, dst_ref, row_chunk):
    def body(c, carry):
        r = pl.multiple_of(c * row_chunk, row_chunk)
        dst_ref[pl.ds(r, row_chunk), :] = src_ref[pl.ds(r, row_chunk), :].astype(dst_ref.dtype)
        return carry
    lax.fori_loop(0, src_ref.shape[0] // row_chunk, body, 0)


def _pack_bf16_pairs(a):
    c = a.shape[1] // 2
    bits = lax.bitcast_convert_type(a.astype(BF16).astype(F32), U32)
    return (bits[:, :c] >> 16) | (bits[:, c:] & U32(0xFFFF0000))


def _unpack_bf16_pairs(w):
    lo = lax.bitcast_convert_type(w << 16, F32)
    hi = lax.bitcast_convert_type(w & U32(0xFFFF0000), F32)
    return jnp.concatenate([lo, hi], axis=1)


def _out_proj_kernel(oh_ref, og_ref, res_ref, w_ref, nw_ref, *rest, with_router):
    if with_router:
        rw_ref, h_ref, hn_ref, lg_ref, wb_ref = rest
    else:
        h_ref, hn_ref, wb_ref = rest

    @pl.when(pl.program_id(0) == 0)
    def _():
        _cast_rows(w_ref, wb_ref, 256)

    kh = oh_ref.shape[1]
    acc = jnp.dot(oh_ref[...], wb_ref[0:kh, :], preferred_element_type=F32)
    acc = acc + jnp.dot(og_ref[...], wb_ref[kh:, :], preferred_element_type=F32)
    h = res_ref[...] + acc
    h_ref[...] = h
    hn = _rms_rows(h, nw_ref[...])
    hn_ref[...] = _pack_bf16_pairs(hn)
    if with_router:
        hn_hi = hn.astype(BF16)
        hn_lo = (hn - hn_hi.astype(F32)).astype(BF16)
        rw = rw_ref[...]
        rw_hi = rw.astype(BF16)
        rw_lo = (rw - rw_hi.astype(F32)).astype(BF16)
        lg = jnp.dot(hn_hi, rw_hi, preferred_element_type=F32)
        lg = lg + jnp.dot(hn_lo, rw_hi, preferred_element_type=F32)
        lg = lg + jnp.dot(hn_hi, rw_lo, preferred_element_type=F32)
        lg_ref[...] = lg


def _out_proj(o_h, o_g, res, w_out, layer, nw, router_pad=None, *, tm=256):
    n, d = res.shape
    kh, kg = o_h.shape[1], o_g.shape[1]
    with_router = router_pad is not None
    row = lambda i: (i, 0)
    fixed = lambda i: (0, 0)
    in_specs = [pl.BlockSpec((tm, kh), row), pl.BlockSpec((tm, kg), row), pl.BlockSpec((tm, d), row),
                pl.BlockSpec((None, kh + kg, d), lambda i: (layer, 0, 0), pipeline_mode=pl.Buffered(1)),
                pl.BlockSpec((1, d), fixed)]
    args = [o_h, o_g, res, w_out, nw.reshape(1, d)]
    out_shape = [jax.ShapeDtypeStruct((n, d), F32), jax.ShapeDtypeStruct((n, d // 2), U32)]
    out_specs = [pl.BlockSpec((tm, d), row), pl.BlockSpec((tm, d // 2), row)]
    if with_router:
        in_specs.append(pl.BlockSpec((d, LANES), fixed))
        args.append(router_pad)
        out_shape.append(jax.ShapeDtypeStruct((n, LANES), F32))
        out_specs.append(pl.BlockSpec((tm, LANES), row))
    return pl.pallas_call(
        functools.partial(_out_proj_kernel, with_router=with_router),
        out_shape=out_shape,
        grid=(n // tm,),
        in_specs=in_specs,
        out_specs=out_specs,
        scratch_shapes=[pltpu.VMEM((kh + kg, d), BF16)],
        compiler_params=_cparams(("arbitrary",)),
        name="out_proj",
    )(*args)


def _dispatch_kernel(zs_ref, dest_ref, hn_ref, xs_ref, zero_ref, sem, *, tb, block):
    i = pl.program_id(0)

    def zero_copy(e):
        start = pl.multiple_of(zs_ref[e] * block, block)
        return pltpu.make_async_copy(zero_ref, xs_ref.at[pl.ds(start, block)], sem.at[1])

    @pl.when(i == 0)
    def _():
        zero_ref[...] = jnp.zeros_like(zero_ref)
        for e in range(zs_ref.shape[0]):
            @pl.when(zs_ref[e] >= 0)
            def _():
                zero_copy(e).start()
        for e in range(zs_ref.shape[0]):
            @pl.when(zs_ref[e] >= 0)
            def _():
                zero_copy(e).wait()

    def body(r, carry):
        src = hn_ref.at[pl.ds(r, 1)]
        for k in range(TOP_K):
            pltpu.make_async_copy(src, xs_ref.at[pl.ds(dest_ref[0, TOP_K * r + k], 1)], sem.at[0]).start()
        return carry

    lax.fori_loop(0, tb, body, 0, unroll=8)
    for k in range(TOP_K):
        pltpu.make_async_copy(hn_ref, xs_ref.at[pl.ds(0, tb)], sem.at[0]).wait()


def _dispatch(hn_packed, dest, zero_start, cap, *, block, tb=512):
    n, dw = hn_packed.shape
    tb = min(tb, n)
    grid_spec = pltpu.PrefetchScalarGridSpec(
        num_scalar_prefetch=1,
        grid=(n // tb,),
        in_specs=[pl.BlockSpec((None, 1, TOP_K * tb), lambda i, zs: (i, 0, 0), memory_space=pltpu.SMEM),
                  pl.BlockSpec((tb, dw), lambda i, zs: (i, 0))],
        out_specs=pl.BlockSpec(memory_space=pl.ANY),
        scratch_shapes=[pltpu.VMEM((block, dw), U32), pltpu.SemaphoreType.DMA((2,))],
    )
    return pl.pallas_call(
        functools.partial(_dispatch_kernel, tb=tb, block=block),
        out_shape=jax.ShapeDtypeStruct((cap, dw), U32),
        grid_spec=grid_spec,
        compiler_params=_cparams(("arbitrary",)),
        name="dispatch",
    )(zero_start, dest.reshape(n // tb, 1, TOP_K * tb), hn_packed)


def _ffn_kernel(be_ref, nused_ref, x_ref, wg_ref, wu_ref, wd_ref, o_ref, xb_ref, acc_ref, *, row_chunk):
    i = pl.program_id(0)
    f = pl.program_id(1)
    used = i < nused_ref[0]
    n_chunks = x_ref.shape[0] // row_chunk

    @pl.when(f == 0)
    def _():
        acc_ref[...] = jnp.zeros_like(acc_ref)

    @pl.when(jnp.logical_and(f == 0, used))
    def _():
        def body(c, carry):
            rows = pl.ds(pl.multiple_of(c * row_chunk, row_chunk), row_chunk)
            xb_ref[rows, :] = _unpack_bf16_pairs(x_ref[rows, :]).astype(BF16)
            return carry
        lax.fori_loop(0, n_chunks, body, 0)

    @pl.when(used)
    def _():
        x = xb_ref[...]
        g = jnp.dot(x, wg_ref[0].astype(BF16), preferred_element_type=F32)
        u = jnp.dot(x, wu_ref[0].astype(BF16), preferred_element_type=F32)
        a = (g * _sigmoid(g) * u).astype(BF16)
        acc_ref[...] += jnp.dot(a, wd_ref[0].astype(BF16), preferred_element_type=F32)

    @pl.when(f == pl.num_programs(1) - 1)
    def _():
        def body(c, carry):
            rows = pl.ds(pl.multiple_of(c * row_chunk, row_chunk), row_chunk)
            o_ref[rows, :] = _pack_bf16_pairs(acc_ref[rows, :])
            return carry
        lax.fori_loop(0, n_chunks, body, 0)


def _ffn(x_packed, wg, wu, wd, block_e, nused, *, tm=1024, tf=512):
    r, dw = x_packed.shape
    d = 2 * dw
    ff = wg.shape[2]
    nf = ff // tf

    def f_eff(i, f, nu):
        return jnp.where(i < nu[0], f, nf - 1)

    grid_spec = pltpu.PrefetchScalarGridSpec(
        num_scalar_prefetch=2,
        grid=(r // tm, nf),
        in_specs=[
            pl.BlockSpec((tm, dw), lambda i, f, be, nu: (i, 0), pipeline_mode=pl.Buffered(1)),
            pl.BlockSpec((1, d, tf), lambda i, f, be, nu: (be[i], 0, f_eff(i, f, nu))),
            pl.BlockSpec((1, d, tf), lambda i, f, be, nu: (be[i], 0, f_eff(i, f, nu))),
            pl.BlockSpec((1, tf, d), lambda i, f, be, nu: (be[i], f_eff(i, f, nu), 0)),
        ],
        out_specs=pl.BlockSpec((tm, dw), lambda i, f, be, nu: (i, 0)),
        scratch_shapes=[pltpu.VMEM((tm, d), BF16), pltpu.VMEM((tm, d), F32)],
    )
    return pl.pallas_call(
        functools.partial(_ffn_kernel, row_chunk=256),
        out_shape=jax.ShapeDtypeStruct((r, dw), U32),
        grid_spec=grid_spec,
        compiler_params=_cparams(("parallel", "arbitrary")),
        name="ffn",
    )(block_e, nused, x_packed, wg, wu, wd)


def _ple_tail(h2, p_ref, nw_ref, fw_ref, o_ref, wgb_ref, wpb_ref, final):
    hn = _rms_rows(h2, nw_ref[...]).astype(BF16)
    gate = _sigmoid(jnp.dot(hn, wgb_ref[...], preferred_element_type=F32))
    pp = jnp.dot(p_ref[...].astype(BF16), wpb_ref[...], preferred_element_type=F32)
    h3 = h2 + gate * pp
    if final:
        h3 = _rms_rows(h3, fw_ref[...])
    o_ref[...] = h3


def _ple_dense_kernel(h_ref, y_ref, p_ref, nw_ref, wg_ref, wp_ref, fw_ref, o_ref, wgb_ref, wpb_ref, *, final):
    @pl.when(pl.program_id(0) == 0)
    def _():
        _cast_rows(wg_ref, wgb_ref, 256)
        _cast_rows(wp_ref, wpb_ref, 256)

    h2 = h_ref[...] + _unpack_bf16_pairs(y_ref[...])
    _ple_tail(h2, p_ref, nw_ref, fw_ref, o_ref, wgb_ref, wpb_ref, final)


def _ple_moe_kernel(dcur_ref, dnext_ref, h_ref, ys_ref, tw_ref, p_ref, nw_ref, wg_ref, wp_ref, fw_ref, o_ref,
                    wgb_ref, wpb_ref, gbuf_ref, sem, *, final):
    i = pl.program_id(0)
    n_steps = pl.num_programs(0)
    tm = h_ref.shape[0]
    slot = i % 2

    def issue(d_ref, s):
        def body(r, carry):
            for k in range(TOP_K):
                pltpu.make_async_copy(ys_ref.at[pl.ds(d_ref[0, TOP_K * r + k], 1)],
                                      gbuf_ref.at[s, k, pl.ds(r, 1)], sem.at[s]).start()
            return carry
        lax.fori_loop(0, tm, body, 0, unroll=8)

    @pl.when(i == 0)
    def _():
        issue(dcur_ref, 0)
        _cast_rows(wg_ref, wgb_ref, 256)
        _cast_rows(wp_ref, wpb_ref, 256)

    @pl.when(i + 1 < n_steps)
    def _():
        issue(dnext_ref, 1 - slot)

    for k in range(TOP_K):
        pltpu.make_async_copy(ys_ref.at[pl.ds(0, tm)], gbuf_ref.at[slot, k], sem.at[slot]).wait()
    tw = tw_ref[...]
    y = tw[:, 0:1] * _unpack_bf16_pairs(gbuf_ref[slot, 0])
    for k in range(1, TOP_K):
        y = y + tw[:, k:k + 1] * _unpack_bf16_pairs(gbuf_ref[slot, k])
    _ple_tail(h_ref[...] + y, p_ref, nw_ref, fw_ref, o_ref, wgb_ref, wpb_ref, final)


def _ple(h1, y_packed, p, layer, nw, w_gate, w_proj, final_w, *, final, dest=None, top_w=None, tm=256):
    n, d = h1.shape
    pd = p.shape[2]
    dw = d // 2
    n_steps = n // tm
    row = lambda i: (i, 0)
    fixed = lambda i: (0, 0)
    tail_specs = [pl.BlockSpec((None, tm, pd), lambda i: (layer, i, 0)),
                  pl.BlockSpec((1, d), fixed),
                  pl.BlockSpec((None, d, d), lambda i: (layer, 0, 0), pipeline_mode=pl.Buffered(1)),
                  pl.BlockSpec((None, pd, d), lambda i: (layer, 0, 0), pipeline_mode=pl.Buffered(1)),
                  pl.BlockSpec((1, d), fixed)]
    tail_args = [p, nw.reshape(1, d), w_gate, w_proj, final_w.reshape(1, d)]
    scratch = [pltpu.VMEM((d, d), BF16), pltpu.VMEM((pd, d), BF16)]
    if dest is None:
        kern = functools.partial(_ple_dense_kernel, final=final)
        in_specs = [pl.BlockSpec((tm, d), row), pl.BlockSpec((tm, dw), row)] + tail_specs
        args = [h1, y_packed] + tail_args
    else:
        kern = functools.partial(_ple_moe_kernel, final=final)
        dest3 = dest.reshape(n_steps, 1, TOP_K * tm)
        smem = lambda imap: pl.BlockSpec((None, 1, TOP_K * tm), imap, memory_space=pltpu.SMEM)
        in_specs = [smem(lambda i: (i, 0, 0)), smem(lambda i: (jnp.minimum(i + 1, n_steps - 1), 0, 0)),
                    pl.BlockSpec((tm, d), row), pl.BlockSpec(memory_space=pl.ANY),
                    pl.BlockSpec((tm, TOP_K), row)] + tail_specs
        args = [dest3, dest3, h1, y_packed, top_w] + tail_args
        scratch = scratch + [pltpu.VMEM((2, TOP_K, tm, dw), U32), pltpu.SemaphoreType.DMA((2,))]
    return pl.pallas_call(
        kern,
        out_shape=jax.ShapeDtypeStruct((n, d), F32),
        grid=(n_steps,),
        in_specs=in_specs,
        out_specs=pl.BlockSpec((tm, d), row),
        scratch_shapes=scratch,
        compiler_params=_cparams(("arbitrary",)),
        name="ple",
    )(*args)


def _route(logits, block):
    n = logits.shape[0]
    top_logit, top_e = lax.top_k(logits, TOP_K)
    top_w = jax.nn.softmax(top_logit, axis=-1)
    flat_e = top_e.reshape(-1)
    onehot = (flat_e[:, None] == jnp.arange(N_EXPERTS, dtype=flat_e.dtype)[None, :]).astype(jnp.int32)
    csum = jnp.cumsum(onehot, axis=0)
    counts = csum[-1]
    rank = jnp.sum((csum - onehot) * onehot, axis=1)
    padded = (counts + block - 1) // block * block
    padded_end = jnp.cumsum(padded)
    padded_start = padded_end - padded
    dest = (padded_start[flat_e] + rank).astype(jnp.int32)
    n_blocks = -(-(n * TOP_K) // block) + N_EXPERTS
    block_e = jnp.minimum(jnp.searchsorted(padded_end, jnp.arange(n_blocks, dtype=jnp.int32) * block, side='right'),
                          N_EXPERTS - 1).astype(jnp.int32)
    nused = (padded_end[-1] // block).astype(jnp.int32).reshape(1)
    block_e = jnp.where(jnp.arange(n_blocks) < nused[0], block_e, block_e[jnp.maximum(nused[0] - 1, 0)])
    tail = nused[0] + jnp.arange(N_EXPERTS, dtype=jnp.int32)
    zero_start = jnp.concatenate([jnp.where(padded > 0, padded_end // block - 1, -1),
                                  jnp.where(tail < n_blocks, tail, -1)]).astype(jnp.int32)
    return dest.reshape(n, TOP_K), top_w, block_e, nused, zero_start, n_blocks * block


def kernel(x, p, norm_mix_w, w_in, hgrn_lb_logits, hgrn_norm_w, gla_gate_w2, gla_gate_b, gla_norm_w, w_out,
           norm_ffn_w, dense_w_gate, dense_w_up, dense_w_down, moe_router, moe_w_gate, moe_w_up, moe_w_down,
           norm_ple_w, ple_w_gate, ple_w_proj, final_norm_w):
    batch, seq, d = x.shape
    depth = w_in.shape[0]
    n = batch * seq
    moe_block = 1024

    lbs = jnp.cumsum(jax.nn.softmax(hgrn_lb_logits.astype(F32), axis=0), axis=0)
    lbs = lbs - lbs[0]
    tri_np, masks_np = _level_tables()
    tri = jnp.asarray(tri_np, BF16)
    masks = jnp.asarray(masks_np, F32)

    h = x.reshape(n, d)
    p3 = p.reshape(depth, n, -1)
    moe_wg = moe_w_gate.reshape((-1,) + moe_w_gate.shape[2:])
    moe_wu = moe_w_up.reshape((-1,) + moe_w_up.shape[2:])
    moe_wd = moe_w_down.reshape((-1,) + moe_w_down.shape[2:])
    gg0 = _COL_GG + GLA_GATE_RANK
    w_tails = jnp.concatenate(
        [w_in[:, :, gg0:], w_in[:, :, _COL_GG:gg0],
         jnp.zeros((depth, d, _PROJ_COLS - _COL_GLR - GLA_GATE_RANK), F32)], axis=2)
    for i in range(depth):
        proj = _in_proj(h, norm_mix_w[i], w_in, i, w_tails[i])
        o_h = _hgrn(proj, lbs[i], hgrn_norm_w[i], tri, masks, batch=batch)
        w2p = jnp.zeros((LANES, gla_gate_w2.shape[2]), F32).at[:GLA_GATE_RANK].set(gla_gate_w2[i]).astype(BF16)
        o_g = _gla(proj, w2p, gla_gate_b[i], gla_norm_w[i], tri, masks, batch=batch)
        j = i // 2
        last = i == depth - 1
        if i % 2 == 0:
            h1, hn = _out_proj(o_h, o_g, h, w_out, i, norm_ffn_w[i])
            n_blk = n // moe_block
            y = _ffn(hn, dense_w_gate, dense_w_up, dense_w_down,
                     jnp.full((n_blk,), j, jnp.int32), jnp.full((1,), n_blk, jnp.int32), tm=moe_block)
            h = _ple(h1, y, p3, i, norm_ple_w[i], ple_w_gate, ple_w_proj, final_norm_w, final=last)
        else:
            router_pad = jnp.zeros((d, LANES), F32).at[:, :N_EXPERTS].set(moe_router[j])
            h1, hn, logits = _out_proj(o_h, o_g, h, w_out, i, norm_ffn_w[i], router_pad)
            dest, top_w, block_e, nused, zero_start, cap = _route(logits[:, :N_EXPERTS], moe_block)
            xs = _dispatch(hn, dest, zero_start, cap, block=moe_block)
            ys = _ffn(xs, moe_wg, moe_wu, moe_wd, block_e + j * N_EXPERTS, nused, tm=moe_block)
            h = _ple(h1, ys, p3, i, norm_ple_w[i], ple_w_gate, ple_w_proj, final_norm_w, final=last,
                     dest=dest, top_w=top_w)
    return h.reshape(batch, seq, d)
```

```python
import functools

import numpy as np
import jax
import jax.numpy as jnp
from jax import lax
from jax.experimental import pallas as pl
from jax.experimental.pallas import tpu as pltpu

F32 = jnp.float32
BF16 = jnp.bfloat16
U32 = jnp.uint32

EPS = 1e-6
LOG2_E = 1.4426950408889634
HGRN_HEADS = 8
HGRN_HEAD_DIM = 128
GLA_HEADS = 4
GLA_HEAD_K = 128
GLA_HEAD_V = 256
GLA_GATE_RANK = 16
GLA_GATE_TEMP = 16.0
N_EXPERTS = 8
TOP_K = 2

LANES = 128
CHUNK = 128
N_LEVELS = 7
VMEM_LIMIT = 56 * 1024 * 1024

_N_MAIN = 6144
_COL_GG = 6144
_COL_GLR = 7168
_PROJ_COLS = 7680


def _cparams(sem):
    return pltpu.CompilerParams(dimension_semantics=sem, vmem_limit_bytes=VMEM_LIMIT)


def _sigmoid(x):
    return 1.0 / (1.0 + jnp.exp(-x))


def _rms_rows(x, w):
    ms = jnp.mean(x * x, axis=-1, keepdims=True)
    return x * lax.rsqrt(ms + EPS) * w


def _in_proj_kernel(x_ref, nw_ref, w1_ref, w2_ref, o_ref, xn_ref, *, n_main, row_chunk):
    j = pl.program_id(1)
    tm = x_ref.shape[0]

    @pl.when(j == 0)
    def _():
        def body(c, carry):
            r = pl.multiple_of(c * row_chunk, row_chunk)
            xn_ref[pl.ds(r, row_chunk), :] = _rms_rows(x_ref[pl.ds(r, row_chunk), :], nw_ref[...]).astype(BF16)
            return carry
        lax.fori_loop(0, tm // row_chunk, body, 0)

    @pl.when(j < n_main)
    def _():
        o_ref[...] = jnp.dot(xn_ref[...], w1_ref[...].astype(BF16),
                             preferred_element_type=F32).astype(o_ref.dtype)

    @pl.when(j >= n_main)
    def _():
        o_ref[...] = jnp.dot(xn_ref[...], w2_ref[...].astype(BF16),
                             preferred_element_type=F32).astype(o_ref.dtype)


def _in_proj(x, nw, w_in, layer, w_tail, *, tm=1024, tn=512):
    n, d = x.shape
    n_main = _N_MAIN // tn
    n_tail = w_tail.shape[1] // tn
    kern = functools.partial(_in_proj_kernel, n_main=n_main, row_chunk=128)
    return pl.pallas_call(
        kern,
        out_shape=jax.ShapeDtypeStruct((n, (n_main + n_tail) * tn), BF16),
        grid=(n // tm, n_main + n_tail),
        in_specs=[
            pl.BlockSpec((tm, d), lambda i, j: (i, 0)),
            pl.BlockSpec((1, d), lambda i, j: (0, 0)),
            pl.BlockSpec((None, d, tn), lambda i, j: (layer, 0, jnp.minimum(j, n_main - 1))),
            pl.BlockSpec((d, tn), lambda i, j: (0, jnp.maximum(j - n_main, 0))),
        ],
        out_specs=pl.BlockSpec((tm, tn), lambda i, j: (i, j)),
        scratch_shapes=[pltpu.VMEM((tm, d), BF16)],
        compiler_params=_cparams(("parallel", "arbitrary")),
        name="in_proj",
    )(x, nw.reshape(1, d), w_in, w_tail)


def _level_tables():
    c = CHUNK
    idx = np.arange(c)
    masks = []
    s = c // 2
    while s >= 1:
        blk = idx // (2 * s)
        upper = (idx % (2 * s)) >= s
        masks.append(((blk[:, None] == blk[None, :]) & upper[:, None] & (~upper)[None, :]).astype(np.float32))
        s //= 2
    masks.append(np.eye(c, dtype=np.float32))
    tri = np.tril(np.ones((c, c), np.float32))
    return np.concatenate([tri, tri], axis=1), np.stack(masks)


def _neg_abs(d):
    return lax.bitcast_convert_type(lax.bitcast_convert_type(d, U32) | U32(0x80000000), F32)


def _level_exponents(g_cum):
    c, dk = g_cum.shape
    row = lax.broadcasted_iota(jnp.int32, (c, dk), 0)
    out = []
    s = c // 2
    while s >= 4:
        nb = c // (2 * s)
        mid = g_cum.reshape(nb, 2 * s, dk)[:, s - 1:s, :]
        mid = jnp.broadcast_to(mid, (nb, 2 * s, dk)).reshape(c, dk)
        out.append(_neg_abs(g_cum - mid))
        s //= 2
    up1 = pltpu.roll(g_cum, 1, 0)
    up2 = pltpu.roll(g_cum, 2, 0)
    dn1 = pltpu.roll(g_cum, c - 1, 0)
    r4 = row % 4
    mid2 = jnp.where(r4 == 0, dn1, jnp.where(r4 == 1, g_cum, jnp.where(r4 == 2, up1, up2)))
    out.append(_neg_abs(g_cum - mid2))
    mid1 = jnp.where(row % 2 == 1, up1, g_cum)
    out.append(_neg_abs(g_cum - mid1))
    return out


_NT = (((1,), (1,)), ((), ()))
_TN = (((0,), (0,)), ((), ()))


def _chunk_prep(g, tri_ref):
    g_hi = g.astype(BF16)
    g_lo = (g - g_hi.astype(F32)).astype(BF16)
    return jnp.dot(tri_ref[...], jnp.concatenate([g_hi, g_lo], axis=0), preferred_element_type=F32) * LOG2_E


def _chunk_local(q, k, v_bf, g_cum, masks_ref):
    c = CHUNK
    q_bf = q.astype(BF16)
    k_bf = k.astype(BF16)
    a = masks_ref[N_LEVELS] * lax.dot_general(q_bf, k_bf, _NT, preferred_element_type=F32).astype(BF16)
    for l, ex in enumerate(_level_exponents(g_cum)):
        e = jnp.exp2(ex).astype(BF16)
        s = lax.dot_general(q_bf * e, k_bf * e, _NT, preferred_element_type=F32)
        a = a + masks_ref[l] * s.astype(BF16)
    g_last = g_cum[c - 1:c, :]
    o_intra = jnp.dot(a, v_bf, preferred_element_type=F32)
    q_dec = (q * jnp.exp2(g_cum)).astype(BF16)
    k_dec = (k * jnp.exp2(g_last - g_cum)).astype(BF16)
    kv = lax.dot_general(v_bf, k_dec, _TN, preferred_element_type=F32)
    return o_intra, q_dec, kv, jnp.exp2(g_last)


def _head_norm(o, w):
    ms = jnp.mean(o * o, axis=-1, keepdims=True)
    return o * lax.rsqrt(ms + EPS) * w


def _scan_block(n_chunks, gates_fn, gate_fn, nw_ref, tri_ref, masks_ref, o_ref, st_ref):
    gates = [gates_fn(ci) for ci in range(n_chunks)]
    cums = [_chunk_prep(g, tri_ref) for (_, _, _, g) in gates]
    parts = [_chunk_local(q, k, v_bf, g_cum, masks_ref) for (q, k, v_bf, _), g_cum in zip(gates, cums)]
    st = st_ref[...]
    for ci, (o_intra, q_dec, kv, decay) in enumerate(parts):
        o = o_intra + lax.dot_general(q_dec, st.astype(BF16), _NT, preferred_element_type=F32)
        rows = pl.ds(ci * CHUNK, CHUNK)
        o_ref[rows, :] = (_head_norm(o, nw_ref[...]) * gate_fn(rows)).astype(o_ref.dtype)
        st = st * decay + kv
    st_ref[...] = st


def _hgrn_kernel(q_ref, f_ref, i_ref, g_ref, lb_ref, nw_ref, tri_ref, masks_ref, o_ref, st_ref):
    @pl.when(pl.program_id(2) == 0)
    def _():
        st_ref[...] = jnp.zeros_like(st_ref)

    lb = lb_ref[...]

    def local(ci):
        rows = pl.ds(ci * CHUNK, CHUNK)
        fgate = lb + (1.0 - lb) * _sigmoid(f_ref[rows, :].astype(F32))
        g = jnp.log(jnp.maximum(fgate, 1e-38))
        k = 1.0 - fgate
        hq = q_ref[rows, :].astype(F32)
        return hq * _sigmoid(hq), k, i_ref[rows, :], g

    def gate(rows):
        return _sigmoid(g_ref[rows, :].astype(F32))

    _scan_block(q_ref.shape[0] // CHUNK, local, gate, nw_ref, tri_ref, masks_ref, o_ref, st_ref)


def _gla_kernel(q_ref, k_ref, v_ref, g_ref, lr_ref, w2_ref, b_ref, nw_ref, tri_ref, masks_ref, o_ref, st_ref):
    @pl.when(pl.program_id(2) == 0)
    def _():
        st_ref[...] = jnp.zeros_like(st_ref)

    def local(ci):
        rows = pl.ds(ci * CHUNK, CHUNK)
        u = jnp.dot(lr_ref[rows, :], w2_ref[...], preferred_element_type=F32) + b_ref[...]
        g = (jnp.minimum(u, 0.0) - jnp.log(1.0 + jnp.exp(-jnp.abs(u)))) * (1.0 / GLA_GATE_TEMP)
        q = q_ref[rows, :].astype(F32) * (GLA_HEAD_K ** -0.5)
        return q, k_ref[rows, :].astype(F32), v_ref[rows, :], g

    def gate(rows):
        gg = g_ref[rows, :].astype(F32)
        return gg * _sigmoid(gg)

    _scan_block(q_ref.shape[0] // CHUNK, local, gate, nw_ref, tri_ref, masks_ref, o_ref, st_ref)


def _const_spec(shape):
    nd = len(shape)
    return pl.BlockSpec(shape, lambda b, h, t: (0,) * nd)


def _hgrn(proj, lb, nw, tri, masks, *, batch, tb=1024):
    n = proj.shape[0]
    nt = n // batch // tb
    dk = HGRN_HEAD_DIM
    hh = HGRN_HEADS

    def col(off):
        return pl.BlockSpec((tb, dk), lambda b, h, t: (b * nt + t, off + h))

    return pl.pallas_call(
        _hgrn_kernel,
        out_shape=jax.ShapeDtypeStruct((n, hh * dk), BF16),
        grid=(batch, hh, nt),
        in_specs=[col(0), col(hh), col(2 * hh), col(3 * hh),
                  pl.BlockSpec((1, dk), lambda b, h, t: (0, h)),
                  pl.BlockSpec((1, dk), lambda b, h, t: (0, h)),
                  _const_spec(tri.shape), _const_spec(masks.shape)],
        out_specs=pl.BlockSpec((tb, dk), lambda b, h, t: (b * nt + t, h)),
        scratch_shapes=[pltpu.VMEM((dk, dk), F32)],
        compiler_params=_cparams(("parallel", "parallel", "arbitrary")),
        name="hgrn",
    )(proj, proj, proj, proj, lb.reshape(1, -1), nw.reshape(1, -1), tri, masks)


def _gla(proj, w2p, b, nw, tri, masks, *, batch, tb=1024):
    n = proj.shape[0]
    nt = n // batch // tb
    dk, dv, hh = GLA_HEAD_K, GLA_HEAD_V, GLA_HEADS
    q0 = 4096 // dk
    k0 = q0 + hh
    v0 = 5120 // dv
    g0 = _COL_GG // dv
    lr0 = _COL_GLR // LANES
    return pl.pallas_call(
        _gla_kernel,
        out_shape=jax.ShapeDtypeStruct((n, hh * dv), BF16),
        grid=(batch, hh, nt),
        in_specs=[pl.BlockSpec((tb, dk), lambda b, h, t: (b * nt + t, q0 + h)),
                  pl.BlockSpec((tb, dk), lambda b, h, t: (b * nt + t, k0 + h)),
                  pl.BlockSpec((tb, dv), lambda b, h, t: (b * nt + t, v0 + h)),
                  pl.BlockSpec((tb, dv), lambda b, h, t: (b * nt + t, g0 + h)),
                  pl.BlockSpec((tb, LANES), lambda b, h, t: (b * nt + t, lr0)),
                  pl.BlockSpec((LANES, dk), lambda b, h, t: (0, h)),
                  pl.BlockSpec((1, dk), lambda b, h, t: (0, h)),
                  pl.BlockSpec((1, dv), lambda b, h, t: (0, h)),
                  _const_spec(tri.shape), _const_spec(masks.shape)],
        out_specs=pl.BlockSpec((tb, dv), lambda b, h, t: (b * nt + t, h)),
        scratch_shapes=[pltpu.VMEM((dv, dk), F32)],
        compiler_params=_cparams(("parallel", "parallel", "arbitrary")),
        name="gla",
    )(proj, proj, proj, proj, proj, w2p, b.reshape(1, -1), nw.reshape(1, -1), tri, masks)


def _cast_rows(src_ref, dst_ref, row_chunk):
    def body(c, carry):
        r = pl.multiple_of(c * row_chunk, row_chunk)
        dst_ref[pl.ds(r, row_chunk), :] = src_ref[pl.ds(r, row_chunk), :].astype(dst_ref.dtype)
        return carry
    lax.fori_loop(0, src_ref.shape[0] // row_chunk, body, 0)


def _pack_bf16_pairs(a):
    c = a.shape[1] // 2
    bits = lax.bitcast_convert_type(a.astype(BF16).astype(F32), U32)
    return (bits[:, :c] >> 16) | (bits[:, c:] & U32(0xFFFF0000))


def _unpack_bf16_pairs(w):
    lo = lax.bitcast_convert_type(w << 16, F32)
    hi = lax.bitcast_convert_type(w & U32(0xFFFF0000), F32)
    return jnp.concatenate([lo, hi], axis=1)


def _out_proj_kernel(oh_ref, og_ref, res_ref, w_ref, nw_ref, *rest, with_router):
    if with_router:
        rw_ref, h_ref, hn_ref, lg_ref, wb_ref = rest
    else:
        h_ref, hn_ref, wb_ref = rest

    @pl.when(pl.program_id(0) == 0)
    def _():
        _cast_rows(w_ref, wb_ref, 256)

    kh = oh_ref.shape[1]
    acc = jnp.dot(oh_ref[...], wb_ref[0:kh, :], preferred_element_type=F32)
    acc = acc + jnp.dot(og_ref[...], wb_ref[kh:, :], preferred_element_type=F32)
    h = res_ref[...] + acc
    h_ref[...] = h
    hn = _rms_rows(h, nw_ref[...])
    hn_ref[...] = _pack_bf16_pairs(hn)
    if with_router:
        hn_hi = hn.astype(BF16)
        hn_lo = (hn - hn_hi.astype(F32)).astype(BF16)
        rw = rw_ref[...]
        rw_hi = rw.astype(BF16)
        rw_lo = (rw - rw_hi.astype(F32)).astype(BF16)
        lg = jnp.dot(hn_hi, rw_hi, preferred_element_type=F32)
        lg = lg + jnp.dot(hn_lo, rw_hi, preferred_element_type=F32)
        lg = lg + jnp.dot(hn_hi, rw_lo, preferred_element_type=F32)
        lg_ref[...] = lg


def _out_proj(o_h, o_g, res, w_out, layer, nw, router_pad=None, *, tm=256):
    n, d = res.shape
    kh, kg = o_h.shape[1], o_g.shape[1]
    with_router = router_pad is not None
    row = lambda i: (i, 0)
    fixed = lambda i: (0, 0)
    in_specs = [pl.BlockSpec((tm, kh), row), pl.BlockSpec((tm, kg), row), pl.BlockSpec((tm, d), row),
                pl.BlockSpec((None, kh + kg, d), lambda i: (layer, 0, 0), pipeline_mode=pl.Buffered(1)),
                pl.BlockSpec((1, d), fixed)]
    args = [o_h, o_g, res, w_out, nw.reshape(1, d)]
    out_shape = [jax.ShapeDtypeStruct((n, d), F32), jax.ShapeDtypeStruct((n, d // 2), U32)]
    out_specs = [pl.BlockSpec((tm, d), row), pl.BlockSpec((tm, d // 2), row)]
    if with_router:
        in_specs.append(pl.BlockSpec((d, LANES), fixed))
        args.append(router_pad)
        out_shape.append(jax.ShapeDtypeStruct((n, LANES), F32))
        out_specs.append(pl.BlockSpec((tm, LANES), row))
    return pl.pallas_call(
        functools.partial(_out_proj_kernel, with_router=with_router),
        out_shape=out_shape,
        grid=(n // tm,),
        in_specs=in_specs,
        out_specs=out_specs,
        scratch_shapes=[pltpu.VMEM((kh + kg, d), BF16)],
        compiler_params=_cparams(("arbitrary",)),
        name="out_proj",
    )(*args)


def _dispatch_kernel(zs_ref, dest_ref, hn_ref, xs_ref, zero_ref, sem, *, tb, block):
    i = pl.program_id(0)

    def zero_copy(e):
        start = pl.multiple_of(zs_ref[e] * block, block)
        return pltpu.make_async_copy(zero_ref, xs_ref.at[pl.ds(start, block)], sem.at[1])

    @pl.when(i == 0)
    def _():
        zero_ref[...] = jnp.zeros_like(zero_ref)
        for e in range(zs_ref.shape[0]):
            @pl.when(zs_ref[e] >= 0)
            def _():
                zero_copy(e).start()
        for e in range(zs_ref.shape[0]):
            @pl.when(zs_ref[e] >= 0)
            def _():
                zero_copy(e).wait()

    def body(r, carry):
        src = hn_ref.at[pl.ds(r, 1)]
        for k in range(TOP_K):
            pltpu.make_async_copy(src, xs_ref.at[pl.ds(dest_ref[0, TOP_K * r + k], 1)], sem.at[0]).start()
        return carry

    lax.fori_loop(0, tb, body, 0, unroll=8)
    for k in range(TOP_K):
        pltpu.make_async_copy(hn_ref, xs_ref.at[pl.ds(0, tb)], sem.at[0]).wait()


def _dispatch(hn_packed, dest, zero_start, cap, *, block, tb=512):
    n, dw = hn_packed.shape
    tb = min(tb, n)
    grid_spec = pltpu.PrefetchScalarGridSpec(
        num_scalar_prefetch=1,
        grid=(n // tb,),
        in_specs=[pl.BlockSpec((None, 1, TOP_K * tb), lambda i, zs: (i, 0, 0), memory_space=pltpu.SMEM),
                  pl.BlockSpec((tb, dw), lambda i, zs: (i, 0))],
        out_specs=pl.BlockSpec(memory_space=pl.ANY),
        scratch_shapes=[pltpu.VMEM((block, dw), U32), pltpu.SemaphoreType.DMA((2,))],
    )
    return pl.pallas_call(
        functools.partial(_dispatch_kernel, tb=tb, block=block),
        out_shape=jax.ShapeDtypeStruct((cap, dw), U32),
        grid_spec=grid_spec,
        compiler_params=_cparams(("arbitrary",)),
        name="dispatch",
    )(zero_start, dest.reshape(n // tb, 1, TOP_K * tb), hn_packed)


def _ffn_kernel(be_ref, nused_ref, x_ref, wg_ref, wu_ref, wd_ref, o_ref, xb_ref, acc_ref, *, row_chunk):
    i = pl.program_id(0)
    f = pl.program_id(1)
    used = i < nused_ref[0]
    n_chunks = x_ref.shape[0] // row_chunk

    @pl.when(f == 0)
    def _():
        acc_ref[...] = jnp.zeros_like(acc_ref)

    @pl.when(jnp.logical_and(f == 0, used))
    def _():
        def body(c, carry):
            rows = pl.ds(pl.multiple_of(c * row_chunk, row_chunk), row_chunk)
            xb_ref[rows, :] = _unpack_bf16_pairs(x_ref[rows, :]).astype(BF16)
            return carry
        lax.fori_loop(0, n_chunks, body, 0)

    @pl.when(used)
    def _():
        x = xb_ref[...]
        g = jnp.dot(x, wg_ref[0].astype(BF16), preferred_element_type=F32)
        u = jnp.dot(x, wu_ref[0].astype(BF16), preferred_element_type=F32)
        a = (g * _sigmoid(g) * u).astype(BF16)
        acc_ref[...] += jnp.dot(a, wd_ref[0].astype(BF16), preferred_element_type=F32)

    @pl.when(f == pl.num_programs(1) - 1)
    def _():
        def body(c, carry):
            rows = pl.ds(pl.multiple_of(c * row_chunk, row_chunk), row_chunk)
            o_ref[rows, :] = _pack_bf16_pairs(acc_ref[rows, :])
            return carry
        lax.fori_loop(0, n_chunks, body, 0)


def _ffn(x_packed, wg, wu, wd, block_e, nused, *, tm=1024, tf=512):
    r, dw = x_packed.shape
    d = 2 * dw
    ff = wg.shape[2]
    nf = ff // tf

    def f_eff(i, f, nu):
        return jnp.where(i < nu[0], f, nf - 1)

    grid_spec = pltpu.PrefetchScalarGridSpec(
        num_scalar_prefetch=2,
        grid=(r // tm, nf),
        in_specs=[
            pl.BlockSpec((tm, dw), lambda i, f, be, nu: (i, 0), pipeline_mode=pl.Buffered(1)),
            pl.BlockSpec((1, d, tf), lambda i, f, be, nu: (be[i], 0, f_eff(i, f, nu))),
            pl.BlockSpec((1, d, tf), lambda i, f, be, nu: (be[i], 0, f_eff(i, f, nu))),
            pl.BlockSpec((1, tf, d), lambda i, f, be, nu: (be[i], f_eff(i, f, nu), 0)),
        ],
        out_specs=pl.BlockSpec((tm, dw), lambda i, f, be, nu: (i, 0)),
        scratch_shapes=[pltpu.VMEM((tm, d), BF16), pltpu.VMEM((tm, d), F32)],
    )
    return pl.pallas_call(
        functools.partial(_ffn_kernel, row_chunk=256),
        out_shape=jax.ShapeDtypeStruct((r, dw), U32),
        grid_spec=grid_spec,
        compiler_params=_cparams(("parallel", "arbitrary")),
        name="ffn",
    )(block_e, nused, x_packed, wg, wu, wd)


def _ple_tail(h2, p_ref, nw_ref, fw_ref, o_ref, wgb_ref, wpb_ref, final):
    hn = _rms_rows(h2, nw_ref[...]).astype(BF16)
    gate = _sigmoid(jnp.dot(hn, wgb_ref[...], preferred_element_type=F32))
    pp = jnp.dot(p_ref[...].astype(BF16), wpb_ref[...], preferred_element_type=F32)
    h3 = h2 + gate * pp
    if final:
        h3 = _rms_rows(h3, fw_ref[...])
    o_ref[...] = h3


def _ple_dense_kernel(h_ref, y_ref, p_ref, nw_ref, wg_ref, wp_ref, fw_ref, o_ref, wgb_ref, wpb_ref, *, final):
    @pl.when(pl.program_id(0) == 0)
    def _():
        _cast_rows(wg_ref, wgb_ref, 256)
        _cast_rows(wp_ref, wpb_ref, 256)

    h2 = h_ref[...] + _unpack_bf16_pairs(y_ref[...])
    _ple_tail(h2, p_ref, nw_ref, fw_ref, o_ref, wgb_ref, wpb_ref, final)


def _ple_moe_kernel(dcur_ref, dnext_ref, h_ref, ys_ref, tw_ref, p_ref, nw_ref, wg_ref, wp_ref, fw_ref, o_ref,
                    wgb_ref, wpb_ref, gbuf_ref, sem, *, final):
    i = pl.program_id(0)
    n_steps = pl.num_programs(0)
    tm = h_ref.shape[0]
    slot = i % 2

    def issue(d_ref, s):
        def body(r, carry):
            for k in range(TOP_K):
                pltpu.make_async_copy(ys_ref.at[pl.ds(d_ref[0, TOP_K * r + k], 1)],
                                      gbuf_ref.at[s, k, pl.ds(r, 1)], sem.at[s]).start()
            return carry
        lax.fori_loop(0, tm, body, 0, unroll=8)

    @pl.when(i == 0)
    def _():
        issue(dcur_ref, 0)
        _cast_rows(wg_ref, wgb_ref, 256)
        _cast_rows(wp_ref, wpb_ref, 256)

    @pl.when(i + 1 < n_steps)
    def _():
        issue(dnext_ref, 1 - slot)

    for k in range(TOP_K):
        pltpu.make_async_copy(ys_ref.at[pl.ds(0, tm)], gbuf_ref.at[slot, k], sem.at[slot]).wait()
    tw = tw_ref[...]
    y = tw[:, 0:1] * _unpack_bf16_pairs(gbuf_ref[slot, 0])
    for k in range(1, TOP_K):
        y = y + tw[:, k:k + 1] * _unpack_bf16_pairs(gbuf_ref[slot, k])
    _ple_tail(h_ref[...] + y, p_ref, nw_ref, fw_ref, o_ref, wgb_ref, wpb_ref, final)


def _ple(h1, y_packed, p, layer, nw, w_gate, w_proj, final_w, *, final, dest=None, top_w=None, tm=256):
    n, d = h1.shape
    pd = p.shape[2]
    dw = d // 2
    n_steps = n // tm
    row = lambda i: (i, 0)
    fixed = lambda i: (0, 0)
    tail_specs = [pl.BlockSpec((None, tm, pd), lambda i: (layer, i, 0)),
                  pl.BlockSpec((1, d), fixed),
                  pl.BlockSpec((None, d, d), lambda i: (layer, 0, 0), pipeline_mode=pl.Buffered(1)),
                  pl.BlockSpec((None, pd, d), lambda i: (layer, 0, 0), pipeline_mode=pl.Buffered(1)),
                  pl.BlockSpec((1, d), fixed)]
    tail_args = [p, nw.reshape(1, d), w_gate, w_proj, final_w.reshape(1, d)]
    scratch = [pltpu.VMEM((d, d), BF16), pltpu.VMEM((pd, d), BF16)]
    if dest is None:
        kern = functools.partial(_ple_dense_kernel, final=final)
        in_specs = [pl.BlockSpec((tm, d), row), pl.BlockSpec((tm, dw), row)] + tail_specs
        args = [h1, y_packed] + tail_args
    else:
        kern = functools.partial(_ple_moe_kernel, final=final)
        dest3 = dest.reshape(n_steps, 1, TOP_K * tm)
        smem = lambda imap: pl.BlockSpec((None, 1, TOP_K * tm), imap, memory_space=pltpu.SMEM)
        in_specs = [smem(lambda i: (i, 0, 0)), smem(lambda i: (jnp.minimum(i + 1, n_steps - 1), 0, 0)),
                    pl.BlockSpec((tm, d), row), pl.BlockSpec(memory_space=pl.ANY),
                    pl.BlockSpec((tm, TOP_K), row)] + tail_specs
        args = [dest3, dest3, h1, y_packed, top_w] + tail_args
        scratch = scratch + [pltpu.VMEM((2, TOP_K, tm, dw), U32), pltpu.SemaphoreType.DMA((2,))]
    return pl.pallas_call(
        kern,
        out_shape=jax.ShapeDtypeStruct((n, d), F32),
        grid=(n_steps,),
        in_specs=in_specs,
        out_specs=pl.BlockSpec((tm, d), row),
        scratch_shapes=scratch,
        compiler_params=_cparams(("arbitrary",)),
        name="ple",
    )(*args)


def _route(logits, block):
    n = logits.shape[0]
    top_logit, top_e = lax.top_k(logits, TOP_K)
    top_w = jax.nn.softmax(top_logit, axis=-1)
    flat_e = top_e.reshape(-1)
    onehot = (flat_e[:, None] == jnp.arange(N_EXPERTS, dtype=flat_e.dtype)[None, :]).astype(jnp.int32)
    csum = jnp.cumsum(onehot, axis=0)
    counts = csum[-1]
    rank = jnp.sum((csum - onehot) * onehot, axis=1)
    padded = (counts + block - 1) // block * block
    padded_end = jnp.cumsum(padded)
    padded_start = padded_end - padded
    dest = (padded_start[flat_e] + rank).astype(jnp.int32)
    n_blocks = -(-(n * TOP_K) // block) + N_EXPERTS
    block_e = jnp.minimum(jnp.searchsorted(padded_end, jnp.arange(n_blocks, dtype=jnp.int32) * block, side='right'),
                          N_EXPERTS - 1).astype(jnp.int32)
    nused = (padded_end[-1] // block).astype(jnp.int32).reshape(1)
    block_e = jnp.where(jnp.arange(n_blocks) < nused[0], block_e, block_e[jnp.maximum(nused[0] - 1, 0)])
    tail = nused[0] + jnp.arange(N_EXPERTS, dtype=jnp.int32)
    zero_start = jnp.concatenate([jnp.where(padded > 0, padded_end // block - 1, -1),
                                  jnp.where(tail < n_blocks, tail, -1)]).astype(jnp.int32)
    return dest.reshape(n, TOP_K), top_w, block_e, nused, zero_start, n_blocks * block


def kernel(x, p, norm_mix_w, w_in, hgrn_lb_logits, hgrn_norm_w, gla_gate_w2, gla_gate_b, gla_norm_w, w_out,
           norm_ffn_w, dense_w_gate, dense_w_up, dense_w_down, moe_router, moe_w_gate, moe_w_up, moe_w_down,
           norm_ple_w, ple_w_gate, ple_w_proj, final_norm_w):
    batch, seq, d = x.shape
    depth = w_in.shape[0]
    n = batch * seq
    moe_block = 1024

    lbs = jnp.cumsum(jax.nn.softmax(hgrn_lb_logits.astype(F32), axis=0), axis=0)
    lbs = lbs - lbs[0]
    tri_np, masks_np = _level_tables()
    tri = jnp.asarray(tri_np, BF16)
    masks = jnp.asarray(masks_np, BF16)

    h = x.reshape(n, d)
    p3 = p.reshape(depth, n, -1)
    moe_wg = moe_w_gate.reshape((-1,) + moe_w_gate.shape[2:])
    moe_wu = moe_w_up.reshape((-1,) + moe_w_up.shape[2:])
    moe_wd = moe_w_down.reshape((-1,) + moe_w_down.shape[2:])
    gg0 = _COL_GG + GLA_GATE_RANK
    w_tails = jnp.concatenate(
        [w_in[:, :, gg0:], w_in[:, :, _COL_GG:gg0],
         jnp.zeros((depth, d, _PROJ_COLS - _COL_GLR - GLA_GATE_RANK), F32)], axis=2)
    for i in range(depth):
        proj = _in_proj(h, norm_mix_w[i], w_in, i, w_tails[i])
        o_h = _hgrn(proj, lbs[i], hgrn_norm_w[i], tri, masks, batch=batch)
        w2p = jnp.zeros((LANES, gla_gate_w2.shape[2]), F32).at[:GLA_GATE_RANK].set(gla_gate_w2[i]).astype(BF16)
        o_g = _gla(proj, w2p, gla_gate_b[i], gla_norm_w[i], tri, masks, batch=batch)
        j = i // 2
        last = i == depth - 1
        if i % 2 == 0:
            h1, hn = _out_proj(o_h, o_g, h, w_out, i, norm_ffn_w[i])
            n_blk = n // moe_block
            y = _ffn(hn, dense_w_gate, dense_w_up, dense_w_down,
                     jnp.full((n_blk,), j, jnp.int32), jnp.full((1,), n_blk, jnp.int32), tm=moe_block)
            h = _ple(h1, y, p3, i, norm_ple_w[i], ple_w_gate, ple_w_proj, final_norm_w, final=last)
        else:
            router_pad = jnp.zeros((d, LANES), F32).at[:, :N_EXPERTS].set(moe_router[j])
            h1, hn, logits = _out_proj(o_h, o_g, h, w_out, i, norm_ffn_w[i], router_pad)
            dest, top_w, block_e, nused, zero_start, cap = _route(logits[:, :N_EXPERTS], moe_block)
            xs = _dispatch(hn, dest, zero_start, cap, block=moe_block)
            ys = _ffn(xs, moe_wg, moe_wu, moe_wd, block_e + j * N_EXPERTS, nused, tm=moe_block)
            h = _ple(h1, ys, p3, i, norm_ple_w[i], ple_w_gate, ple_w_proj, final_norm_w, final=last,
                     dest=dest, top_w=top_w)
    return h.reshape(batch, seq, d)
```

```python
import functools

import numpy as np
import jax
import jax.numpy as jnp
from jax import lax
from jax.experimental import pallas as pl
from jax.experimental.pallas import tpu as pltpu

F32 = jnp.float32
BF16 = jnp.bfloat16
U32 = jnp.uint32

EPS = 1e-6
LOG2_E = 1.4426950408889634
HGRN_HEADS = 8
HGRN_HEAD_DIM = 128
GLA_HEADS = 4
GLA_HEAD_K = 128
GLA_HEAD_V = 256
GLA_GATE_RANK = 16
GLA_GATE_TEMP = 16.0
N_EXPERTS = 8
TOP_K = 2

LANES = 128
CHUNK = 128
N_LEVELS = 7
VMEM_LIMIT = 56 * 1024 * 1024

_N_MAIN = 6144
_COL_GG = 6144
_COL_GLR = 7168
_PROJ_COLS = 7680


def _cparams(sem):
    return pltpu.CompilerParams(dimension_semantics=sem, vmem_limit_bytes=VMEM_LIMIT)


def _sigmoid(x):
    return 1.0 / (1.0 + jnp.exp(-x))


def _rms_rows(x, w):
    ms = jnp.mean(x * x, axis=-1, keepdims=True)
    return x * lax.rsqrt(ms + EPS) * w


def _in_proj_kernel(x_ref, nw_ref, w1_ref, w2_ref, o_ref, xn_ref, *, n_main, row_chunk):
    j = pl.program_id(1)
    tm = x_ref.shape[0]

    @pl.when(j == 0)
    def _():
        def body(c, carry):
            r = pl.multiple_of(c * row_chunk, row_chunk)
            xn_ref[pl.ds(r, row_chunk), :] = _rms_rows(x_ref[pl.ds(r, row_chunk), :], nw_ref[...]).astype(BF16)
            return carry
        lax.fori_loop(0, tm // row_chunk, body, 0)

    @pl.when(j < n_main)
    def _():
        o_ref[...] = jnp.dot(xn_ref[...], w1_ref[...], preferred_element_type=F32).astype(o_ref.dtype)

    @pl.when(j >= n_main)
    def _():
        o_ref[...] = jnp.dot(xn_ref[...], w2_ref[...], preferred_element_type=F32).astype(o_ref.dtype)


def _in_proj(x, nw, w_main, w_tail, layer, *, tm=1024, tn=512):
    n, d = x.shape
    n_main = w_main.shape[2] // tn
    n_tail = w_tail.shape[2] // tn
    kern = functools.partial(_in_proj_kernel, n_main=n_main, row_chunk=128)
    return pl.pallas_call(
        kern,
        out_shape=jax.ShapeDtypeStruct((n, (n_main + n_tail) * tn), BF16),
        grid=(n // tm, n_main + n_tail),
        in_specs=[
            pl.BlockSpec((tm, d), lambda i, j: (i, 0)),
            pl.BlockSpec((1, d), lambda i, j: (0, 0)),
            pl.BlockSpec((None, d, tn), lambda i, j: (layer, 0, jnp.minimum(j, n_main - 1))),
            pl.BlockSpec((None, d, tn), lambda i, j: (layer, 0, jnp.maximum(j - n_main, 0))),
        ],
        out_specs=pl.BlockSpec((tm, tn), lambda i, j: (i, j)),
        scratch_shapes=[pltpu.VMEM((tm, d), BF16)],
        compiler_params=_cparams(("parallel", "arbitrary")),
        name="in_proj",
    )(x, nw.reshape(1, d), w_main, w_tail)


def _level_tables():
    c = CHUNK
    idx = np.arange(c)
    masks = []
    s = c // 2
    while s >= 1:
        blk = idx // (2 * s)
        upper = (idx % (2 * s)) >= s
        masks.append(((blk[:, None] == blk[None, :]) & upper[:, None] & (~upper)[None, :]).astype(np.float32))
        s //= 2
    masks.append(np.eye(c, dtype=np.float32))
    tri = np.tril(np.ones((c, c), np.float32))
    return np.concatenate([tri, tri], axis=1), np.stack(masks)


def _neg_abs(d):
    return lax.bitcast_convert_type(lax.bitcast_convert_type(d, U32) | U32(0x80000000), F32)


def _level_exponents(g_cum):
    c, dk = g_cum.shape
    row = lax.broadcasted_iota(jnp.int32, (c, dk), 0)
    out = []
    s = c // 2
    while s >= 4:
        nb = c // (2 * s)
        mid = g_cum.reshape(nb, 2 * s, dk)[:, s - 1:s, :]
        mid = jnp.broadcast_to(mid, (nb, 2 * s, dk)).reshape(c, dk)
        out.append(_neg_abs(g_cum - mid))
        s //= 2
    up1 = pltpu.roll(g_cum, 1, 0)
    up2 = pltpu.roll(g_cum, 2, 0)
    dn1 = pltpu.roll(g_cum, c - 1, 0)
    r4 = row % 4
    mid2 = jnp.where(r4 == 0, dn1, jnp.where(r4 == 1, g_cum, jnp.where(r4 == 2, up1, up2)))
    out.append(_neg_abs(g_cum - mid2))
    mid1 = jnp.where(row % 2 == 1, up1, g_cum)
    out.append(_neg_abs(g_cum - mid1))
    return out


_NT = (((1,), (1,)), ((), ()))
_TN = (((0,), (0,)), ((), ()))


def _chunk_prep(g, tri_ref):
    g_hi = g.astype(BF16)
    g_lo = (g - g_hi.astype(F32)).astype(BF16)
    return jnp.dot(tri_ref[...], jnp.concatenate([g_hi, g_lo], axis=0), preferred_element_type=F32) * LOG2_E


def _chunk_local(q, k, v_bf, g_cum, masks_ref):
    c = CHUNK
    q_bf = q.astype(BF16)
    k_bf = k.astype(BF16)
    a = masks_ref[N_LEVELS] * lax.dot_general(q_bf, k_bf, _NT, preferred_element_type=F32).astype(BF16)
    for l, ex in enumerate(_level_exponents(g_cum)):
        e = jnp.exp2(ex).astype(BF16)
        s = lax.dot_general(q_bf * e, k_bf * e, _NT, preferred_element_type=F32)
        a = a + masks_ref[l] * s.astype(BF16)
    g_last = g_cum[c - 1:c, :]
    o_intra = jnp.dot(a, v_bf, preferred_element_type=F32)
    q_dec = (q * jnp.exp2(g_cum)).astype(BF16)
    k_dec = (k * jnp.exp2(g_last - g_cum)).astype(BF16)
    kv = lax.dot_general(v_bf, k_dec, _TN, preferred_element_type=F32)
    return o_intra, q_dec, kv, jnp.exp2(g_last)


def _head_norm(o, w):
    ms = jnp.mean(o * o, axis=-1, keepdims=True)
    return o * lax.rsqrt(ms + EPS) * w


def _scan_block(n_chunks, gates_fn, gate_fn, nw_ref, tri_ref, masks_ref, o_ref, st_ref):
    gates = [gates_fn(ci) for ci in range(n_chunks)]
    cums = [_chunk_prep(g, tri_ref) for (_, _, _, g) in gates]
    parts = [_chunk_local(q, k, v_bf, g_cum, masks_ref) for (q, k, v_bf, _), g_cum in zip(gates, cums)]
    st = st_ref[...]
    for ci, (o_intra, q_dec, kv, decay) in enumerate(parts):
        o = o_intra + lax.dot_general(q_dec, st.astype(BF16), _NT, preferred_element_type=F32)
        rows = pl.ds(ci * CHUNK, CHUNK)
        o_ref[rows, :] = (_head_norm(o, nw_ref[...]) * gate_fn(rows)).astype(o_ref.dtype)
        st = st * decay + kv
    st_ref[...] = st


def _hgrn_kernel(q_ref, f_ref, i_ref, g_ref, lb_ref, nw_ref, tri_ref, masks_ref, o_ref, st_ref):
    @pl.when(pl.program_id(2) == 0)
    def _():
        st_ref[...] = jnp.zeros_like(st_ref)

    lb = lb_ref[...]

    def local(ci):
        rows = pl.ds(ci * CHUNK, CHUNK)
        fgate = lb + (1.0 - lb) * _sigmoid(f_ref[rows, :].astype(F32))
        g = jnp.log(jnp.maximum(fgate, 1e-38))
        k = 1.0 - fgate
        hq = q_ref[rows, :].astype(F32)
        return hq * _sigmoid(hq), k, i_ref[rows, :], g

    def gate(rows):
        return _sigmoid(g_ref[rows, :].astype(F32))

    _scan_block(q_ref.shape[0] // CHUNK, local, gate, nw_ref, tri_ref, masks_ref, o_ref, st_ref)


def _gla_kernel(q_ref, k_ref, v_ref, g_ref, lr_ref, w2_ref, b_ref, nw_ref, tri_ref, masks_ref, o_ref, st_ref):
    @pl.when(pl.program_id(2) == 0)
    def _():
        st_ref[...] = jnp.zeros_like(st_ref)

    def local(ci):
        rows = pl.ds(ci * CHUNK, CHUNK)
        u = jnp.dot(lr_ref[rows, :], w2_ref[...], preferred_element_type=F32) + b_ref[...]
        g = (jnp.minimum(u, 0.0) - jnp.log(1.0 + jnp.exp(-jnp.abs(u)))) * (1.0 / GLA_GATE_TEMP)
        q = q_ref[rows, :].astype(F32) * (GLA_HEAD_K ** -0.5)
        return q, k_ref[rows, :].astype(F32), v_ref[rows, :], g

    def gate(rows):
        gg = g_ref[rows, :].astype(F32)
        return gg * _sigmoid(gg)

    _scan_block(q_ref.shape[0] // CHUNK, local, gate, nw_ref, tri_ref, masks_ref, o_ref, st_ref)


def _const_spec(shape):
    nd = len(shape)
    return pl.BlockSpec(shape, lambda b, h, t: (0,) * nd)


def _hgrn(proj, lb, nw, tri, masks, *, batch, tb=1024):
    n = proj.shape[0]
    nt = n // batch // tb
    dk = HGRN_HEAD_DIM
    hh = HGRN_HEADS

    def col(off):
        return pl.BlockSpec((tb, dk), lambda b, h, t: (b * nt + t, off + h))

    return pl.pallas_call(
        _hgrn_kernel,
        out_shape=jax.ShapeDtypeStruct((n, hh * dk), BF16),
        grid=(batch, hh, nt),
        in_specs=[col(0), col(hh), col(2 * hh), col(3 * hh),
                  pl.BlockSpec((1, dk), lambda b, h, t: (0, h)),
                  pl.BlockSpec((1, dk), lambda b, h, t: (0, h)),
                  _const_spec(tri.shape), _const_spec(masks.shape)],
        out_specs=pl.BlockSpec((tb, dk), lambda b, h, t: (b * nt + t, h)),
        scratch_shapes=[pltpu.VMEM((dk, dk), F32)],
        compiler_params=_cparams(("parallel", "parallel", "arbitrary")),
        name="hgrn",
    )(proj, proj, proj, proj, lb.reshape(1, -1), nw.reshape(1, -1), tri, masks)


def _gla(proj, w2p, b, nw, tri, masks, *, batch, tb=1024):
    n = proj.shape[0]
    nt = n // batch // tb
    dk, dv, hh = GLA_HEAD_K, GLA_HEAD_V, GLA_HEADS
    q0 = 4096 // dk
    k0 = q0 + hh
    v0 = 5120 // dv
    g0 = _COL_GG // dv
    lr0 = _COL_GLR // LANES
    return pl.pallas_call(
        _gla_kernel,
        out_shape=jax.ShapeDtypeStruct((n, hh * dv), BF16),
        grid=(batch, hh, nt),
        in_specs=[pl.BlockSpec((tb, dk), lambda b, h, t: (b * nt + t, q0 + h)),
                  pl.BlockSpec((tb, dk), lambda b, h, t: (b * nt + t, k0 + h)),
                  pl.BlockSpec((tb, dv), lambda b, h, t: (b * nt + t, v0 + h)),
                  pl.BlockSpec((tb, dv), lambda b, h, t: (b * nt + t, g0 + h)),
                  pl.BlockSpec((tb, LANES), lambda b, h, t: (b * nt + t, lr0)),
                  pl.BlockSpec((LANES, dk), lambda b, h, t: (0, h)),
                  pl.BlockSpec((1, dk), lambda b, h, t: (0, h)),
                  pl.BlockSpec((1, dv), lambda b, h, t: (0, h)),
                  _const_spec(tri.shape), _const_spec(masks.shape)],
        out_specs=pl.BlockSpec((tb, dv), lambda b, h, t: (b * nt + t, h)),
        scratch_shapes=[pltpu.VMEM((dv, dk), F32)],
        compiler_params=_cparams(("parallel", "parallel", "arbitrary")),
        name="gla",
    )(proj, proj, proj, proj, proj, w2p, b.reshape(1, -1), nw.reshape(1, -1), tri, masks)


def _pack_bf16_pairs(a):
    c = a.shape[1] // 2
    bits = lax.bitcast_convert_type(a.astype(BF16).astype(F32), U32)
    return (bits[:, :c] >> 16) | (bits[:, c:] & U32(0xFFFF0000))


def _unpack_bf16_pairs(w):
    lo = lax.bitcast_convert_type(w << 16, F32)
    hi = lax.bitcast_convert_type(w & U32(0xFFFF0000), F32)
    return jnp.concatenate([lo, hi], axis=1)


def _out_proj_kernel(oh_ref, og_ref, res_ref, w_ref, nw_ref, *rest, with_router):
    if with_router:
        rw_ref, h_ref, hn_ref, lg_ref = rest
    else:
        h_ref, hn_ref = rest

    kh = oh_ref.shape[1]
    acc = jnp.dot(oh_ref[...], w_ref[0:kh, :], preferred_element_type=F32)
    acc = acc + jnp.dot(og_ref[...], w_ref[kh:, :], preferred_element_type=F32)
    h = res_ref[...] + acc
    h_ref[...] = h
    hn = _rms_rows(h, nw_ref[...])
    hn_ref[...] = _pack_bf16_pairs(hn)
    if with_router:
        hn_hi = hn.astype(BF16)
        hn_lo = (hn - hn_hi.astype(F32)).astype(BF16)
        rw = rw_ref[...]
        rw_hi = rw.astype(BF16)
        rw_lo = (rw - rw_hi.astype(F32)).astype(BF16)
        lg = jnp.dot(hn_hi, rw_hi, preferred_element_type=F32)
        lg = lg + jnp.dot(hn_lo, rw_hi, preferred_element_type=F32)
        lg = lg + jnp.dot(hn_hi, rw_lo, preferred_element_type=F32)
        lg_ref[...] = lg


def _out_proj(o_h, o_g, res, w_out, layer, nw, router_pad=None):
    n, d = res.shape
    kh, kg = o_h.shape[1], o_g.shape[1]
    with_router = router_pad is not None
    tm = 256 if with_router else 512
    row = lambda i: (i, 0)
    fixed = lambda i: (0, 0)
    in_specs = [pl.BlockSpec((tm, kh), row), pl.BlockSpec((tm, kg), row), pl.BlockSpec((tm, d), row),
                pl.BlockSpec((None, kh + kg, d), lambda i: (layer, 0, 0), pipeline_mode=pl.Buffered(1)),
                pl.BlockSpec((1, d), fixed)]
    args = [o_h, o_g, res, w_out, nw.reshape(1, d)]
    out_shape = [jax.ShapeDtypeStruct((n, d), F32), jax.ShapeDtypeStruct((n, d // 2), U32)]
    out_specs = [pl.BlockSpec((tm, d), row), pl.BlockSpec((tm, d // 2), row)]
    if with_router:
        in_specs.append(pl.BlockSpec((d, LANES), fixed))
        args.append(router_pad)
        out_shape.append(jax.ShapeDtypeStruct((n, LANES), F32))
        out_specs.append(pl.BlockSpec((tm, LANES), row))
    return pl.pallas_call(
        functools.partial(_out_proj_kernel, with_router=with_router),
        out_shape=out_shape,
        grid=(n // tm,),
        in_specs=in_specs,
        out_specs=out_specs,
        compiler_params=_cparams(("parallel",)),
        name="out_proj",
    )(*args)


def _dispatch_kernel(zs_ref, dest_ref, hn_ref, xs_ref, zero_ref, sem, *, tb, block):
    i = pl.program_id(0)

    def zero_copy(e):
        start = pl.multiple_of(zs_ref[e] * block, block)
        return pltpu.make_async_copy(zero_ref, xs_ref.at[pl.ds(start, block)], sem.at[1])

    @pl.when(i == 0)
    def _():
        zero_ref[...] = jnp.zeros_like(zero_ref)
        for e in range(zs_ref.shape[0]):
            @pl.when(zs_ref[e] >= 0)
            def _():
                zero_copy(e).start()
        for e in range(zs_ref.shape[0]):
            @pl.when(zs_ref[e] >= 0)
            def _():
                zero_copy(e).wait()

    def body(r, carry):
        src = hn_ref.at[pl.ds(r, 1)]
        for k in range(TOP_K):
            pltpu.make_async_copy(src, xs_ref.at[pl.ds(dest_ref[0, TOP_K * r + k], 1)], sem.at[0]).start()
        return carry

    lax.fori_loop(0, tb, body, 0, unroll=8)
    for k in range(TOP_K):
        pltpu.make_async_copy(hn_ref, xs_ref.at[pl.ds(0, tb)], sem.at[0]).wait()


def _dispatch(hn_packed, dest, zero_start, cap, *, block, tb=512):
    n, dw = hn_packed.shape
    tb = min(tb, n)
    grid_spec = pltpu.PrefetchScalarGridSpec(
        num_scalar_prefetch=1,
        grid=(n // tb,),
        in_specs=[pl.BlockSpec((None, 1, TOP_K * tb), lambda i, zs: (i, 0, 0), memory_space=pltpu.SMEM),
                  pl.BlockSpec((tb, dw), lambda i, zs: (i, 0))],
        out_specs=pl.BlockSpec(memory_space=pl.ANY),
        scratch_shapes=[pltpu.VMEM((block, dw), U32), pltpu.SemaphoreType.DMA((2,))],
    )
    return pl.pallas_call(
        functools.partial(_dispatch_kernel, tb=tb, block=block),
        out_shape=jax.ShapeDtypeStruct((cap, dw), U32),
        grid_spec=grid_spec,
        compiler_params=_cparams(("arbitrary",)),
        name="dispatch",
    )(zero_start, dest.reshape(n // tb, 1, TOP_K * tb), hn_packed)


def _ffn_kernel(be_ref, nused_ref, nrows_ref, x_ref, wg_ref, wu_ref, wd_ref, o_ref, xb_ref, acc_ref, *, row_chunk):
    i = pl.program_id(0)
    f = pl.program_id(1)
    used = i < nused_ref[0]
    n_chunks = x_ref.shape[0] // row_chunk

    @pl.when(f == 0)
    def _():
        acc_ref[...] = jnp.zeros_like(acc_ref)

    @pl.when(jnp.logical_and(f == 0, used))
    def _():
        def body(c, carry):
            rows = pl.ds(pl.multiple_of(c * row_chunk, row_chunk), row_chunk)
            xb_ref[rows, :] = _unpack_bf16_pairs(x_ref[rows, :]).astype(BF16)
            return carry
        lax.fori_loop(0, n_chunks, body, 0)

    live = (nrows_ref[i] + row_chunk - 1) // row_chunk
    for nc in range(1, n_chunks + 1):
        @pl.when(jnp.logical_and(used, live == nc))
        def _():
            rows = pl.ds(0, nc * row_chunk)
            x = xb_ref[rows, :]
            g = jnp.dot(x, wg_ref[0].astype(BF16), preferred_element_type=F32)
            u = jnp.dot(x, wu_ref[0].astype(BF16), preferred_element_type=F32)
            a = (g * _sigmoid(g) * u).astype(BF16)
            acc_ref[rows, :] += jnp.dot(a, wd_ref[0].astype(BF16), preferred_element_type=F32)

    @pl.when(f == pl.num_programs(1) - 1)
    def _():
        def body(c, carry):
            rows = pl.ds(pl.multiple_of(c * row_chunk, row_chunk), row_chunk)
            o_ref[rows, :] = _pack_bf16_pairs(acc_ref[rows, :])
            return carry
        lax.fori_loop(0, n_chunks, body, 0)


def _ffn(x_packed, wg, wu, wd, block_e, nused, nrows, *, tm=1024, tf=512):
    r, dw = x_packed.shape
    d = 2 * dw
    ff = wg.shape[2]
    nf = ff // tf

    def f_eff(i, f, nu):
        return jnp.where(i < nu[0], f, nf - 1)

    grid_spec = pltpu.PrefetchScalarGridSpec(
        num_scalar_prefetch=3,
        grid=(r // tm, nf),
        in_specs=[
            pl.BlockSpec((tm, dw), lambda i, f, be, nu, nr: (i, 0), pipeline_mode=pl.Buffered(1)),
            pl.BlockSpec((1, d, tf), lambda i, f, be, nu, nr: (be[i], 0, f_eff(i, f, nu))),
            pl.BlockSpec((1, d, tf), lambda i, f, be, nu, nr: (be[i], 0, f_eff(i, f, nu))),
            pl.BlockSpec((1, tf, d), lambda i, f, be, nu, nr: (be[i], f_eff(i, f, nu), 0)),
        ],
        out_specs=pl.BlockSpec((tm, dw), lambda i, f, be, nu, nr: (i, 0)),
        scratch_shapes=[pltpu.VMEM((tm, d), BF16), pltpu.VMEM((tm, d), F32)],
    )
    return pl.pallas_call(
        functools.partial(_ffn_kernel, row_chunk=256),
        out_shape=jax.ShapeDtypeStruct((r, dw), U32),
        grid_spec=grid_spec,
        compiler_params=_cparams(("parallel", "arbitrary")),
        name="ffn",
    )(block_e, nused, nrows, x_packed, wg, wu, wd)


def _ple_tail(h2, p_ref, nw_ref, wg_ref, wp_ref, fw_ref, o_ref, final):
    hn = _rms_rows(h2, nw_ref[...]).astype(BF16)
    gate = _sigmoid(jnp.dot(hn, wg_ref[...], preferred_element_type=F32))
    pp = jnp.dot(p_ref[...].astype(BF16), wp_ref[...], preferred_element_type=F32)
    h3 = h2 + gate * pp
    if final:
        h3 = _rms_rows(h3, fw_ref[...])
    o_ref[...] = h3


def _ple_dense_kernel(h_ref, y_ref, p_ref, nw_ref, wg_ref, wp_ref, fw_ref, o_ref, *, final):
    h2 = h_ref[...] + _unpack_bf16_pairs(y_ref[...])
    _ple_tail(h2, p_ref, nw_ref, wg_ref, wp_ref, fw_ref, o_ref, final)


def _ple_moe_kernel(dcur_ref, dnext_ref, h_ref, ys_ref, tw_ref, p_ref, nw_ref, wg_ref, wp_ref, fw_ref, o_ref,
                    gbuf_ref, sem, *, final):
    i = pl.program_id(0)
    n_steps = pl.num_programs(0)
    tm = h_ref.shape[0]
    slot = i % 2

    def issue(d_ref, s):
        def body(r, carry):
            for k in range(TOP_K):
                pltpu.make_async_copy(ys_ref.at[pl.ds(d_ref[0, TOP_K * r + k], 1)],
                                      gbuf_ref.at[s, k, pl.ds(r, 1)], sem.at[s]).start()
            return carry
        lax.fori_loop(0, tm, body, 0, unroll=8)

    @pl.when(i == 0)
    def _():
        issue(dcur_ref, 0)

    @pl.when(i + 1 < n_steps)
    def _():
        issue(dnext_ref, 1 - slot)

    for k in range(TOP_K):
        pltpu.make_async_copy(ys_ref.at[pl.ds(0, tm)], gbuf_ref.at[slot, k], sem.at[slot]).wait()
    tw = tw_ref[...]
    y = tw[:, 0:1] * _unpack_bf16_pairs(gbuf_ref[slot, 0])
    for k in range(1, TOP_K):
        y = y + tw[:, k:k + 1] * _unpack_bf16_pairs(gbuf_ref[slot, k])
    _ple_tail(h_ref[...] + y, p_ref, nw_ref, wg_ref, wp_ref, fw_ref, o_ref, final)


def _ple(h1, y_packed, p, layer, nw, w_gate, w_proj, final_w, *, final, dest=None, top_w=None, tm=512):
    n, d = h1.shape
    pd = p.shape[2]
    dw = d // 2
    n_steps = n // tm
    row = lambda i: (i, 0)
    fixed = lambda i: (0, 0)
    tail_specs = [pl.BlockSpec((None, tm, pd), lambda i: (layer, i, 0)),
                  pl.BlockSpec((1, d), fixed),
                  pl.BlockSpec((None, d, d), lambda i: (layer, 0, 0), pipeline_mode=pl.Buffered(1)),
                  pl.BlockSpec((None, pd, d), lambda i: (layer, 0, 0), pipeline_mode=pl.Buffered(1)),
                  pl.BlockSpec((1, d), fixed)]
    tail_args = [p, nw.reshape(1, d), w_gate, w_proj, final_w.reshape(1, d)]
    scratch = []
    if dest is None:
        kern = functools.partial(_ple_dense_kernel, final=final)
        in_specs = [pl.BlockSpec((tm, d), row), pl.BlockSpec((tm, dw), row)] + tail_specs
        args = [h1, y_packed] + tail_args
    else:
        kern = functools.partial(_ple_moe_kernel, final=final)
        dest3 = dest.reshape(n_steps, 1, TOP_K * tm)
        smem = lambda imap: pl.BlockSpec((None, 1, TOP_K * tm), imap, memory_space=pltpu.SMEM)
        in_specs = [smem(lambda i: (i, 0, 0)), smem(lambda i: (jnp.minimum(i + 1, n_steps - 1), 0, 0)),
                    pl.BlockSpec((tm, d), row), pl.BlockSpec(memory_space=pl.ANY),
                    pl.BlockSpec((tm, TOP_K), row)] + tail_specs
        args = [dest3, dest3, h1, y_packed, top_w] + tail_args
        scratch = scratch + [pltpu.VMEM((2, TOP_K, tm, dw), U32), pltpu.SemaphoreType.DMA((2,))]
    return pl.pallas_call(
        kern,
        out_shape=jax.ShapeDtypeStruct((n, d), F32),
        grid=(n_steps,),
        in_specs=in_specs,
        out_specs=pl.BlockSpec((tm, d), row),
        scratch_shapes=scratch,
        compiler_params=_cparams(("arbitrary",)),
        name="ple",
    )(*args)


def _route(logits, block):
    n = logits.shape[0]
    top_logit, top_e = lax.top_k(logits, TOP_K)
    top_w = jax.nn.softmax(top_logit, axis=-1)
    flat_e = top_e.reshape(-1)
    onehot = (flat_e[:, None] == jnp.arange(N_EXPERTS, dtype=flat_e.dtype)[None, :]).astype(jnp.int32)
    csum = jnp.cumsum(onehot, axis=0)
    counts = csum[-1]
    rank = jnp.sum((csum - onehot) * onehot, axis=1)
    padded = (counts + block - 1) // block * block
    padded_end = jnp.cumsum(padded)
    padded_start = padded_end - padded
    dest = (padded_start[flat_e] + rank).astype(jnp.int32)
    n_blocks = -(-(n * TOP_K) // block) + N_EXPERTS
    block_e = jnp.minimum(jnp.searchsorted(padded_end, jnp.arange(n_blocks, dtype=jnp.int32) * block, side='right'),
                          N_EXPERTS - 1).astype(jnp.int32)
    nused = (padded_end[-1] // block).astype(jnp.int32).reshape(1)
    block_e = jnp.where(jnp.arange(n_blocks) < nused[0], block_e, block_e[jnp.maximum(nused[0] - 1, 0)])
    blk = jnp.arange(n_blocks, dtype=jnp.int32)
    nrows = jnp.clip(counts[block_e] - (blk * block - padded_start[block_e]), 0, block).astype(jnp.int32)
    tail = nused[0] + jnp.arange(N_EXPERTS, dtype=jnp.int32)
    zero_start = jnp.concatenate([jnp.where(padded > 0, padded_end // block - 1, -1),
                                  jnp.where(tail < n_blocks, tail, -1)]).astype(jnp.int32)
    return dest.reshape(n, TOP_K), top_w, block_e, nused, nrows, zero_start, n_blocks * block


def kernel(x, p, norm_mix_w, w_in, hgrn_lb_logits, hgrn_norm_w, gla_gate_w2, gla_gate_b, gla_norm_w, w_out,
           norm_ffn_w, dense_w_gate, dense_w_up, dense_w_down, moe_router, moe_w_gate, moe_w_up, moe_w_down,
           norm_ple_w, ple_w_gate, ple_w_proj, final_norm_w):
    batch, seq, d = x.shape
    depth = w_in.shape[0]
    n = batch * seq
    moe_block = 1024

    lbs = jnp.cumsum(jax.nn.softmax(hgrn_lb_logits.astype(F32), axis=0), axis=0)
    lbs = lbs - lbs[0]
    tri_np, masks_np = _level_tables()
    tri = jnp.asarray(tri_np, BF16)
    masks = jnp.asarray(masks_np, BF16)

    h = x.reshape(n, d)
    p3 = p.reshape(depth, n, -1)
    moe_wg = moe_w_gate.reshape((-1,) + moe_w_gate.shape[2:])
    moe_wu = moe_w_up.reshape((-1,) + moe_w_up.shape[2:])
    moe_wd = moe_w_down.reshape((-1,) + moe_w_down.shape[2:])
    gg0 = _COL_GG + GLA_GATE_RANK
    w_main = w_in[:, :, :_N_MAIN].astype(BF16)
    w_tail = jnp.concatenate(
        [w_in[:, :, gg0:], w_in[:, :, _COL_GG:gg0],
         jnp.zeros((depth, d, _PROJ_COLS - _COL_GLR - GLA_GATE_RANK), F32)], axis=2).astype(BF16)
    w_out_bf = w_out.astype(BF16)
    ple_wg_bf = ple_w_gate.astype(BF16)
    ple_wp_bf = ple_w_proj.astype(BF16)
    for i in range(depth):
        proj = _in_proj(h, norm_mix_w[i], w_main, w_tail, i)
        o_h = _hgrn(proj, lbs[i], hgrn_norm_w[i], tri, masks, batch=batch)
        w2p = jnp.zeros((LANES, gla_gate_w2.shape[2]), F32).at[:GLA_GATE_RANK].set(gla_gate_w2[i]).astype(BF16)
        o_g = _gla(proj, w2p, gla_gate_b[i], gla_norm_w[i], tri, masks, batch=batch)
        j = i // 2
        last = i == depth - 1
        if i % 2 == 0:
            h1, hn = _out_proj(o_h, o_g, h, w_out_bf, i, norm_ffn_w[i])
            n_blk = n // moe_block
            y = _ffn(hn, dense_w_gate, dense_w_up, dense_w_down,
                     jnp.full((n_blk,), j, jnp.int32), jnp.full((1,), n_blk, jnp.int32),
                     jnp.full((n_blk,), moe_block, jnp.int32), tm=moe_block)
            h = _ple(h1, y, p3, i, norm_ple_w[i], ple_wg_bf, ple_wp_bf, final_norm_w, final=last)
        else:
            router_pad = jnp.zeros((d, LANES), F32).at[:, :N_EXPERTS].set(moe_router[j])
            h1, hn, logits = _out_proj(o_h, o_g, h, w_out_bf, i, norm_ffn_w[i], router_pad)
            dest, top_w, block_e, nused, nrows, zero_start, cap = _route(logits[:, :N_EXPERTS], moe_block)
            xs = _dispatch(hn, dest, zero_start, cap, block=moe_block)
            ys = _ffn(xs, moe_wg, moe_wu, moe_wd, block_e + j * N_EXPERTS, nused, nrows, tm=moe_block)
            h = _ple(h1, ys, p3, i, norm_ple_w[i], ple_wg_bf, ple_wp_bf, final_norm_w, final=last,
                     dest=dest, top_w=top_w)
    return h.reshape(batch, seq, d)
```

```python
import functools

import numpy as np
import jax
import jax.numpy as jnp
from jax import lax
from jax.experimental import pallas as pl
from jax.experimental.pallas import tpu as pltpu

F32 = jnp.float32
BF16 = jnp.bfloat16
U32 = jnp.uint32

EPS = 1e-6
LOG2_E = 1.4426950408889634
HGRN_HEADS = 8
HGRN_HEAD_DIM = 128
GLA_HEADS = 4
GLA_HEAD_K = 128
GLA_HEAD_V = 256
GLA_GATE_RANK = 16
GLA_GATE_TEMP = 16.0
N_EXPERTS = 8
TOP_K = 2

LANES = 128
CHUNK = 128
N_LEVELS = 7
VMEM_LIMIT = 56 * 1024 * 1024

_N_MAIN = 6144
_COL_GG = 6144
_COL_GLR = 7168


def _cparams(sem):
    return pltpu.CompilerParams(dimension_semantics=sem, vmem_limit_bytes=VMEM_LIMIT)


def _sigmoid(x):
    return 1.0 / (1.0 + jnp.exp(-x))


def _rms_rows(x, w):
    ms = jnp.mean(x * x, axis=-1, keepdims=True)
    return x * lax.rsqrt(ms + EPS) * w


def _in_proj_kernel(x_ref, nw_ref, w_ref, o_ref, xn_ref, *, row_chunk):
    tm = x_ref.shape[0]

    @pl.when(pl.program_id(1) == 0)
    def _():
        def body(c, carry):
            r = pl.multiple_of(c * row_chunk, row_chunk)
            xn_ref[pl.ds(r, row_chunk), :] = _rms_rows(x_ref[pl.ds(r, row_chunk), :], nw_ref[...]).astype(BF16)
            return carry
        lax.fori_loop(0, tm // row_chunk, body, 0)

    o_ref[...] = lax.dot_general(xn_ref[...], w_ref[0].astype(BF16), _NT,
                                 preferred_element_type=F32).astype(o_ref.dtype)


def _in_proj(x, nw, w_t, layer, *, tm=1024, tn=512):
    n, d = x.shape
    tm = min(tm, n)
    n_main = _N_MAIN // tn
    n_gg = (_COL_GLR - _COL_GG) // tn
    gg0 = _COL_GG + GLA_GATE_RANK

    def w_row(j):
        q, t = GLA_GATE_RANK, tn // GLA_GATE_RANK
        return q * jnp.where(j < n_main, j * t,
                             jnp.where(j < n_main + n_gg, gg0 // q + (j - n_main) * t, _COL_GG // q))

    return pl.pallas_call(
        functools.partial(_in_proj_kernel, row_chunk=128),
        out_shape=jax.ShapeDtypeStruct((n, (n_main + n_gg + 1) * tn), BF16),
        grid=(n // tm, n_main + n_gg + 1),
        in_specs=[
            pl.BlockSpec((tm, d), lambda i, j: (i, 0)),
            pl.BlockSpec((1, d), lambda i, j: (0, 0)),
            pl.BlockSpec((pl.Element(1), pl.Element(tn), pl.Element(d)), lambda i, j: (layer, w_row(j), 0)),
        ],
        out_specs=pl.BlockSpec((tm, tn), lambda i, j: (i, j)),
        scratch_shapes=[pltpu.VMEM((tm, d), BF16)],
        compiler_params=_cparams(("parallel", "arbitrary")),
        name="in_proj",
    )(x, nw.reshape(1, d), w_t)


def _level_tables():
    c = CHUNK
    idx = np.arange(c)
    masks = []
    s = c // 2
    while s >= 1:
        blk = idx // (2 * s)
        upper = (idx % (2 * s)) >= s
        masks.append(((blk[:, None] == blk[None, :]) & upper[:, None] & (~upper)[None, :]).astype(np.float32))
        s //= 2
    masks.append(np.eye(c, dtype=np.float32))
    tri = np.tril(np.ones((c, c), np.float32))
    return np.concatenate([tri, tri], axis=1), np.stack(masks)


def _neg_abs(d):
    return lax.bitcast_convert_type(lax.bitcast_convert_type(d, U32) | U32(0x80000000), F32)


def _level_exponents(g_cum):
    c, dk = g_cum.shape
    row = lax.broadcasted_iota(jnp.int32, (c, dk), 0)
    out = []
    s = c // 2
    while s >= 4:
        nb = c // (2 * s)
        mid = g_cum.reshape(nb, 2 * s, dk)[:, s - 1:s, :]
        mid = jnp.broadcast_to(mid, (nb, 2 * s, dk)).reshape(c, dk)
        out.append(_neg_abs(g_cum - mid))
        s //= 2
    up1 = pltpu.roll(g_cum, 1, 0)
    up2 = pltpu.roll(g_cum, 2, 0)
    dn1 = pltpu.roll(g_cum, c - 1, 0)
    r4 = row % 4
    mid2 = jnp.where(r4 == 0, dn1, jnp.where(r4 == 1, g_cum, jnp.where(r4 == 2, up1, up2)))
    out.append(_neg_abs(g_cum - mid2))
    mid1 = jnp.where(row % 2 == 1, up1, g_cum)
    out.append(_neg_abs(g_cum - mid1))
    return out


_NT = (((1,), (1,)), ((), ()))
_TN = (((0,), (0,)), ((), ()))


def _chunk_prep(g, tri_ref):
    g_hi = g.astype(BF16)
    g_lo = (g - g_hi.astype(F32)).astype(BF16)
    return jnp.dot(tri_ref[...], jnp.concatenate([g_hi, g_lo], axis=0), preferred_element_type=F32) * LOG2_E


def _chunk_local(q, k, v_bf, g_cum, masks_ref):
    c = CHUNK
    q_bf = q.astype(BF16)
    k_bf = k.astype(BF16)
    a = masks_ref[N_LEVELS] * lax.dot_general(q_bf, k_bf, _NT, preferred_element_type=F32).astype(BF16)
    for l, ex in enumerate(_level_exponents(g_cum)):
        e = jnp.exp2(ex).astype(BF16)
        s = lax.dot_general(q_bf * e, k_bf * e, _NT, preferred_element_type=F32)
        a = a + masks_ref[l] * s.astype(BF16)
    g_last = g_cum[c - 1:c, :]
    o_intra = jnp.dot(a, v_bf, preferred_element_type=F32)
    q_dec = (q * jnp.exp2(g_cum)).astype(BF16)
    k_dec = (k * jnp.exp2(g_last - g_cum)).astype(BF16)
    kv = lax.dot_general(v_bf, k_dec, _TN, preferred_element_type=F32)
    return o_intra, q_dec, kv, jnp.exp2(g_last)


def _head_norm(o, w):
    ms = jnp.mean(o * o, axis=-1, keepdims=True)
    return o * lax.rsqrt(ms + EPS) * w


def _scan_block(n_chunks, gates_fn, gate_fn, nw_ref, tri_ref, masks_ref, o_ref, st_ref):
    gates = [gates_fn(ci) for ci in range(n_chunks)]
    cums = [_chunk_prep(g, tri_ref) for (_, _, _, g) in gates]
    parts = [_chunk_local(q, k, v_bf, g_cum, masks_ref) for (q, k, v_bf, _), g_cum in zip(gates, cums)]
    st = st_ref[...]
    for ci, (o_intra, q_dec, kv, decay) in enumerate(parts):
        o = o_intra + lax.dot_general(q_dec, st.astype(BF16), _NT, preferred_element_type=F32)
        rows = pl.ds(ci * CHUNK, CHUNK)
        o_ref[rows, :] = (_head_norm(o, nw_ref[...]) * gate_fn(rows)).astype(o_ref.dtype)
        st = st * decay + kv
    st_ref[...] = st


def _hgrn_kernel(q_ref, f_ref, i_ref, g_ref, lb_ref, nw_ref, tri_ref, masks_ref, o_ref, st_ref):
    @pl.when(pl.program_id(2) == 0)
    def _():
        st_ref[...] = jnp.zeros_like(st_ref)

    lb = lb_ref[...]

    def local(ci):
        rows = pl.ds(ci * CHUNK, CHUNK)
        fgate = lb + (1.0 - lb) * _sigmoid(f_ref[rows, :].astype(F32))
        g = jnp.log(jnp.maximum(fgate, 1e-38))
        k = 1.0 - fgate
        hq = q_ref[rows, :].astype(F32)
        return hq * _sigmoid(hq), k, i_ref[rows, :], g

    def gate(rows):
        return _sigmoid(g_ref[rows, :].astype(F32))

    _scan_block(q_ref.shape[0] // CHUNK, local, gate, nw_ref, tri_ref, masks_ref, o_ref, st_ref)


def _gla_kernel(q_ref, k_ref, v_ref, g_ref, lr_ref, w2_ref, b_ref, nw_ref, tri_ref, masks_ref, o_ref, st_ref):
    @pl.when(pl.program_id(2) == 0)
    def _():
        st_ref[...] = jnp.zeros_like(st_ref)

    def local(ci):
        rows = pl.ds(ci * CHUNK, CHUNK)
        u = jnp.dot(lr_ref[rows, :], w2_ref[...], preferred_element_type=F32) + b_ref[...]
        g = (jnp.minimum(u, 0.0) - jnp.log(1.0 + jnp.exp(-jnp.abs(u)))) * (1.0 / GLA_GATE_TEMP)
        q = q_ref[rows, :].astype(F32) * (GLA_HEAD_K ** -0.5)
        return q, k_ref[rows, :].astype(F32), v_ref[rows, :], g

    def gate(rows):
        gg = g_ref[rows, :].astype(F32)
        return gg * _sigmoid(gg)

    _scan_block(q_ref.shape[0] // CHUNK, local, gate, nw_ref, tri_ref, masks_ref, o_ref, st_ref)


def _const_spec(shape):
    nd = len(shape)
    return pl.BlockSpec(shape, lambda b, h, t: (0,) * nd)


def _hgrn(proj, lb, nw, tri, masks, *, batch, tb=1024):
    n = proj.shape[0]
    nt = n // batch // tb
    dk = HGRN_HEAD_DIM
    hh = HGRN_HEADS

    def col(off):
        return pl.BlockSpec((tb, dk), lambda b, h, t: (b * nt + t, off + h))

    return pl.pallas_call(
        _hgrn_kernel,
        out_shape=jax.ShapeDtypeStruct((n, hh * dk), BF16),
        grid=(batch, hh, nt),
        in_specs=[col(0), col(hh), col(2 * hh), col(3 * hh),
                  pl.BlockSpec((1, dk), lambda b, h, t: (0, h)),
                  pl.BlockSpec((1, dk), lambda b, h, t: (0, h)),
                  _const_spec(tri.shape), _const_spec(masks.shape)],
        out_specs=pl.BlockSpec((tb, dk), lambda b, h, t: (b * nt + t, h)),
        scratch_shapes=[pltpu.VMEM((dk, dk), F32)],
        compiler_params=_cparams(("parallel", "parallel", "arbitrary")),
        name="hgrn",
    )(proj, proj, proj, proj, lb.reshape(1, -1), nw.reshape(1, -1), tri, masks)


def _gla(proj, w2p, b, nw, tri, masks, *, batch, tb=1024):
    n = proj.shape[0]
    nt = n // batch // tb
    dk, dv, hh = GLA_HEAD_K, GLA_HEAD_V, GLA_HEADS
    q0 = 4096 // dk
    k0 = q0 + hh
    v0 = 5120 // dv
    g0 = _COL_GG // dv
    lr0 = _COL_GLR // LANES
    return pl.pallas_call(
        _gla_kernel,
        out_shape=jax.ShapeDtypeStruct((n, hh * dv), BF16),
        grid=(batch, hh, nt),
        in_specs=[pl.BlockSpec((tb, dk), lambda b, h, t: (b * nt + t, q0 + h)),
                  pl.BlockSpec((tb, dk), lambda b, h, t: (b * nt + t, k0 + h)),
                  pl.BlockSpec((tb, dv), lambda b, h, t: (b * nt + t, v0 + h)),
                  pl.BlockSpec((tb, dv), lambda b, h, t: (b * nt + t, g0 + h)),
                  pl.BlockSpec((tb, LANES), lambda b, h, t: (b * nt + t, lr0)),
                  pl.BlockSpec((LANES, dk), lambda b, h, t: (0, h)),
                  pl.BlockSpec((1, dk), lambda b, h, t: (0, h)),
                  pl.BlockSpec((1, dv), lambda b, h, t: (0, h)),
                  _const_spec(tri.shape), _const_spec(masks.shape)],
        out_specs=pl.BlockSpec((tb, dv), lambda b, h, t: (b * nt + t, h)),
        scratch_shapes=[pltpu.VMEM((dv, dk), F32)],
        compiler_params=_cparams(("parallel", "parallel", "arbitrary")),
        name="gla",
    )(proj, proj, proj, proj, proj, w2p, b.reshape(1, -1), nw.reshape(1, -1), tri, masks)


def _pack_bf16_pairs(a):
    c = a.shape[1] // 2
    bits = lax.bitcast_convert_type(a.astype(BF16).astype(F32), U32)
    return (bits[:, :c] >> 16) | (bits[:, c:] & U32(0xFFFF0000))


def _unpack_bf16_pairs(w):
    lo = lax.bitcast_convert_type(w << 16, F32)
    hi = lax.bitcast_convert_type(w & U32(0xFFFF0000), F32)
    return jnp.concatenate([lo, hi], axis=1)


def _out_proj_kernel(oh_ref, og_ref, res_ref, w_ref, nw_ref, *rest, with_router):
    if with_router:
        rw_ref, h_ref, hn_ref, lg_ref = rest
    else:
        h_ref, hn_ref = rest

    kh = oh_ref.shape[1]
    acc = jnp.dot(oh_ref[...], w_ref[0:kh, :], preferred_element_type=F32)
    acc = acc + jnp.dot(og_ref[...], w_ref[kh:, :], preferred_element_type=F32)
    h = res_ref[...] + acc
    h_ref[...] = h
    hn = _rms_rows(h, nw_ref[...])
    hn_ref[...] = _pack_bf16_pairs(hn)
    if with_router:
        hn_hi = hn.astype(BF16)
        hn_lo = (hn - hn_hi.astype(F32)).astype(BF16)
        rw2 = rw_ref[...]
        l2 = jnp.dot(hn_hi, rw2, preferred_element_type=F32)
        lg_ref[...] = (l2[:, :LANES] + l2[:, LANES:]
                       + jnp.dot(hn_lo, rw2[:, :LANES], preferred_element_type=F32))


def _out_proj(o_h, o_g, res, w_out, layer, nw, router_pad=None):
    n, d = res.shape
    kh, kg = o_h.shape[1], o_g.shape[1]
    with_router = router_pad is not None
    tm = 256 if with_router else 512
    row = lambda i: (i, 0)
    fixed = lambda i: (0, 0)
    in_specs = [pl.BlockSpec((tm, kh), row), pl.BlockSpec((tm, kg), row), pl.BlockSpec((tm, d), row),
                pl.BlockSpec((None, kh + kg, d), lambda i: (layer, 0, 0), pipeline_mode=pl.Buffered(1)),
                pl.BlockSpec((1, d), fixed)]
    args = [o_h, o_g, res, w_out, nw.reshape(1, d)]
    out_shape = [jax.ShapeDtypeStruct((n, d), F32), jax.ShapeDtypeStruct((n, d // 2), U32)]
    out_specs = [pl.BlockSpec((tm, d), row), pl.BlockSpec((tm, d // 2), row)]
    if with_router:
        in_specs.append(pl.BlockSpec((d, 2 * LANES), fixed))
        args.append(router_pad)
        out_shape.append(jax.ShapeDtypeStruct((n, LANES), F32))
        out_specs.append(pl.BlockSpec((tm, LANES), row))
    return pl.pallas_call(
        functools.partial(_out_proj_kernel, with_router=with_router),
        out_shape=out_shape,
        grid=(n // tm,),
        in_specs=in_specs,
        out_specs=out_specs,
        compiler_params=_cparams(("parallel",)),
        name="out_proj",
    )(*args)


def _dispatch_kernel(zs_ref, dest_ref, hn_ref, xs_ref, zero_ref, sem, *, tb, block):
    i = pl.program_id(0)

    def zero_copy(e):
        start = pl.multiple_of(zs_ref[e] * block, block)
        return pltpu.make_async_copy(zero_ref, xs_ref.at[pl.ds(start, block)], sem.at[1])

    @pl.when(i == 0)
    def _():
        zero_ref[...] = jnp.zeros_like(zero_ref)
        for e in range(zs_ref.shape[0]):
            @pl.when(zs_ref[e] >= 0)
            def _():
                zero_copy(e).start()
        for e in range(zs_ref.shape[0]):
            @pl.when(zs_ref[e] >= 0)
            def _():
                zero_copy(e).wait()

    def body(r, carry):
        src = hn_ref.at[pl.ds(r, 1)]
        for k in range(TOP_K):
            pltpu.make_async_copy(src, xs_ref.at[pl.ds(dest_ref[0, TOP_K * r + k], 1)], sem.at[0]).start()
        return carry

    lax.fori_loop(0, tb, body, 0, unroll=8)
    for k in range(TOP_K):
        pltpu.make_async_copy(hn_ref, xs_ref.at[pl.ds(0, tb)], sem.at[0]).wait()


def _dispatch(hn_packed, dest, zero_start, cap, *, block, tb=512):
    n, dw = hn_packed.shape
    tb = min(tb, n)
    grid_spec = pltpu.PrefetchScalarGridSpec(
        num_scalar_prefetch=1,
        grid=(n // tb,),
        in_specs=[pl.BlockSpec((None, 1, TOP_K * tb), lambda i, zs: (i, 0, 0), memory_space=pltpu.SMEM),
                  pl.BlockSpec((tb, dw), lambda i, zs: (i, 0))],
        out_specs=pl.BlockSpec(memory_space=pl.ANY),
        scratch_shapes=[pltpu.VMEM((block, dw), U32), pltpu.SemaphoreType.DMA((2,))],
    )
    return pl.pallas_call(
        functools.partial(_dispatch_kernel, tb=tb, block=block),
        out_shape=jax.ShapeDtypeStruct((cap, dw), U32),
        grid_spec=grid_spec,
        compiler_params=_cparams(("arbitrary",)),
        name="dispatch",
    )(zero_start, dest.reshape(n // tb, 1, TOP_K * tb), hn_packed)


def _ffn_kernel(be_ref, nused_ref, nrows_ref, x_ref, wg_ref, wu_ref, wd_ref, o_ref, xb_ref, acc_ref, *, row_chunk):
    i = pl.program_id(0)
    f = pl.program_id(1)
    used = i < nused_ref[0]
    n_chunks = x_ref.shape[0] // row_chunk

    @pl.when(f == 0)
    def _():
        acc_ref[...] = jnp.zeros_like(acc_ref)

    @pl.when(jnp.logical_and(f == 0, used))
    def _():
        def body(c, carry):
            rows = pl.ds(pl.multiple_of(c * row_chunk, row_chunk), row_chunk)
            xb_ref[rows, :] = _unpack_bf16_pairs(x_ref[rows, :]).astype(BF16)
            return carry
        lax.fori_loop(0, n_chunks, body, 0)

    live = (nrows_ref[i] + row_chunk - 1) // row_chunk
    for nc in range(1, n_chunks + 1):
        @pl.when(jnp.logical_and(used, live == nc))
        def _():
            rows = pl.ds(0, nc * row_chunk)
            x = xb_ref[rows, :]
            g = jnp.dot(x, wg_ref[0].astype(BF16), preferred_element_type=F32)
            u = jnp.dot(x, wu_ref[0].astype(BF16), preferred_element_type=F32)
            a = (g * _sigmoid(g) * u).astype(BF16)
            acc_ref[rows, :] += jnp.dot(a, wd_ref[0].astype(BF16), preferred_element_type=F32)

    @pl.when(f == pl.num_programs(1) - 1)
    def _():
        def body(c, carry):
            rows = pl.ds(pl.multiple_of(c * row_chunk, row_chunk), row_chunk)
            o_ref[rows, :] = _pack_bf16_pairs(acc_ref[rows, :])
            return carry
        lax.fori_loop(0, n_chunks, body, 0)


def _ffn(x_packed, wg, wu, wd, block_e, nused, nrows, *, tm=1024, tf=512):
    r, dw = x_packed.shape
    d = 2 * dw
    ff = wg.shape[2]
    nf = ff // tf

    def f_eff(i, f, nu):
        return jnp.where(i < nu[0], f, nf - 1)

    grid_spec = pltpu.PrefetchScalarGridSpec(
        num_scalar_prefetch=3,
        grid=(r // tm, nf),
        in_specs=[
            pl.BlockSpec((tm, dw), lambda i, f, be, nu, nr: (i, 0), pipeline_mode=pl.Buffered(1)),
            pl.BlockSpec((1, d, tf), lambda i, f, be, nu, nr: (be[i], 0, f_eff(i, f, nu))),
            pl.BlockSpec((1, d, tf), lambda i, f, be, nu, nr: (be[i], 0, f_eff(i, f, nu))),
            pl.BlockSpec((1, tf, d), lambda i, f, be, nu, nr: (be[i], f_eff(i, f, nu), 0)),
        ],
        out_specs=pl.BlockSpec((tm, dw), lambda i, f, be, nu, nr: (i, 0)),
        scratch_shapes=[pltpu.VMEM((tm, d), BF16), pltpu.VMEM((tm, d), F32)],
    )
    return pl.pallas_call(
        functools.partial(_ffn_kernel, row_chunk=256),
        out_shape=jax.ShapeDtypeStruct((r, dw), U32),
        grid_spec=grid_spec,
        compiler_params=_cparams(("parallel", "arbitrary")),
        name="ffn",
    )(block_e, nused, nrows, x_packed, wg, wu, wd)


def _ple_tail(h2, p_ref, nw_ref, wg_ref, wp_ref, fw_ref, o_ref, final):
    hn = _rms_rows(h2, nw_ref[...]).astype(BF16)
    gate = _sigmoid(jnp.dot(hn, wg_ref[...], preferred_element_type=F32))
    pp = jnp.dot(p_ref[...].astype(BF16), wp_ref[...], preferred_element_type=F32)
    h3 = h2 + gate * pp
    if final:
        h3 = _rms_rows(h3, fw_ref[...])
    o_ref[...] = h3


def _ple_dense_kernel(h_ref, y_ref, p_ref, nw_ref, wg_ref, wp_ref, fw_ref, o_ref, *, final):
    h2 = h_ref[...] + _unpack_bf16_pairs(y_ref[...])
    _ple_tail(h2, p_ref, nw_ref, wg_ref, wp_ref, fw_ref, o_ref, final)


def _ple_moe_kernel(dcur_ref, dnext_ref, h_ref, ys_ref, tw_ref, p_ref, nw_ref, wg_ref, wp_ref, fw_ref, o_ref,
                    gbuf_ref, sem, *, final):
    i = pl.program_id(0)
    n_steps = pl.num_programs(0)
    tm = h_ref.shape[0]
    slot = i % 2

    def issue(d_ref, s):
        def body(r, carry):
            for k in range(TOP_K):
                pltpu.make_async_copy(ys_ref.at[pl.ds(d_ref[0, TOP_K * r + k], 1)],
                                      gbuf_ref.at[s, k, pl.ds(r, 1)], sem.at[s]).start()
            return carry
        lax.fori_loop(0, tm, body, 0, unroll=8)

    @pl.when(i == 0)
    def _():
        issue(dcur_ref, 0)

    @pl.when(i + 1 < n_steps)
    def _():
        issue(dnext_ref, 1 - slot)

    for k in range(TOP_K):
        pltpu.make_async_copy(ys_ref.at[pl.ds(0, tm)], gbuf_ref.at[slot, k], sem.at[slot]).wait()
    tw = tw_ref[...]
    y = tw[:, 0:1] * _unpack_bf16_pairs(gbuf_ref[slot, 0])
    for k in range(1, TOP_K):
        y = y + tw[:, k:k + 1] * _unpack_bf16_pairs(gbuf_ref[slot, k])
    _ple_tail(h_ref[...] + y, p_ref, nw_ref, wg_ref, wp_ref, fw_ref, o_ref, final)


def _ple(h1, y_packed, p, layer, nw, w_gate, w_proj, final_w, *, final, dest=None, top_w=None, tm=512):
    n, d = h1.shape
    pd = p.shape[2]
    dw = d // 2
    n_steps = n // tm
    row = lambda i: (i, 0)
    fixed = lambda i: (0, 0)
    tail_specs = [pl.BlockSpec((None, tm, pd), lambda i: (layer, i, 0)),
                  pl.BlockSpec((1, d), fixed),
                  pl.BlockSpec((None, d, d), lambda i: (layer, 0, 0), pipeline_mode=pl.Buffered(1)),
                  pl.BlockSpec((None, pd, d), lambda i: (layer, 0, 0), pipeline_mode=pl.Buffered(1)),
                  pl.BlockSpec((1, d), fixed)]
    tail_args = [p, nw.reshape(1, d), w_gate, w_proj, final_w.reshape(1, d)]
    scratch = []
    if dest is None:
        kern = functools.partial(_ple_dense_kernel, final=final)
        in_specs = [pl.BlockSpec((tm, d), row), pl.BlockSpec((tm, dw), row)] + tail_specs
        args = [h1, y_packed] + tail_args
    else:
        kern = functools.partial(_ple_moe_kernel, final=final)
        dest3 = dest.reshape(n_steps, 1, TOP_K * tm)
        smem = lambda imap: pl.BlockSpec((None, 1, TOP_K * tm), imap, memory_space=pltpu.SMEM)
        in_specs = [smem(lambda i: (i, 0, 0)), smem(lambda i: (jnp.minimum(i + 1, n_steps - 1), 0, 0)),
                    pl.BlockSpec((tm, d), row), pl.BlockSpec(memory_space=pl.ANY),
                    pl.BlockSpec((tm, TOP_K), row)] + tail_specs
        args = [dest3, dest3, h1, y_packed, top_w] + tail_args
        scratch = scratch + [pltpu.VMEM((2, TOP_K, tm, dw), U32), pltpu.SemaphoreType.DMA((2,))]
    return pl.pallas_call(
        kern,
        out_shape=jax.ShapeDtypeStruct((n, d), F32),
        grid=(n_steps,),
        in_specs=in_specs,
        out_specs=pl.BlockSpec((tm, d), row),
        scratch_shapes=scratch,
        compiler_params=_cparams(("arbitrary",)),
        name="ple",
    )(*args)


def _route(logits, block):
    n = logits.shape[0]
    top_logit, top_e = lax.top_k(logits, TOP_K)
    top_w = jax.nn.softmax(top_logit, axis=-1)
    flat_e = top_e.reshape(-1)
    onehot = (flat_e[:, None] == jnp.arange(N_EXPERTS, dtype=flat_e.dtype)[None, :]).astype(jnp.int32)
    csum = jnp.cumsum(onehot, axis=0)
    counts = csum[-1]
    rank = jnp.sum((csum - onehot) * onehot, axis=1)
    padded = (counts + block - 1) // block * block
    padded_end = jnp.cumsum(padded)
    padded_start = padded_end - padded
    dest = (padded_start[flat_e] + rank).astype(jnp.int32)
    n_blocks = -(-(n * TOP_K) // block) + N_EXPERTS
    block_e = jnp.minimum(jnp.searchsorted(padded_end, jnp.arange(n_blocks, dtype=jnp.int32) * block, side='right'),
                          N_EXPERTS - 1).astype(jnp.int32)
    nused = (padded_end[-1] // block).astype(jnp.int32).reshape(1)
    block_e = jnp.where(jnp.arange(n_blocks) < nused[0], block_e, block_e[jnp.maximum(nused[0] - 1, 0)])
    blk = jnp.arange(n_blocks, dtype=jnp.int32)
    nrows = jnp.clip(counts[block_e] - (blk * block - padded_start[block_e]), 0, block).astype(jnp.int32)
    tail = nused[0] + jnp.arange(N_EXPERTS, dtype=jnp.int32)
    zero_start = jnp.concatenate([jnp.where(padded > 0, padded_end // block - 1, -1),
                                  jnp.where(tail < n_blocks, tail, -1)]).astype(jnp.int32)
    return dest.reshape(n, TOP_K), top_w, block_e, nused, nrows, zero_start, n_blocks * block


def kernel(x, p, norm_mix_w, w_in, hgrn_lb_logits, hgrn_norm_w, gla_gate_w2, gla_gate_b, gla_norm_w, w_out,
           norm_ffn_w, dense_w_gate, dense_w_up, dense_w_down, moe_router, moe_w_gate, moe_w_up, moe_w_down,
           norm_ple_w, ple_w_gate, ple_w_proj, final_norm_w):
    batch, seq, d = x.shape
    depth = w_in.shape[0]
    n = batch * seq
    moe_block = 1024

    lbs = jnp.cumsum(jax.nn.softmax(hgrn_lb_logits.astype(F32), axis=0), axis=0)
    lbs = lbs - lbs[0]
    tri_np, masks_np = _level_tables()
    tri = jnp.asarray(tri_np, BF16)
    masks = jnp.asarray(masks_np, BF16)

    h = x.reshape(n, d)
    p3 = p.reshape(depth, n, -1)
    moe_wg = moe_w_gate.reshape((-1,) + moe_w_gate.shape[2:])
    moe_wu = moe_w_up.reshape((-1,) + moe_w_up.shape[2:])
    moe_wd = moe_w_down.reshape((-1,) + moe_w_down.shape[2:])
    w_in_t = jnp.swapaxes(w_in, 1, 2)
    w_out_bf = w_out.astype(BF16)
    ple_wg_bf = ple_w_gate.astype(BF16)
    ple_wp_bf = ple_w_proj.astype(BF16)
    for i in range(depth):
        proj = _in_proj(h, norm_mix_w[i], w_in_t, i)
        o_h = _hgrn(proj, lbs[i], hgrn_norm_w[i], tri, masks, batch=batch)
        w2p = jnp.zeros((LANES, gla_gate_w2.shape[2]), F32).at[:GLA_GATE_RANK].set(gla_gate_w2[i]).astype(BF16)
        o_g = _gla(proj, w2p, gla_gate_b[i], gla_norm_w[i], tri, masks, batch=batch)
        j = i // 2
        last = i == depth - 1
        if i % 2 == 0:
            h1, hn = _out_proj(o_h, o_g, h, w_out_bf, i, norm_ffn_w[i])
            n_blk = n // moe_block
            y = _ffn(hn, dense_w_gate, dense_w_up, dense_w_down,
                     jnp.full((n_blk,), j, jnp.int32), jnp.full((1,), n_blk, jnp.int32),
                     jnp.full((n_blk,), moe_block, jnp.int32), tm=moe_block)
            h = _ple(h1, y, p3, i, norm_ple_w[i], ple_wg_bf, ple_wp_bf, final_norm_w, final=last)
        else:
            rw = jnp.zeros((d, LANES), F32).at[:, :N_EXPERTS].set(moe_router[j])
            rw_hi = rw.astype(BF16)
            router2 = jnp.concatenate([rw_hi, (rw - rw_hi.astype(F32)).astype(BF16)], axis=1)
            h1, hn, logits = _out_proj(o_h, o_g, h, w_out_bf, i, norm_ffn_w[i], router2)
            dest, top_w, block_e, nused, nrows, zero_start, cap = _route(logits[:, :N_EXPERTS], moe_block)
            xs = _dispatch(hn, dest, zero_start, cap, block=moe_block)
            ys = _ffn(xs, moe_wg, moe_wu, moe_wd, block_e + j * N_EXPERTS, nused, nrows, tm=moe_block)
            h = _ple(h1, ys, p3, i, norm_ple_w[i], ple_wg_bf, ple_wp_bf, final_norm_w, final=last,
                     dest=dest, top_w=top_w)
    return h.reshape(batch, seq, d)
```

```python
import functools

import numpy as np
import jax
import jax.numpy as jnp
from jax import lax
from jax.experimental import pallas as pl
from jax.experimental.pallas import tpu as pltpu

F32 = jnp.float32
BF16 = jnp.bfloat16
U32 = jnp.uint32

EPS = 1e-6
LOG2_E = 1.4426950408889634
HGRN_HEADS = 8
HGRN_HEAD_DIM = 128
GLA_HEADS = 4
GLA_HEAD_K = 128
GLA_HEAD_V = 256
GLA_GATE_RANK = 16
GLA_GATE_TEMP = 16.0
N_EXPERTS = 8
TOP_K = 2

LANES = 128
CHUNK = 128
N_LEVELS = 7
VMEM_LIMIT = 56 * 1024 * 1024

_N_MAIN = 6144
_COL_GG = 6144
_COL_GLR = 7168


def _cparams(sem):
    return pltpu.CompilerParams(dimension_semantics=sem, vmem_limit_bytes=VMEM_LIMIT)


def _sigmoid(x):
    return 1.0 / (1.0 + jnp.exp(-x))


def _rms_rows(x, w):
    ms = jnp.mean(x * x, axis=-1, keepdims=True)
    return x * lax.rsqrt(ms + EPS) * w


def _in_proj_kernel(x_ref, nw_ref, w_ref, wlr_ref, o_ref, lr_ref, xn_ref, *, row_chunk):
    tm = x_ref.shape[0]

    @pl.when(pl.program_id(1) == 0)
    def _():
        def body(c, carry):
            r = pl.multiple_of(c * row_chunk, row_chunk)
            xn_ref[pl.ds(r, row_chunk), :] = _rms_rows(x_ref[pl.ds(r, row_chunk), :], nw_ref[...]).astype(BF16)
            return carry
        lax.fori_loop(0, tm // row_chunk, body, 0)
        lr_ref[...] = lax.dot_general(xn_ref[...], wlr_ref[0].astype(BF16), _NT,
                                      preferred_element_type=F32).astype(lr_ref.dtype)

    o_ref[...] = lax.dot_general(xn_ref[...], w_ref[0].astype(BF16), _NT,
                                 preferred_element_type=F32).astype(o_ref.dtype)


def _in_proj(x, nw, w_t, layer, *, tm=1024, tn=1024):
    n, d = x.shape
    tm = min(tm, n)
    n_main = _N_MAIN // tn
    n_gg = (_COL_GLR - _COL_GG) // tn
    gg0 = _COL_GG + GLA_GATE_RANK

    def w_row(j):
        q, t = GLA_GATE_RANK, tn // GLA_GATE_RANK
        return q * jnp.where(j < n_main, j * t, gg0 // q + (j - n_main) * t)

    slab = lambda rows, imap: pl.BlockSpec((pl.Element(1), pl.Element(rows), pl.Element(d)), imap)
    return pl.pallas_call(
        functools.partial(_in_proj_kernel, row_chunk=128),
        out_shape=[jax.ShapeDtypeStruct((n, (n_main + n_gg) * tn), BF16), jax.ShapeDtypeStruct((n, LANES), BF16)],
        grid=(n // tm, n_main + n_gg),
        in_specs=[
            pl.BlockSpec((tm, d), lambda i, j: (i, 0)),
            pl.BlockSpec((1, d), lambda i, j: (0, 0)),
            slab(tn, lambda i, j: (layer, w_row(j), 0)),
            slab(LANES, lambda i, j: (layer, _COL_GG, 0)),
        ],
        out_specs=[pl.BlockSpec((tm, tn), lambda i, j: (i, j)), pl.BlockSpec((tm, LANES), lambda i, j: (i, 0))],
        scratch_shapes=[pltpu.VMEM((tm, d), BF16)],
        compiler_params=_cparams(("parallel", "arbitrary")),
        name="in_proj",
    )(x, nw.reshape(1, d), w_t, w_t)


def _level_tables():
    c = CHUNK
    idx = np.arange(c)
    masks = []
    s = c // 2
    while s >= 1:
        blk = idx // (2 * s)
        upper = (idx % (2 * s)) >= s
        masks.append(((blk[:, None] == blk[None, :]) & upper[:, None] & (~upper)[None, :]).astype(np.float32))
        s //= 2
    masks.append(np.eye(c, dtype=np.float32))
    tri = np.tril(np.ones((c, c), np.float32))
    return np.concatenate([tri, tri], axis=1), np.stack(masks)


def _neg_abs(d):
    return lax.bitcast_convert_type(lax.bitcast_convert_type(d, U32) | U32(0x80000000), F32)


def _level_exponents(g_cum):
    c, dk = g_cum.shape
    row = lax.broadcasted_iota(jnp.int32, (c, dk), 0)
    out = []
    s = c // 2
    while s >= 4:
        nb = c // (2 * s)
        mid = g_cum.reshape(nb, 2 * s, dk)[:, s - 1:s, :]
        mid = jnp.broadcast_to(mid, (nb, 2 * s, dk)).reshape(c, dk)
        out.append(_neg_abs(g_cum - mid))
        s //= 2
    up1 = pltpu.roll(g_cum, 1, 0)
    up2 = pltpu.roll(g_cum, 2, 0)
    dn1 = pltpu.roll(g_cum, c - 1, 0)
    r4 = row % 4
    mid2 = jnp.where(r4 == 0, dn1, jnp.where(r4 == 1, g_cum, jnp.where(r4 == 2, up1, up2)))
    out.append(_neg_abs(g_cum - mid2))
    mid1 = jnp.where(row % 2 == 1, up1, g_cum)
    out.append(_neg_abs(g_cum - mid1))
    return out


_NT = (((1,), (1,)), ((), ()))
_TN = (((0,), (0,)), ((), ()))


def _chunk_prep(g, tri_ref):
    g_hi = g.astype(BF16)
    g_lo = (g - g_hi.astype(F32)).astype(BF16)
    return jnp.dot(tri_ref[...], jnp.concatenate([g_hi, g_lo], axis=0), preferred_element_type=F32) * LOG2_E


def _chunk_local(q, k, v_bf, g_cum, masks_ref):
    c = CHUNK
    q_bf = q.astype(BF16)
    k_bf = k.astype(BF16)
    a = masks_ref[N_LEVELS] * lax.dot_general(q_bf, k_bf, _NT, preferred_element_type=F32).astype(BF16)
    for l, ex in enumerate(_level_exponents(g_cum)):
        e = jnp.exp2(ex).astype(BF16)
        s = lax.dot_general(q_bf * e, k_bf * e, _NT, preferred_element_type=F32)
        a = a + masks_ref[l] * s.astype(BF16)
    g_last = g_cum[c - 1:c, :]
    o_intra = jnp.dot(a, v_bf, preferred_element_type=F32)
    q_dec = (q * jnp.exp2(g_cum)).astype(BF16)
    k_dec = (k * jnp.exp2(g_last - g_cum)).astype(BF16)
    kv = lax.dot_general(v_bf, k_dec, _TN, preferred_element_type=F32)
    return o_intra, q_dec, kv, jnp.exp2(g_last)


def _head_norm(o, w):
    ms = jnp.mean(o * o, axis=-1, keepdims=True)
    return o * lax.rsqrt(ms + EPS) * w


def _scan_block(n_chunks, gates_fn, gate_fn, nw_ref, tri_ref, masks_ref, o_ref, st_ref):
    gates = [gates_fn(ci) for ci in range(n_chunks)]
    cums = [_chunk_prep(g, tri_ref) for (_, _, _, g) in gates]
    parts = [_chunk_local(q, k, v_bf, g_cum, masks_ref) for (q, k, v_bf, _), g_cum in zip(gates, cums)]
    st = st_ref[...]
    for ci, (o_intra, q_dec, kv, decay) in enumerate(parts):
        o = o_intra + lax.dot_general(q_dec, st.astype(BF16), _NT, preferred_element_type=F32)
        rows = pl.ds(ci * CHUNK, CHUNK)
        o_ref[rows, :] = (_head_norm(o, nw_ref[...]) * gate_fn(rows)).astype(o_ref.dtype)
        st = st * decay + kv
    st_ref[...] = st


def _hgrn_kernel(q_ref, f_ref, i_ref, g_ref, lb_ref, nw_ref, tri_ref, masks_ref, o_ref, st_ref):
    @pl.when(pl.program_id(2) == 0)
    def _():
        st_ref[...] = jnp.zeros_like(st_ref)

    lb = lb_ref[...]

    def local(ci):
        rows = pl.ds(ci * CHUNK, CHUNK)
        fgate = lb + (1.0 - lb) * _sigmoid(f_ref[rows, :].astype(F32))
        g = jnp.log(jnp.maximum(fgate, 1e-38))
        k = 1.0 - fgate
        hq = q_ref[rows, :].astype(F32)
        return hq * _sigmoid(hq), k, i_ref[rows, :], g

    def gate(rows):
        return _sigmoid(g_ref[rows, :].astype(F32))

    _scan_block(q_ref.shape[0] // CHUNK, local, gate, nw_ref, tri_ref, masks_ref, o_ref, st_ref)


def _gla_kernel(q_ref, k_ref, v_ref, g_ref, lr_ref, w2_ref, b_ref, nw_ref, tri_ref, masks_ref, o_ref, st_ref):
    @pl.when(pl.program_id(2) == 0)
    def _():
        st_ref[...] = jnp.zeros_like(st_ref)

    def local(ci):
        rows = pl.ds(ci * CHUNK, CHUNK)
        u = jnp.dot(lr_ref[rows, :], w2_ref[...], preferred_element_type=F32) + b_ref[...]
        g = (jnp.minimum(u, 0.0) - jnp.log(1.0 + jnp.exp(-jnp.abs(u)))) * (1.0 / GLA_GATE_TEMP)
        q = q_ref[rows, :].astype(F32) * (GLA_HEAD_K ** -0.5)
        return q, k_ref[rows, :].astype(F32), v_ref[rows, :], g

    def gate(rows):
        gg = g_ref[rows, :].astype(F32)
        return gg * _sigmoid(gg)

    _scan_block(q_ref.shape[0] // CHUNK, local, gate, nw_ref, tri_ref, masks_ref, o_ref, st_ref)


def _const_spec(shape):
    nd = len(shape)
    return pl.BlockSpec(shape, lambda b, h, t: (0,) * nd)


def _hgrn(proj, lb, nw, tri, masks, *, batch, tb=2048):
    n = proj.shape[0]
    nt = n // batch // tb
    dk = HGRN_HEAD_DIM
    hh = HGRN_HEADS

    def col(off):
        return pl.BlockSpec((tb, dk), lambda b, h, t: (b * nt + t, off + h))

    return pl.pallas_call(
        _hgrn_kernel,
        out_shape=jax.ShapeDtypeStruct((n, hh * dk), BF16),
        grid=(batch, hh, nt),
        in_specs=[col(0), col(hh), col(2 * hh), col(3 * hh),
                  pl.BlockSpec((1, dk), lambda b, h, t: (0, h)),
                  pl.BlockSpec((1, dk), lambda b, h, t: (0, h)),
                  _const_spec(tri.shape), _const_spec(masks.shape)],
        out_specs=pl.BlockSpec((tb, dk), lambda b, h, t: (b * nt + t, h)),
        scratch_shapes=[pltpu.VMEM((dk, dk), F32)],
        compiler_params=_cparams(("parallel", "parallel", "arbitrary")),
        name="hgrn",
    )(proj, proj, proj, proj, lb.reshape(1, -1), nw.reshape(1, -1), tri, masks)


def _gla(proj, lr, w2p, b, nw, tri, masks, *, batch, tb=2048):
    n = proj.shape[0]
    nt = n // batch // tb
    dk, dv, hh = GLA_HEAD_K, GLA_HEAD_V, GLA_HEADS
    q0 = 4096 // dk
    k0 = q0 + hh
    v0 = 5120 // dv
    g0 = _COL_GG // dv
    return pl.pallas_call(
        _gla_kernel,
        out_shape=jax.ShapeDtypeStruct((n, hh * dv), BF16),
        grid=(batch, hh, nt),
        in_specs=[pl.BlockSpec((tb, dk), lambda b, h, t: (b * nt + t, q0 + h)),
                  pl.BlockSpec((tb, dk), lambda b, h, t: (b * nt + t, k0 + h)),
                  pl.BlockSpec((tb, dv), lambda b, h, t: (b * nt + t, v0 + h)),
                  pl.BlockSpec((tb, dv), lambda b, h, t: (b * nt + t, g0 + h)),
                  pl.BlockSpec((tb, LANES), lambda b, h, t: (b * nt + t, 0)),
                  pl.BlockSpec((LANES, dk), lambda b, h, t: (0, h)),
                  pl.BlockSpec((1, dk), lambda b, h, t: (0, h)),
                  pl.BlockSpec((1, dv), lambda b, h, t: (0, h)),
                  _const_spec(tri.shape), _const_spec(masks.shape)],
        out_specs=pl.BlockSpec((tb, dv), lambda b, h, t: (b * nt + t, h)),
        scratch_shapes=[pltpu.VMEM((dv, dk), F32)],
        compiler_params=_cparams(("parallel", "parallel", "arbitrary")),
        name="gla",
    )(proj, proj, proj, proj, lr, w2p, b.reshape(1, -1), nw.reshape(1, -1), tri, masks)


def _pack_bf16_pairs(a):
    c = a.shape[1] // 2
    bits = lax.bitcast_convert_type(a.astype(BF16).astype(F32), U32)
    return (bits[:, :c] >> 16) | (bits[:, c:] & U32(0xFFFF0000))


def _unpack_bf16_pairs(w):
    lo = lax.bitcast_convert_type(w << 16, F32)
    hi = lax.bitcast_convert_type(w & U32(0xFFFF0000), F32)
    return jnp.concatenate([lo, hi], axis=1)


def _out_proj_kernel(oh_ref, og_ref, res_ref, w_ref, nw_ref, *rest, with_router):
    if with_router:
        rw_ref, h_ref, hn_ref, lg_ref = rest
    else:
        h_ref, hn_ref = rest

    kh = oh_ref.shape[1]
    acc = jnp.dot(oh_ref[...], w_ref[0:kh, :], preferred_element_type=F32)
    acc = acc + jnp.dot(og_ref[...], w_ref[kh:, :], preferred_element_type=F32)
    h = res_ref[...] + acc
    h_ref[...] = h
    hn = _rms_rows(h, nw_ref[...])
    hn_ref[...] = _pack_bf16_pairs(hn)
    if with_router:
        hn_hi = hn.astype(BF16)
        hn_lo = (hn - hn_hi.astype(F32)).astype(BF16)
        rw2 = rw_ref[...]
        l2 = jnp.dot(hn_hi, rw2, preferred_element_type=F32)
        lg_ref[...] = (l2[:, :LANES] + l2[:, LANES:]
                       + jnp.dot(hn_lo, rw2[:, :LANES], preferred_element_type=F32))


def _out_proj(o_h, o_g, res, w_out, layer, nw, router_pad=None):
    n, d = res.shape
    kh, kg = o_h.shape[1], o_g.shape[1]
    with_router = router_pad is not None
    tm = 256 if with_router else 512
    row = lambda i: (i, 0)
    fixed = lambda i: (0, 0)
    in_specs = [pl.BlockSpec((tm, kh), row), pl.BlockSpec((tm, kg), row), pl.BlockSpec((tm, d), row),
                pl.BlockSpec((None, kh + kg, d), lambda i: (layer, 0, 0), pipeline_mode=pl.Buffered(1)),
                pl.BlockSpec((1, d), fixed)]
    args = [o_h, o_g, res, w_out, nw.reshape(1, d)]
    out_shape = [jax.ShapeDtypeStruct((n, d), F32), jax.ShapeDtypeStruct((n, d // 2), U32)]
    out_specs = [pl.BlockSpec((tm, d), row), pl.BlockSpec((tm, d // 2), row)]
    if with_router:
        in_specs.append(pl.BlockSpec((d, 2 * LANES), fixed))
        args.append(router_pad)
        out_shape.append(jax.ShapeDtypeStruct((n, LANES), F32))
        out_specs.append(pl.BlockSpec((tm, LANES), row))
    return pl.pallas_call(
        functools.partial(_out_proj_kernel, with_router=with_router),
        out_shape=out_shape,
        grid=(n // tm,),
        in_specs=in_specs,
        out_specs=out_specs,
        compiler_params=_cparams(("parallel",)),
        name="out_proj",
    )(*args)


def _dispatch_kernel(zs_ref, dest_ref, hn_ref, xs_ref, zero_ref, sem, *, tb, block):
    i = pl.program_id(0)

    def zero_copy(e):
        start = pl.multiple_of(zs_ref[e] * block, block)
        return pltpu.make_async_copy(zero_ref, xs_ref.at[pl.ds(start, block)], sem.at[1])

    @pl.when(i == 0)
    def _():
        zero_ref[...] = jnp.zeros_like(zero_ref)
        for e in range(zs_ref.shape[0]):
            @pl.when(zs_ref[e] >= 0)
            def _():
                zero_copy(e).start()
        for e in range(zs_ref.shape[0]):
            @pl.when(zs_ref[e] >= 0)
            def _():
                zero_copy(e).wait()

    def body(r, carry):
        src = hn_ref.at[pl.ds(r, 1)]
        for k in range(TOP_K):
            pltpu.make_async_copy(src, xs_ref.at[pl.ds(dest_ref[0, TOP_K * r + k], 1)], sem.at[0]).start()
        return carry

    lax.fori_loop(0, tb, body, 0, unroll=8)
    for k in range(TOP_K):
        pltpu.make_async_copy(hn_ref, xs_ref.at[pl.ds(0, tb)], sem.at[0]).wait()


def _dispatch(hn_packed, dest, zero_start, cap, *, block, tb=512):
    n, dw = hn_packed.shape
    tb = min(tb, n)
    grid_spec = pltpu.PrefetchScalarGridSpec(
        num_scalar_prefetch=1,
        grid=(n // tb,),
        in_specs=[pl.BlockSpec((None, 1, TOP_K * tb), lambda i, zs: (i, 0, 0), memory_space=pltpu.SMEM),
                  pl.BlockSpec((tb, dw), lambda i, zs: (i, 0))],
        out_specs=pl.BlockSpec(memory_space=pl.ANY),
        scratch_shapes=[pltpu.VMEM((block, dw), U32), pltpu.SemaphoreType.DMA((2,))],
    )
    return pl.pallas_call(
        functools.partial(_dispatch_kernel, tb=tb, block=block),
        out_shape=jax.ShapeDtypeStruct((cap, dw), U32),
        grid_spec=grid_spec,
        compiler_params=_cparams(("arbitrary",)),
        name="dispatch",
    )(zero_start, dest.reshape(n // tb, 1, TOP_K * tb), hn_packed)


def _ffn_kernel(be_ref, nused_ref, nrows_ref, x_ref, wg_ref, wu_ref, wd_ref, o_ref, xb_ref, acc_ref, *, row_chunk):
    i = pl.program_id(0)
    f = pl.program_id(1)
    used = i < nused_ref[0]
    n_chunks = x_ref.shape[0] // row_chunk

    @pl.when(f == 0)
    def _():
        acc_ref[...] = jnp.zeros_like(acc_ref)

    @pl.when(jnp.logical_and(f == 0, used))
    def _():
        def body(c, carry):
            rows = pl.ds(pl.multiple_of(c * row_chunk, row_chunk), row_chunk)
            xb_ref[rows, :] = _unpack_bf16_pairs(x_ref[rows, :]).astype(BF16)
            return carry
        lax.fori_loop(0, n_chunks, body, 0)

    live = (nrows_ref[i] + row_chunk - 1) // row_chunk
    for nc in range(1, n_chunks + 1):
        @pl.when(jnp.logical_and(used, live == nc))
        def _():
            rows = pl.ds(0, nc * row_chunk)
            x = xb_ref[rows, :]
            g = jnp.dot(x, wg_ref[0].astype(BF16), preferred_element_type=F32)
            u = jnp.dot(x, wu_ref[0].astype(BF16), preferred_element_type=F32)
            a = (g * _sigmoid(g) * u).astype(BF16)
            acc_ref[rows, :] += jnp.dot(a, wd_ref[0].astype(BF16), preferred_element_type=F32)

    @pl.when(f == pl.num_programs(1) - 1)
    def _():
        def body(c, carry):
            rows = pl.ds(pl.multiple_of(c * row_chunk, row_chunk), row_chunk)
            o_ref[rows, :] = _pack_bf16_pairs(acc_ref[rows, :])
            return carry
        lax.fori_loop(0, n_chunks, body, 0)


def _ffn(x_packed, wg, wu, wd, block_e, nused, nrows, *, tm=1024, tf=512):
    r, dw = x_packed.shape
    d = 2 * dw
    ff = wg.shape[2]
    nf = ff // tf

    def f_eff(i, f, nu):
        return jnp.where(i < nu[0], f, nf - 1)

    grid_spec = pltpu.PrefetchScalarGridSpec(
        num_scalar_prefetch=3,
        grid=(r // tm, nf),
        in_specs=[
            pl.BlockSpec((tm, dw), lambda i, f, be, nu, nr: (i, 0), pipeline_mode=pl.Buffered(1)),
            pl.BlockSpec((1, d, tf), lambda i, f, be, nu, nr: (be[i], 0, f_eff(i, f, nu))),
            pl.BlockSpec((1, d, tf), lambda i, f, be, nu, nr: (be[i], 0, f_eff(i, f, nu))),
            pl.BlockSpec((1, tf, d), lambda i, f, be, nu, nr: (be[i], f_eff(i, f, nu), 0)),
        ],
        out_specs=pl.BlockSpec((tm, dw), lambda i, f, be, nu, nr: (i, 0)),
        scratch_shapes=[pltpu.VMEM((tm, d), BF16), pltpu.VMEM((tm, d), F32)],
    )
    return pl.pallas_call(
        functools.partial(_ffn_kernel, row_chunk=256),
        out_shape=jax.ShapeDtypeStruct((r, dw), U32),
        grid_spec=grid_spec,
        compiler_params=_cparams(("parallel", "arbitrary")),
        name="ffn",
    )(block_e, nused, nrows, x_packed, wg, wu, wd)


def _ple_tail(h2, p_ref, nw_ref, wg_ref, wp_ref, fw_ref, o_ref, final):
    hn = _rms_rows(h2, nw_ref[...]).astype(BF16)
    gate = _sigmoid(jnp.dot(hn, wg_ref[...], preferred_element_type=F32))
    pp = jnp.dot(p_ref[...].astype(BF16), wp_ref[...], preferred_element_type=F32)
    h3 = h2 + gate * pp
    if final:
        h3 = _rms_rows(h3, fw_ref[...])
    o_ref[...] = h3


def _ple_dense_kernel(h_ref, y_ref, p_ref, nw_ref, wg_ref, wp_ref, fw_ref, o_ref, *, final):
    h2 = h_ref[...] + _unpack_bf16_pairs(y_ref[...])
    _ple_tail(h2, p_ref, nw_ref, wg_ref, wp_ref, fw_ref, o_ref, final)


def _ple_moe_kernel(dcur_ref, dnext_ref, h_ref, ys_ref, tw_ref, p_ref, nw_ref, wg_ref, wp_ref, fw_ref, o_ref,
                    gbuf_ref, sem, *, final):
    i = pl.program_id(0)
    n_steps = pl.num_programs(0)
    tm = h_ref.shape[0]
    slot = i % 2

    def issue(d_ref, s):
        def body(r, carry):
            for k in range(TOP_K):
                pltpu.make_async_copy(ys_ref.at[pl.ds(d_ref[0, TOP_K * r + k], 1)],
                                      gbuf_ref.at[s, k, pl.ds(r, 1)], sem.at[s]).start()
            return carry
        lax.fori_loop(0, tm, body, 0, unroll=8)

    @pl.when(i == 0)
    def _():
        issue(dcur_ref, 0)

    @pl.when(i + 1 < n_steps)
    def _():
        issue(dnext_ref, 1 - slot)

    for k in range(TOP_K):
        pltpu.make_async_copy(ys_ref.at[pl.ds(0, tm)], gbuf_ref.at[slot, k], sem.at[slot]).wait()
    tw = tw_ref[...]
    y = tw[:, 0:1] * _unpack_bf16_pairs(gbuf_ref[slot, 0])
    for k in range(1, TOP_K):
        y = y + tw[:, k:k + 1] * _unpack_bf16_pairs(gbuf_ref[slot, k])
    _ple_tail(h_ref[...] + y, p_ref, nw_ref, wg_ref, wp_ref, fw_ref, o_ref, final)


def _ple(h1, y_packed, p, layer, nw, w_gate, w_proj, final_w, *, final, dest=None, top_w=None, tm=512):
    n, d = h1.shape
    pd = p.shape[2]
    dw = d // 2
    n_steps = n // tm
    row = lambda i: (i, 0)
    fixed = lambda i: (0, 0)
    tail_specs = [pl.BlockSpec((None, tm, pd), lambda i: (layer, i, 0)),
                  pl.BlockSpec((1, d), fixed),
                  pl.BlockSpec((None, d, d), lambda i: (layer, 0, 0), pipeline_mode=pl.Buffered(1)),
                  pl.BlockSpec((None, pd, d), lambda i: (layer, 0, 0), pipeline_mode=pl.Buffered(1)),
                  pl.BlockSpec((1, d), fixed)]
    tail_args = [p, nw.reshape(1, d), w_gate, w_proj, final_w.reshape(1, d)]
    scratch = []
    if dest is None:
        kern = functools.partial(_ple_dense_kernel, final=final)
        in_specs = [pl.BlockSpec((tm, d), row), pl.BlockSpec((tm, dw), row)] + tail_specs
        args = [h1, y_packed] + tail_args
    else:
        kern = functools.partial(_ple_moe_kernel, final=final)
        dest3 = dest.reshape(n_steps, 1, TOP_K * tm)
        smem = lambda imap: pl.BlockSpec((None, 1, TOP_K * tm), imap, memory_space=pltpu.SMEM)
        in_specs = [smem(lambda i: (i, 0, 0)), smem(lambda i: (jnp.minimum(i + 1, n_steps - 1), 0, 0)),
                    pl.BlockSpec((tm, d), row), pl.BlockSpec(memory_space=pl.ANY),
                    pl.BlockSpec((tm, TOP_K), row)] + tail_specs
        args = [dest3, dest3, h1, y_packed, top_w] + tail_args
        scratch = scratch + [pltpu.VMEM((2, TOP_K, tm, dw), U32), pltpu.SemaphoreType.DMA((2,))]
    return pl.pallas_call(
        kern,
        out_shape=jax.ShapeDtypeStruct((n, d), F32),
        grid=(n_steps,),
        in_specs=in_specs,
        out_specs=pl.BlockSpec((tm, d), row),
        scratch_shapes=scratch,
        compiler_params=_cparams(("arbitrary",)),
        name="ple",
    )(*args)


def _route(logits, block):
    n = logits.shape[0]
    top_logit, top_e = lax.top_k(logits, TOP_K)
    top_w = jax.nn.softmax(top_logit, axis=-1)
    flat_e = top_e.reshape(-1)
    onehot = (flat_e[:, None] == jnp.arange(N_EXPERTS, dtype=flat_e.dtype)[None, :]).astype(jnp.int32)
    csum = jnp.cumsum(onehot, axis=0)
    counts = csum[-1]
    rank = jnp.sum((csum - onehot) * onehot, axis=1)
    padded = (counts + block - 1) // block * block
    padded_end = jnp.cumsum(padded)
    padded_start = padded_end - padded
    dest = (padded_start[flat_e] + rank).astype(jnp.int32)
    n_blocks = -(-(n * TOP_K) // block) + N_EXPERTS
    block_e = jnp.minimum(jnp.searchsorted(padded_end, jnp.arange(n_blocks, dtype=jnp.int32) * block, side='right'),
                          N_EXPERTS - 1).astype(jnp.int32)
    nused = (padded_end[-1] // block).astype(jnp.int32).reshape(1)
    block_e = jnp.where(jnp.arange(n_blocks) < nused[0], block_e, block_e[jnp.maximum(nused[0] - 1, 0)])
    blk = jnp.arange(n_blocks, dtype=jnp.int32)
    nrows = jnp.clip(counts[block_e] - (blk * block - padded_start[block_e]), 0, block).astype(jnp.int32)
    tail = nused[0] + jnp.arange(N_EXPERTS, dtype=jnp.int32)
    zero_start = jnp.concatenate([jnp.where(padded > 0, padded_end // block - 1, -1),
                                  jnp.where(tail < n_blocks, tail, -1)]).astype(jnp.int32)
    return dest.reshape(n, TOP_K), top_w, block_e, nused, nrows, zero_start, n_blocks * block


def kernel(x, p, norm_mix_w, w_in, hgrn_lb_logits, hgrn_norm_w, gla_gate_w2, gla_gate_b, gla_norm_w, w_out,
           norm_ffn_w, dense_w_gate, dense_w_up, dense_w_down, moe_router, moe_w_gate, moe_w_up, moe_w_down,
           norm_ple_w, ple_w_gate, ple_w_proj, final_norm_w):
    batch, seq, d = x.shape
    depth = w_in.shape[0]
    n = batch * seq
    moe_block = 1024

    lbs = jnp.cumsum(jax.nn.softmax(hgrn_lb_logits.astype(F32), axis=0), axis=0)
    lbs = lbs - lbs[0]
    tri_np, masks_np = _level_tables()
    tri = jnp.asarray(tri_np, BF16)
    masks = jnp.asarray(masks_np, BF16)

    h = x.reshape(n, d)
    p3 = p.reshape(depth, n, -1)
    moe_wg = moe_w_gate.reshape((-1,) + moe_w_gate.shape[2:])
    moe_wu = moe_w_up.reshape((-1,) + moe_w_up.shape[2:])
    moe_wd = moe_w_down.reshape((-1,) + moe_w_down.shape[2:])
    w_in_t = jnp.swapaxes(w_in, 1, 2)
    w_out_bf = w_out.astype(BF16)
    ple_wg_bf = ple_w_gate.astype(BF16)
    ple_wp_bf = ple_w_proj.astype(BF16)
    for i in range(depth):
        proj, lr = _in_proj(h, norm_mix_w[i], w_in_t, i)
        o_h = _hgrn(proj, lbs[i], hgrn_norm_w[i], tri, masks, batch=batch)
        w2p = jnp.zeros((LANES, gla_gate_w2.shape[2]), F32).at[:GLA_GATE_RANK].set(gla_gate_w2[i]).astype(BF16)
        o_g = _gla(proj, lr, w2p, gla_gate_b[i], gla_norm_w[i], tri, masks, batch=batch)
        j = i // 2
        last = i == depth - 1
        if i % 2 == 0:
            h1, hn = _out_proj(o_h, o_g, h, w_out_bf, i, norm_ffn_w[i])
            n_blk = n // moe_block
            y = _ffn(hn, dense_w_gate, dense_w_up, dense_w_down,
                     jnp.full((n_blk,), j, jnp.int32), jnp.full((1,), n_blk, jnp.int32),
                     jnp.full((n_blk,), moe_block, jnp.int32), tm=moe_block)
            h = _ple(h1, y, p3, i, norm_ple_w[i], ple_wg_bf, ple_wp_bf, final_norm_w, final=last)
        else:
            rw = jnp.zeros((d, LANES), F32).at[:, :N_EXPERTS].set(moe_router[j])
            rw_hi = rw.astype(BF16)
            router2 = jnp.concatenate([rw_hi, (rw - rw_hi.astype(F32)).astype(BF16)], axis=1)
            h1, hn, logits = _out_proj(o_h, o_g, h, w_out_bf, i, norm_ffn_w[i], router2)
            dest, top_w, block_e, nused, nrows, zero_start, cap = _route(logits[:, :N_EXPERTS], moe_block)
            xs = _dispatch(hn, dest, zero_start, cap, block=moe_block)
            ys = _ffn(xs, moe_wg, moe_wu, moe_wd, block_e + j * N_EXPERTS, nused, nrows, tm=moe_block)
            h = _ple(h1, ys, p3, i, norm_ple_w[i], ple_wg_bf, ple_wp_bf, final_norm_w, final=last,
                     dest=dest, top_w=top_w)
    return h.reshape(batch, seq, d)
```

```python
import functools

import numpy as np
import jax
import jax.numpy as jnp
from jax import lax
from jax.experimental import pallas as pl
from jax.experimental.pallas import tpu as pltpu

F32 = jnp.float32
BF16 = jnp.bfloat16
U32 = jnp.uint32

EPS = 1e-6
LOG2_E = 1.4426950408889634
HGRN_HEADS = 8
HGRN_HEAD_DIM = 128
GLA_HEADS = 4
GLA_HEAD_K = 128
GLA_HEAD_V = 256
GLA_GATE_RANK = 16
GLA_GATE_TEMP = 16.0
N_EXPERTS = 8
TOP_K = 2

LANES = 128
CHUNK = 128
N_LEVELS = 7
VMEM_LIMIT = 56 * 1024 * 1024

_N_MAIN = 6144
_COL_GG = 6144
_COL_GLR = 7168


def _cparams(sem):
    return pltpu.CompilerParams(dimension_semantics=sem, vmem_limit_bytes=VMEM_LIMIT)


def _sigmoid(x):
    return 1.0 / (1.0 + jnp.exp(-x))


def _rms_rows(x, w):
    ms = jnp.mean(x * x, axis=-1, keepdims=True)
    return x * lax.rsqrt(ms + EPS) * w


def _in_proj_kernel(x_ref, nw_ref, w_ref, wlr_ref, o_ref, lr_ref, xn_ref, *, row_chunk):
    tm = x_ref.shape[0]

    @pl.when(pl.program_id(1) == 0)
    def _():
        def body(c, carry):
            r = pl.multiple_of(c * row_chunk, row_chunk)
            xn_ref[pl.ds(r, row_chunk), :] = _rms_rows(x_ref[pl.ds(r, row_chunk), :], nw_ref[...]).astype(BF16)
            return carry
        lax.fori_loop(0, tm // row_chunk, body, 0)
        lr_ref[...] = lax.dot_general(xn_ref[...], wlr_ref[0].astype(BF16), _NT,
                                      preferred_element_type=F32).astype(lr_ref.dtype)

    o_ref[...] = lax.dot_general(xn_ref[...], w_ref[0].astype(BF16), _NT,
                                 preferred_element_type=F32).astype(o_ref.dtype)


def _in_proj(x, nw, w_t, layer, *, tm=1024, tn=1024):
    n, d = x.shape
    tm = min(tm, n)
    n_main = _N_MAIN // tn
    n_gg = (_COL_GLR - _COL_GG) // tn
    gg0 = _COL_GG + GLA_GATE_RANK

    def w_row(j):
        q, t = GLA_GATE_RANK, tn // GLA_GATE_RANK
        return q * jnp.where(j < n_main, j * t, gg0 // q + (j - n_main) * t)

    slab = lambda rows, imap: pl.BlockSpec((pl.Element(1), pl.Element(rows), pl.Element(d)), imap)
    return pl.pallas_call(
        functools.partial(_in_proj_kernel, row_chunk=128),
        out_shape=[jax.ShapeDtypeStruct((n, (n_main + n_gg) * tn), BF16), jax.ShapeDtypeStruct((n, LANES), BF16)],
        grid=(n // tm, n_main + n_gg),
        in_specs=[
            pl.BlockSpec((tm, d), lambda i, j: (i, 0)),
            pl.BlockSpec((1, d), lambda i, j: (0, 0)),
            slab(tn, lambda i, j: (layer, w_row(j), 0)),
            slab(LANES, lambda i, j: (layer, _COL_GG, 0)),
        ],
        out_specs=[pl.BlockSpec((tm, tn), lambda i, j: (i, j)), pl.BlockSpec((tm, LANES), lambda i, j: (i, 0))],
        scratch_shapes=[pltpu.VMEM((tm, d), BF16)],
        compiler_params=_cparams(("parallel", "arbitrary")),
        name="in_proj",
    )(x, nw.reshape(1, d), w_t, w_t)


def _level_tables():
    c = CHUNK
    idx = np.arange(c)
    masks = []
    s = c // 2
    while s >= 1:
        blk = idx // (2 * s)
        upper = (idx % (2 * s)) >= s
        masks.append(((blk[:, None] == blk[None, :]) & upper[:, None] & (~upper)[None, :]).astype(np.float32))
        s //= 2
    masks.append(np.eye(c, dtype=np.float32))
    tri = np.tril(np.ones((c, c), np.float32))
    return np.concatenate([tri, tri], axis=1), np.stack(masks)


def _neg_abs(d):
    return lax.bitcast_convert_type(lax.bitcast_convert_type(d, U32) | U32(0x80000000), F32)


def _level_exponents(g_cum):
    c, dk = g_cum.shape
    row = lax.broadcasted_iota(jnp.int32, (c, dk), 0)
    out = []
    s = c // 2
    while s >= 4:
        nb = c // (2 * s)
        mid = g_cum.reshape(nb, 2 * s, dk)[:, s - 1:s, :]
        mid = jnp.broadcast_to(mid, (nb, 2 * s, dk)).reshape(c, dk)
        out.append(_neg_abs(g_cum - mid))
        s //= 2
    up1 = pltpu.roll(g_cum, 1, 0)
    up2 = pltpu.roll(g_cum, 2, 0)
    dn1 = pltpu.roll(g_cum, c - 1, 0)
    r4 = row % 4
    mid2 = jnp.where(r4 == 0, dn1, jnp.where(r4 == 1, g_cum, jnp.where(r4 == 2, up1, up2)))
    out.append(_neg_abs(g_cum - mid2))
    mid1 = jnp.where(row % 2 == 1, up1, g_cum)
    out.append(_neg_abs(g_cum - mid1))
    return out


_NT = (((1,), (1,)), ((), ()))
_TN = (((0,), (0,)), ((), ()))


def _chunk_prep(g, tri_ref):
    g_hi = g.astype(BF16)
    g_lo = (g - g_hi.astype(F32)).astype(BF16)
    return jnp.dot(tri_ref[...], jnp.concatenate([g_hi, g_lo], axis=0), preferred_element_type=F32) * LOG2_E


def _chunk_local(q, k, v_bf, g_cum, masks_ref):
    c = CHUNK
    q_bf = q.astype(BF16)
    k_bf = k.astype(BF16)
    a = masks_ref[N_LEVELS] * lax.dot_general(q_bf, k_bf, _NT, preferred_element_type=F32).astype(BF16)
    for l, ex in enumerate(_level_exponents(g_cum)):
        e = jnp.exp2(ex).astype(BF16)
        s = lax.dot_general(q_bf * e, k_bf * e, _NT, preferred_element_type=F32)
        a = a + masks_ref[l] * s.astype(BF16)
    g_last = g_cum[c - 1:c, :]
    o_intra = jnp.dot(a, v_bf, preferred_element_type=F32)
    q_dec = (q * jnp.exp2(g_cum)).astype(BF16)
    k_dec = (k * jnp.exp2(g_last - g_cum)).astype(BF16)
    kv = lax.dot_general(v_bf, k_dec, _TN, preferred_element_type=F32)
    return o_intra, q_dec, kv, jnp.exp2(g_last)


def _head_norm(o, w):
    ms = jnp.mean(o * o, axis=-1, keepdims=True)
    return o * lax.rsqrt(ms + EPS) * w


def _scan_block(n_chunks, gates_fn, gate_fn, nw_ref, tri_ref, masks_ref, o_ref, st_ref):
    gates = [gates_fn(ci) for ci in range(n_chunks)]
    cums = [_chunk_prep(g, tri_ref) for (_, _, _, g) in gates]
    parts = [_chunk_local(q, k, v_bf, g_cum, masks_ref) for (q, k, v_bf, _), g_cum in zip(gates, cums)]
    st = st_ref[...]
    for ci, (o_intra, q_dec, kv, decay) in enumerate(parts):
        o = o_intra + lax.dot_general(q_dec, st.astype(BF16), _NT, preferred_element_type=F32)
        rows = pl.ds(ci * CHUNK, CHUNK)
        o_ref[rows, :] = (_head_norm(o, nw_ref[...]) * gate_fn(rows)).astype(o_ref.dtype)
        st = st * decay + kv
    st_ref[...] = st


def _hgrn_kernel(q_ref, f_ref, i_ref, g_ref, lb_ref, nw_ref, tri_ref, masks_ref, o_ref, st_ref):
    @pl.when(pl.program_id(2) == 0)
    def _():
        st_ref[...] = jnp.zeros_like(st_ref)

    lb = lb_ref[...]

    def local(ci):
        rows = pl.ds(ci * CHUNK, CHUNK)
        fgate = lb + (1.0 - lb) * _sigmoid(f_ref[rows, :].astype(F32))
        g = jnp.log(jnp.maximum(fgate, 1e-38))
        k = 1.0 - fgate
        hq = q_ref[rows, :].astype(F32)
        return hq * _sigmoid(hq), k, i_ref[rows, :], g

    def gate(rows):
        return _sigmoid(g_ref[rows, :].astype(F32))

    _scan_block(q_ref.shape[0] // CHUNK, local, gate, nw_ref, tri_ref, masks_ref, o_ref, st_ref)


def _gla_kernel(q_ref, k_ref, v_ref, g_ref, lr_ref, w2_ref, b_ref, nw_ref, tri_ref, masks_ref, o_ref, st_ref):
    @pl.when(pl.program_id(2) == 0)
    def _():
        st_ref[...] = jnp.zeros_like(st_ref)

    def local(ci):
        rows = pl.ds(ci * CHUNK, CHUNK)
        u = jnp.dot(lr_ref[rows, :], w2_ref[...], preferred_element_type=F32) + b_ref[...]
        g = (jnp.minimum(u, 0.0) - jnp.log(1.0 + jnp.exp(-jnp.abs(u)))) * (1.0 / GLA_GATE_TEMP)
        q = q_ref[rows, :].astype(F32) * (GLA_HEAD_K ** -0.5)
        return q, k_ref[rows, :].astype(F32), v_ref[rows, :], g

    def gate(rows):
        gg = g_ref[rows, :].astype(F32)
        return gg * _sigmoid(gg)

    _scan_block(q_ref.shape[0] // CHUNK, local, gate, nw_ref, tri_ref, masks_ref, o_ref, st_ref)


def _const_spec(shape):
    nd = len(shape)
    return pl.BlockSpec(shape, lambda b, h, t: (0,) * nd)


def _hgrn(proj, lb, nw, tri, masks, *, batch, tb=2048):
    n = proj.shape[0]
    nt = n // batch // tb
    dk = HGRN_HEAD_DIM
    hh = HGRN_HEADS

    def col(off):
        return pl.BlockSpec((tb, dk), lambda b, h, t: (b * nt + t, off + h))

    return pl.pallas_call(
        _hgrn_kernel,
        out_shape=jax.ShapeDtypeStruct((n, hh * dk), BF16),
        grid=(batch, hh, nt),
        in_specs=[col(0), col(hh), col(2 * hh), col(3 * hh),
                  pl.BlockSpec((1, dk), lambda b, h, t: (0, h)),
                  pl.BlockSpec((1, dk), lambda b, h, t: (0, h)),
                  _const_spec(tri.shape), _const_spec(masks.shape)],
        out_specs=pl.BlockSpec((tb, dk), lambda b, h, t: (b * nt + t, h)),
        scratch_shapes=[pltpu.VMEM((dk, dk), F32)],
        compiler_params=_cparams(("parallel", "parallel", "arbitrary")),
        name="hgrn",
    )(proj, proj, proj, proj, lb.reshape(1, -1), nw.reshape(1, -1), tri, masks)


def _gla(proj, lr, w2p, b, nw, tri, masks, *, batch, tb=2048):
    n = proj.shape[0]
    nt = n // batch // tb
    dk, dv, hh = GLA_HEAD_K, GLA_HEAD_V, GLA_HEADS
    q0 = 4096 // dk
    k0 = q0 + hh
    v0 = 5120 // dv
    g0 = _COL_GG // dv
    return pl.pallas_call(
        _gla_kernel,
        out_shape=jax.ShapeDtypeStruct((n, hh * dv), BF16),
        grid=(batch, hh, nt),
        in_specs=[pl.BlockSpec((tb, dk), lambda b, h, t: (b * nt + t, q0 + h)),
                  pl.BlockSpec((tb, dk), lambda b, h, t: (b * nt + t, k0 + h)),
                  pl.BlockSpec((tb, dv), lambda b, h, t: (b * nt + t, v0 + h)),
                  pl.BlockSpec((tb, dv), lambda b, h, t: (b * nt + t, g0 + h)),
                  pl.BlockSpec((tb, LANES), lambda b, h, t: (b * nt + t, 0)),
                  pl.BlockSpec((LANES, dk), lambda b, h, t: (0, h)),
                  pl.BlockSpec((1, dk), lambda b, h, t: (0, h)),
                  pl.BlockSpec((1, dv), lambda b, h, t: (0, h)),
                  _const_spec(tri.shape), _const_spec(masks.shape)],
        out_specs=pl.BlockSpec((tb, dv), lambda b, h, t: (b * nt + t, h)),
        scratch_shapes=[pltpu.VMEM((dv, dk), F32)],
        compiler_params=_cparams(("parallel", "parallel", "arbitrary")),
        name="gla",
    )(proj, proj, proj, proj, lr, w2p, b.reshape(1, -1), nw.reshape(1, -1), tri, masks)


def _pack_bf16_pairs(a):
    c = a.shape[1] // 2
    bits = lax.bitcast_convert_type(a.astype(BF16).astype(F32), U32)
    return (bits[:, :c] >> 16) | (bits[:, c:] & U32(0xFFFF0000))


def _unpack_bf16_pairs(w):
    lo = lax.bitcast_convert_type(w << 16, F32)
    hi = lax.bitcast_convert_type(w & U32(0xFFFF0000), F32)
    return jnp.concatenate([lo, hi], axis=1)


def _store_token_tiles(ref, tok0, words):
    rows, dw = words.shape
    s = dw // LANES
    for c in range(s):
        ref[pl.ds(tok0 * s + c, rows, stride=s), :] = words[:, c * LANES:(c + 1) * LANES]


def _load_token_tiles(ref, tok0, rows, s):
    return jnp.concatenate([ref[pl.ds(tok0 * s + c, rows, stride=s), :] for c in range(s)], axis=1)


def _token_tile(ref, tok, s):
    return ref.at[pl.ds(pl.multiple_of(tok * s, s), s)]


def _out_proj_kernel(oh_ref, og_ref, res_ref, w_ref, nw_ref, *rest, with_router):
    if with_router:
        rw_ref, h_ref, hn_ref, lg_ref = rest
    else:
        h_ref, hn_ref = rest

    kh = oh_ref.shape[1]
    acc = jnp.dot(oh_ref[...], w_ref[0:kh, :], preferred_element_type=F32)
    acc = acc + jnp.dot(og_ref[...], w_ref[kh:, :], preferred_element_type=F32)
    h = res_ref[...] + acc
    h_ref[...] = h
    hn = _rms_rows(h, nw_ref[...])
    _store_token_tiles(hn_ref, 0, _pack_bf16_pairs(hn))
    if with_router:
        hn_hi = hn.astype(BF16)
        hn_lo = (hn - hn_hi.astype(F32)).astype(BF16)
        rw2 = rw_ref[...]
        l2 = jnp.dot(hn_hi, rw2, preferred_element_type=F32)
        lg_ref[...] = (l2[:, :LANES] + l2[:, LANES:]
                       + jnp.dot(hn_lo, rw2[:, :LANES], preferred_element_type=F32))


def _out_proj(o_h, o_g, res, w_out, layer, nw, router_pad=None):
    n, d = res.shape
    kh, kg = o_h.shape[1], o_g.shape[1]
    with_router = router_pad is not None
    tm = 256 if with_router else 512
    row = lambda i: (i, 0)
    fixed = lambda i: (0, 0)
    in_specs = [pl.BlockSpec((tm, kh), row), pl.BlockSpec((tm, kg), row), pl.BlockSpec((tm, d), row),
                pl.BlockSpec((None, kh + kg, d), lambda i: (layer, 0, 0), pipeline_mode=pl.Buffered(1)),
                pl.BlockSpec((1, d), fixed)]
    args = [o_h, o_g, res, w_out, nw.reshape(1, d)]
    s = d // 2 // LANES
    out_shape = [jax.ShapeDtypeStruct((n, d), F32), jax.ShapeDtypeStruct((n * s, LANES), U32)]
    out_specs = [pl.BlockSpec((tm, d), row), pl.BlockSpec((tm * s, LANES), row)]
    if with_router:
        in_specs.append(pl.BlockSpec((d, 2 * LANES), fixed))
        args.append(router_pad)
        out_shape.append(jax.ShapeDtypeStruct((n, LANES), F32))
        out_specs.append(pl.BlockSpec((tm, LANES), row))
    return pl.pallas_call(
        functools.partial(_out_proj_kernel, with_router=with_router),
        out_shape=out_shape,
        grid=(n // tm,),
        in_specs=in_specs,
        out_specs=out_specs,
        compiler_params=_cparams(("parallel",)),
        name="out_proj",
    )(*args)


def _dispatch_kernel(zs_ref, dest_ref, hn_ref, xs_ref, zero_ref, sem, *, tb, s):
    i = pl.program_id(0)

    zrows = zero_ref.shape[0]

    def zero_copy(e):
        start = pl.multiple_of(zs_ref[e] * zrows, zrows)
        return pltpu.make_async_copy(zero_ref, xs_ref.at[pl.ds(start, zrows)], sem.at[1])

    @pl.when(i == 0)
    def _():
        zero_ref[...] = jnp.zeros_like(zero_ref)
        for e in range(zs_ref.shape[0]):
            @pl.when(zs_ref[e] >= 0)
            def _():
                zero_copy(e).start()
        for e in range(zs_ref.shape[0]):
            @pl.when(zs_ref[e] >= 0)
            def _():
                zero_copy(e).wait()

    def body(r, carry):
        src = _token_tile(hn_ref, r, s)
        for k in range(TOP_K):
            pltpu.make_async_copy(src, _token_tile(xs_ref, dest_ref[0, TOP_K * r + k], s), sem.at[0]).start()
        return carry

    lax.fori_loop(0, tb, body, 0, unroll=8)
    for k in range(TOP_K):
        pltpu.make_async_copy(hn_ref, xs_ref.at[pl.ds(0, tb * s)], sem.at[0]).wait()


def _dispatch(hn_packed, dest, zero_start, cap, s, *, block, tb=512):
    n = hn_packed.shape[0] // s
    tb = min(tb, n)
    grid_spec = pltpu.PrefetchScalarGridSpec(
        num_scalar_prefetch=1,
        grid=(n // tb,),
        in_specs=[pl.BlockSpec((None, 1, TOP_K * tb), lambda i, zs: (i, 0, 0), memory_space=pltpu.SMEM),
                  pl.BlockSpec((tb * s, LANES), lambda i, zs: (i, 0))],
        out_specs=pl.BlockSpec(memory_space=pl.ANY),
        scratch_shapes=[pltpu.VMEM((block * s, LANES), U32), pltpu.SemaphoreType.DMA((2,))],
    )
    return pl.pallas_call(
        functools.partial(_dispatch_kernel, tb=tb, s=s),
        out_shape=jax.ShapeDtypeStruct((cap * s, LANES), U32),
        grid_spec=grid_spec,
        compiler_params=_cparams(("arbitrary",)),
        name="dispatch",
    )(zero_start, dest.reshape(n // tb, 1, TOP_K * tb), hn_packed)


def _ffn_kernel(be_ref, nused_ref, nrows_ref, x_ref, wg_ref, wu_ref, wd_ref, o_ref, xb_ref, acc_ref, *, row_chunk, s):
    i = pl.program_id(0)
    f = pl.program_id(1)
    used = i < nused_ref[0]
    n_chunks = xb_ref.shape[0] // row_chunk

    @pl.when(f == 0)
    def _():
        acc_ref[...] = jnp.zeros_like(acc_ref)

    @pl.when(jnp.logical_and(f == 0, used))
    def _():
        def body(c, carry):
            tok0 = pl.multiple_of(c * row_chunk, row_chunk)
            words = _load_token_tiles(x_ref, tok0, row_chunk, s)
            xb_ref[pl.ds(tok0, row_chunk), :] = _unpack_bf16_pairs(words).astype(BF16)
            return carry
        lax.fori_loop(0, n_chunks, body, 0)

    live = (nrows_ref[i] + row_chunk - 1) // row_chunk
    for nc in range(1, n_chunks + 1):
        @pl.when(jnp.logical_and(used, live == nc))
        def _():
            rows = pl.ds(0, nc * row_chunk)
            x = xb_ref[rows, :]
            g = jnp.dot(x, wg_ref[0].astype(BF16), preferred_element_type=F32)
            u = jnp.dot(x, wu_ref[0].astype(BF16), preferred_element_type=F32)
            a = (g * _sigmoid(g) * u).astype(BF16)
            acc_ref[rows, :] += jnp.dot(a, wd_ref[0].astype(BF16), preferred_element_type=F32)

    @pl.when(f == pl.num_programs(1) - 1)
    def _():
        def body(c, carry):
            tok0 = pl.multiple_of(c * row_chunk, row_chunk)
            _store_token_tiles(o_ref, tok0, _pack_bf16_pairs(acc_ref[pl.ds(tok0, row_chunk), :]))
            return carry
        lax.fori_loop(0, n_chunks, body, 0)


def _ffn(x_packed, wg, wu, wd, block_e, nused, nrows, *, tm=1024, tf=512):
    d = wg.shape[1]
    s = d // 2 // LANES
    r = x_packed.shape[0] // s
    ff = wg.shape[2]
    nf = ff // tf

    def f_eff(i, f, nu):
        return jnp.where(i < nu[0], f, nf - 1)

    grid_spec = pltpu.PrefetchScalarGridSpec(
        num_scalar_prefetch=3,
        grid=(r // tm, nf),
        in_specs=[
            pl.BlockSpec((tm * s, LANES), lambda i, f, be, nu, nr: (i, 0), pipeline_mode=pl.Buffered(1)),
            pl.BlockSpec((1, d, tf), lambda i, f, be, nu, nr: (be[i], 0, f_eff(i, f, nu))),
            pl.BlockSpec((1, d, tf), lambda i, f, be, nu, nr: (be[i], 0, f_eff(i, f, nu))),
            pl.BlockSpec((1, tf, d), lambda i, f, be, nu, nr: (be[i], f_eff(i, f, nu), 0)),
        ],
        out_specs=pl.BlockSpec((tm * s, LANES), lambda i, f, be, nu, nr: (i, 0)),
        scratch_shapes=[pltpu.VMEM((tm, d), BF16), pltpu.VMEM((tm, d), F32)],
    )
    return pl.pallas_call(
        functools.partial(_ffn_kernel, row_chunk=256, s=s),
        out_shape=jax.ShapeDtypeStruct((r * s, LANES), U32),
        grid_spec=grid_spec,
        compiler_params=_cparams(("parallel", "arbitrary")),
        name="ffn",
    )(block_e, nused, nrows, x_packed, wg, wu, wd)


def _ple_tail(h2, p_ref, nw_ref, wg_ref, wp_ref, fw_ref, o_ref, final):
    hn = _rms_rows(h2, nw_ref[...]).astype(BF16)
    gate = _sigmoid(jnp.dot(hn, wg_ref[...], preferred_element_type=F32))
    pp = jnp.dot(p_ref[...].astype(BF16), wp_ref[...], preferred_element_type=F32)
    h3 = h2 + gate * pp
    if final:
        h3 = _rms_rows(h3, fw_ref[...])
    o_ref[...] = h3


def _ple_dense_kernel(h_ref, y_ref, p_ref, nw_ref, wg_ref, wp_ref, fw_ref, o_ref, *, final):
    tm = h_ref.shape[0]
    h2 = h_ref[...] + _unpack_bf16_pairs(_load_token_tiles(y_ref, 0, tm, y_ref.shape[0] // tm))
    _ple_tail(h2, p_ref, nw_ref, wg_ref, wp_ref, fw_ref, o_ref, final)


def _ple_moe_kernel(dcur_ref, dnext_ref, h_ref, ys_ref, tw_ref, p_ref, nw_ref, wg_ref, wp_ref, fw_ref, o_ref,
                    gbuf_ref, sem, *, final):
    i = pl.program_id(0)
    n_steps = pl.num_programs(0)
    tm = h_ref.shape[0]
    slot = i % 2

    ts = gbuf_ref.shape[2] // tm

    def issue(d_ref, b):
        def body(r, carry):
            for k in range(TOP_K):
                pltpu.make_async_copy(_token_tile(ys_ref, d_ref[0, TOP_K * r + k], ts),
                                      _token_tile(gbuf_ref.at[b, k], r, ts), sem.at[b]).start()
            return carry
        lax.fori_loop(0, tm, body, 0, unroll=8)

    @pl.when(i == 0)
    def _():
        issue(dcur_ref, 0)

    @pl.when(i + 1 < n_steps)
    def _():
        issue(dnext_ref, 1 - slot)

    for k in range(TOP_K):
        pltpu.make_async_copy(ys_ref.at[pl.ds(0, tm * ts)], gbuf_ref.at[slot, k], sem.at[slot]).wait()
    tw = tw_ref[...]
    y = tw[:, 0:1] * _unpack_bf16_pairs(_load_token_tiles(gbuf_ref.at[slot, 0], 0, tm, ts))
    for k in range(1, TOP_K):
        y = y + tw[:, k:k + 1] * _unpack_bf16_pairs(_load_token_tiles(gbuf_ref.at[slot, k], 0, tm, ts))
    _ple_tail(h_ref[...] + y, p_ref, nw_ref, wg_ref, wp_ref, fw_ref, o_ref, final)


def _ple(h1, y_packed, p, layer, nw, w_gate, w_proj, final_w, *, final, dest=None, top_w=None, tm=512):
    n, d = h1.shape
    pd = p.shape[2]
    s = d // 2 // LANES
    n_steps = n // tm
    row = lambda i: (i, 0)
    fixed = lambda i: (0, 0)
    tail_specs = [pl.BlockSpec((None, tm, pd), lambda i: (layer, i, 0)),
                  pl.BlockSpec((1, d), fixed),
                  pl.BlockSpec((None, d, d), lambda i: (layer, 0, 0), pipeline_mode=pl.Buffered(1)),
                  pl.BlockSpec((None, pd, d), lambda i: (layer, 0, 0), pipeline_mode=pl.Buffered(1)),
                  pl.BlockSpec((1, d), fixed)]
    tail_args = [p, nw.reshape(1, d), w_gate, w_proj, final_w.reshape(1, d)]
    scratch = []
    if dest is None:
        kern = functools.partial(_ple_dense_kernel, final=final)
        in_specs = [pl.BlockSpec((tm, d), row), pl.BlockSpec((tm * s, LANES), row)] + tail_specs
        args = [h1, y_packed] + tail_args
    else:
        kern = functools.partial(_ple_moe_kernel, final=final)
        dest3 = dest.reshape(n_steps, 1, TOP_K * tm)
        smem = lambda imap: pl.BlockSpec((None, 1, TOP_K * tm), imap, memory_space=pltpu.SMEM)
        in_specs = [smem(lambda i: (i, 0, 0)), smem(lambda i: (jnp.minimum(i + 1, n_steps - 1), 0, 0)),
                    pl.BlockSpec((tm, d), row), pl.BlockSpec(memory_space=pl.ANY),
                    pl.BlockSpec((tm, TOP_K), row)] + tail_specs
        args = [dest3, dest3, h1, y_packed, top_w] + tail_args
        scratch = scratch + [pltpu.VMEM((2, TOP_K, tm * s, LANES), U32), pltpu.SemaphoreType.DMA((2,))]
    return pl.pallas_call(
        kern,
        out_shape=jax.ShapeDtypeStruct((n, d), F32),
        grid=(n_steps,),
        in_specs=in_specs,
        out_specs=pl.BlockSpec((tm, d), row),
        scratch_shapes=scratch,
        compiler_params=_cparams(("arbitrary",)),
        name="ple",
    )(*args)


def _route(logits, block):
    n = logits.shape[0]
    top_logit, top_e = lax.top_k(logits, TOP_K)
    top_w = jax.nn.softmax(top_logit, axis=-1)
    flat_e = top_e.reshape(-1)
    onehot = (flat_e[:, None] == jnp.arange(N_EXPERTS, dtype=flat_e.dtype)[None, :]).astype(jnp.int32)
    csum = jnp.cumsum(onehot, axis=0)
    counts = csum[-1]
    rank = jnp.sum((csum - onehot) * onehot, axis=1)
    padded = (counts + block - 1) // block * block
    padded_end = jnp.cumsum(padded)
    padded_start = padded_end - padded
    dest = (padded_start[flat_e] + rank).astype(jnp.int32)
    n_blocks = -(-(n * TOP_K) // block) + N_EXPERTS
    block_e = jnp.minimum(jnp.searchsorted(padded_end, jnp.arange(n_blocks, dtype=jnp.int32) * block, side='right'),
                          N_EXPERTS - 1).astype(jnp.int32)
    nused = (padded_end[-1] // block).astype(jnp.int32).reshape(1)
    block_e = jnp.where(jnp.arange(n_blocks) < nused[0], block_e, block_e[jnp.maximum(nused[0] - 1, 0)])
    blk = jnp.arange(n_blocks, dtype=jnp.int32)
    nrows = jnp.clip(counts[block_e] - (blk * block - padded_start[block_e]), 0, block).astype(jnp.int32)
    tail = nused[0] + jnp.arange(N_EXPERTS, dtype=jnp.int32)
    zero_start = jnp.concatenate([jnp.where(padded > 0, padded_end // block - 1, -1),
                                  jnp.where(tail < n_blocks, tail, -1)]).astype(jnp.int32)
    return dest.reshape(n, TOP_K), top_w, block_e, nused, nrows, zero_start, n_blocks * block


def kernel(x, p, norm_mix_w, w_in, hgrn_lb_logits, hgrn_norm_w, gla_gate_w2, gla_gate_b, gla_norm_w, w_out,
           norm_ffn_w, dense_w_gate, dense_w_up, dense_w_down, moe_router, moe_w_gate, moe_w_up, moe_w_down,
           norm_ple_w, ple_w_gate, ple_w_proj, final_norm_w):
    batch, seq, d = x.shape
    depth = w_in.shape[0]
    n = batch * seq
    moe_block = 1024

    lbs = jnp.cumsum(jax.nn.softmax(hgrn_lb_logits.astype(F32), axis=0), axis=0)
    lbs = lbs - lbs[0]
    tri_np, masks_np = _level_tables()
    tri = jnp.asarray(tri_np, BF16)
    masks = jnp.asarray(masks_np, BF16)

    h = x.reshape(n, d)
    p3 = p.reshape(depth, n, -1)
    moe_wg = moe_w_gate.reshape((-1,) + moe_w_gate.shape[2:])
    moe_wu = moe_w_up.reshape((-1,) + moe_w_up.shape[2:])
    moe_wd = moe_w_down.reshape((-1,) + moe_w_down.shape[2:])
    w_in_t = jnp.swapaxes(w_in, 1, 2)
    w_out_bf = w_out.astype(BF16)
    ple_wg_bf = ple_w_gate.astype(BF16)
    ple_wp_bf = ple_w_proj.astype(BF16)
    for i in range(depth):
        proj, lr = _in_proj(h, norm_mix_w[i], w_in_t, i)
        o_h = _hgrn(proj, lbs[i], hgrn_norm_w[i], tri, masks, batch=batch)
        w2p = jnp.zeros((LANES, gla_gate_w2.shape[2]), F32).at[:GLA_GATE_RANK].set(gla_gate_w2[i]).astype(BF16)
        o_g = _gla(proj, lr, w2p, gla_gate_b[i], gla_norm_w[i], tri, masks, batch=batch)
        j = i // 2
        last = i == depth - 1
        if i % 2 == 0:
            h1, hn = _out_proj(o_h, o_g, h, w_out_bf, i, norm_ffn_w[i])
            n_blk = n // moe_block
            y = _ffn(hn, dense_w_gate, dense_w_up, dense_w_down,
                     jnp.full((n_blk,), j, jnp.int32), jnp.full((1,), n_blk, jnp.int32),
                     jnp.full((n_blk,), moe_block, jnp.int32), tm=moe_block)
            h = _ple(h1, y, p3, i, norm_ple_w[i], ple_wg_bf, ple_wp_bf, final_norm_w, final=last)
        else:
            rw = jnp.zeros((d, LANES), F32).at[:, :N_EXPERTS].set(moe_router[j])
            rw_hi = rw.astype(BF16)
            router2 = jnp.concatenate([rw_hi, (rw - rw_hi.astype(F32)).astype(BF16)], axis=1)
            h1, hn, logits = _out_proj(o_h, o_g, h, w_out_bf, i, norm_ffn_w[i], router2)
            dest, top_w, block_e, nused, nrows, zero_start, cap = _route(logits[:, :N_EXPERTS], moe_block)
            xs = _dispatch(hn, dest, zero_start, cap, d // 2 // LANES, block=moe_block)
            ys = _ffn(xs, moe_wg, moe_wu, moe_wd, block_e + j * N_EXPERTS, nused, nrows, tm=moe_block)
            h = _ple(h1, ys, p3, i, norm_ple_w[i], ple_wg_bf, ple_wp_bf, final_norm_w, final=last,
                     dest=dest, top_w=top_w)
    return h.reshape(batch, seq, d)
```

```python
import functools

import numpy as np
import jax
import jax.numpy as jnp
from jax import lax
from jax.experimental import pallas as pl
from jax.experimental.pallas import tpu as pltpu

F32 = jnp.float32
BF16 = jnp.bfloat16
U32 = jnp.uint32

EPS = 1e-6
LOG2_E = 1.4426950408889634
HGRN_HEADS = 8
HGRN_HEAD_DIM = 128
GLA_HEADS = 4
GLA_HEAD_K = 128
GLA_HEAD_V = 256
GLA_GATE_RANK = 16
GLA_GATE_TEMP = 16.0
N_EXPERTS = 8
TOP_K = 2

LANES = 128
CHUNK = 128
N_LEVELS = 7
VMEM_LIMIT = 56 * 1024 * 1024

_N_MAIN = 6144
_COL_GG = 6144
_COL_GLR = 7168


def _cparams(sem):
    return pltpu.CompilerParams(dimension_semantics=sem, vmem_limit_bytes=VMEM_LIMIT)


def _sigmoid(x):
    return 1.0 / (1.0 + jnp.exp(-x))


def _rms_rows(x, w):
    ms = jnp.mean(x * x, axis=-1, keepdims=True)
    return x * lax.rsqrt(ms + EPS) * w


def _in_proj_kernel(x_ref, nw_ref, w_ref, wlr_ref, o_ref, lr_ref, xn_ref, *, row_chunk):
    tm = x_ref.shape[0]

    @pl.when(pl.program_id(1) == 0)
    def _():
        def body(c, carry):
            r = pl.multiple_of(c * row_chunk, row_chunk)
            xn_ref[pl.ds(r, row_chunk), :] = _rms_rows(x_ref[pl.ds(r, row_chunk), :], nw_ref[...]).astype(BF16)
            return carry
        lax.fori_loop(0, tm // row_chunk, body, 0)
        lr_ref[...] = lax.dot_general(xn_ref[...], wlr_ref[0].astype(BF16), _NT,
                                      preferred_element_type=F32).astype(lr_ref.dtype)

    o_ref[...] = lax.dot_general(xn_ref[...], w_ref[0].astype(BF16), _NT,
                                 preferred_element_type=F32).astype(o_ref.dtype)


def _in_proj(x, nw, w_t, layer, *, tm=1024, tn=1024):
    n, d = x.shape
    tm = min(tm, n)
    n_main = _N_MAIN // tn
    n_gg = (_COL_GLR - _COL_GG) // tn
    gg0 = _COL_GG + GLA_GATE_RANK

    def w_row(j):
        q, t = GLA_GATE_RANK, tn // GLA_GATE_RANK
        return q * jnp.where(j < n_main, j * t, gg0 // q + (j - n_main) * t)

    slab = lambda rows, imap: pl.BlockSpec((pl.Element(1), pl.Element(rows), pl.Element(d)), imap)
    return pl.pallas_call(
        functools.partial(_in_proj_kernel, row_chunk=128),
        out_shape=[jax.ShapeDtypeStruct((n, (n_main + n_gg) * tn), BF16), jax.ShapeDtypeStruct((n, LANES), BF16)],
        grid=(n // tm, n_main + n_gg),
        in_specs=[
            pl.BlockSpec((tm, d), lambda i, j: (i, 0)),
            pl.BlockSpec((1, d), lambda i, j: (0, 0)),
            slab(tn, lambda i, j: (layer, w_row(j), 0)),
            slab(LANES, lambda i, j: (layer, _COL_GG, 0)),
        ],
        out_specs=[pl.BlockSpec((tm, tn), lambda i, j: (i, j)), pl.BlockSpec((tm, LANES), lambda i, j: (i, 0))],
        scratch_shapes=[pltpu.VMEM((tm, d), BF16)],
        compiler_params=_cparams(("parallel", "arbitrary")),
        name="in_proj",
    )(x, nw.reshape(1, d), w_t, w_t)


def _level_tables():
    c = CHUNK
    idx = np.arange(c)
    masks = []
    s = c // 2
    while s >= 1:
        blk = idx // (2 * s)
        upper = (idx % (2 * s)) >= s
        masks.append(((blk[:, None] == blk[None, :]) & upper[:, None] & (~upper)[None, :]).astype(np.float32))
        s //= 2
    masks.append(np.eye(c, dtype=np.float32))
    tri = np.tril(np.ones((c, c), np.float32))
    return np.concatenate([tri, tri], axis=1), np.stack(masks)


def _neg_abs(d):
    return lax.bitcast_convert_type(lax.bitcast_convert_type(d, U32) | U32(0x80000000), F32)


def _level_exponents(g_cum):
    c, dk = g_cum.shape
    row = lax.broadcasted_iota(jnp.int32, (c, dk), 0)
    out = []
    s = c // 2
    while s >= 4:
        nb = c // (2 * s)
        mid = g_cum.reshape(nb, 2 * s, dk)[:, s - 1:s, :]
        mid = jnp.broadcast_to(mid, (nb, 2 * s, dk)).reshape(c, dk)
        out.append(_neg_abs(g_cum - mid))
        s //= 2
    up1 = pltpu.roll(g_cum, 1, 0)
    up2 = pltpu.roll(g_cum, 2, 0)
    dn1 = pltpu.roll(g_cum, c - 1, 0)
    r4 = row % 4
    mid2 = jnp.where(r4 == 0, dn1, jnp.where(r4 == 1, g_cum, jnp.where(r4 == 2, up1, up2)))
    out.append(_neg_abs(g_cum - mid2))
    mid1 = jnp.where(row % 2 == 1, up1, g_cum)
    out.append(_neg_abs(g_cum - mid1))
    return out


_NT = (((1,), (1,)), ((), ()))
_TN = (((0,), (0,)), ((), ()))


def _chunk_prep(g, tri_ref):
    g_hi = g.astype(BF16)
    g_lo = (g - g_hi.astype(F32)).astype(BF16)
    return jnp.dot(tri_ref[...], jnp.concatenate([g_hi, g_lo], axis=0), preferred_element_type=F32) * LOG2_E


def _chunk_local(q, k, v_bf, g_cum, masks_ref):
    c = CHUNK
    q_bf = q.astype(BF16)
    k_bf = k.astype(BF16)
    a = masks_ref[N_LEVELS] * lax.dot_general(q_bf, k_bf, _NT, preferred_element_type=F32).astype(BF16)
    for l, ex in enumerate(_level_exponents(g_cum)):
        e = jnp.exp2(ex).astype(BF16)
        s = lax.dot_general(q_bf * e, k_bf * e, _NT, preferred_element_type=F32)
        a = a + masks_ref[l] * s.astype(BF16)
    g_last = g_cum[c - 1:c, :]
    o_intra = jnp.dot(a, v_bf, preferred_element_type=F32)
    q_dec = (q * jnp.exp2(g_cum)).astype(BF16)
    k_dec = (k * jnp.exp2(g_last - g_cum)).astype(BF16)
    kv = lax.dot_general(v_bf, k_dec, _TN, preferred_element_type=F32)
    return o_intra, q_dec, kv, jnp.exp2(g_last)


def _head_norm(o, w):
    ms = jnp.mean(o * o, axis=-1, keepdims=True)
    return o * lax.rsqrt(ms + EPS) * w


def _scan_block(n_chunks, gates_fn, gate_fn, nw_ref, tri_ref, masks_ref, o_ref, st_ref):
    gates = [gates_fn(ci) for ci in range(n_chunks)]
    cums = [_chunk_prep(g, tri_ref) for (_, _, _, g) in gates]
    parts = [_chunk_local(q, k, v_bf, g_cum, masks_ref) for (q, k, v_bf, _), g_cum in zip(gates, cums)]
    st = st_ref[...]
    for ci, (o_intra, q_dec, kv, decay) in enumerate(parts):
        o = o_intra + lax.dot_general(q_dec, st.astype(BF16), _NT, preferred_element_type=F32)
        rows = pl.ds(ci * CHUNK, CHUNK)
        o_ref[rows, :] = (_head_norm(o, nw_ref[...]) * gate_fn(rows)).astype(o_ref.dtype)
        st = st * decay + kv
    st_ref[...] = st


def _hgrn_kernel(q_ref, f_ref, i_ref, g_ref, lb_ref, nw_ref, tri_ref, masks_ref, o_ref, st_ref):
    @pl.when(pl.program_id(2) == 0)
    def _():
        st_ref[...] = jnp.zeros_like(st_ref)

    lb = lb_ref[...]

    def local(ci):
        rows = pl.ds(ci * CHUNK, CHUNK)
        fgate = lb + (1.0 - lb) * _sigmoid(f_ref[rows, :].astype(F32))
        g = jnp.log(jnp.maximum(fgate, 1e-38))
        k = 1.0 - fgate
        hq = q_ref[rows, :].astype(F32)
        return hq * _sigmoid(hq), k, i_ref[rows, :], g

    def gate(rows):
        return _sigmoid(g_ref[rows, :].astype(F32))

    _scan_block(q_ref.shape[0] // CHUNK, local, gate, nw_ref, tri_ref, masks_ref, o_ref, st_ref)


def _gla_kernel(q_ref, k_ref, v_ref, g_ref, lr_ref, w2_ref, b_ref, nw_ref, tri_ref, masks_ref, o_ref, st_ref):
    @pl.when(pl.program_id(2) == 0)
    def _():
        st_ref[...] = jnp.zeros_like(st_ref)

    def local(ci):
        rows = pl.ds(ci * CHUNK, CHUNK)
        u = jnp.dot(lr_ref[rows, :], w2_ref[...], preferred_element_type=F32) + b_ref[...]
        g = (jnp.minimum(u, 0.0) - jnp.log(1.0 + jnp.exp(-jnp.abs(u)))) * (1.0 / GLA_GATE_TEMP)
        q = q_ref[rows, :].astype(F32) * (GLA_HEAD_K ** -0.5)
        return q, k_ref[rows, :].astype(F32), v_ref[rows, :], g

    def gate(rows):
        gg = g_ref[rows, :].astype(F32)
        return gg * _sigmoid(gg)

    _scan_block(q_ref.shape[0] // CHUNK, local, gate, nw_ref, tri_ref, masks_ref, o_ref, st_ref)


def _const_spec(shape):
    nd = len(shape)
    return pl.BlockSpec(shape, lambda b, h, t: (0,) * nd)


def _hgrn(proj, lb, nw, tri, masks, *, batch, tb=2048):
    n = proj.shape[0]
    nt = n // batch // tb
    dk = HGRN_HEAD_DIM
    hh = HGRN_HEADS

    def col(off):
        return pl.BlockSpec((tb, dk), lambda b, h, t: (b * nt + t, off + h))

    return pl.pallas_call(
        _hgrn_kernel,
        out_shape=jax.ShapeDtypeStruct((n, hh * dk), BF16),
        grid=(batch, hh, nt),
        in_specs=[col(0), col(hh), col(2 * hh), col(3 * hh),
                  pl.BlockSpec((1, dk), lambda b, h, t: (0, h)),
                  pl.BlockSpec((1, dk), lambda b, h, t: (0, h)),
                  _const_spec(tri.shape), _const_spec(masks.shape)],
        out_specs=pl.BlockSpec((tb, dk), lambda b, h, t: (b * nt + t, h)),
        scratch_shapes=[pltpu.VMEM((dk, dk), F32)],
        compiler_params=_cparams(("parallel", "parallel", "arbitrary")),
        name="hgrn",
    )(proj, proj, proj, proj, lb.reshape(1, -1), nw.reshape(1, -1), tri, masks)


def _gla(proj, lr, w2p, b, nw, tri, masks, *, batch, tb=1024):
    n = proj.shape[0]
    nt = n // batch // tb
    dk, dv, hh = GLA_HEAD_K, GLA_HEAD_V, GLA_HEADS
    q0 = 4096 // dk
    k0 = q0 + hh
    v0 = 5120 // dv
    g0 = _COL_GG // dv
    return pl.pallas_call(
        _gla_kernel,
        out_shape=jax.ShapeDtypeStruct((n, hh * dv), BF16),
        grid=(batch, hh, nt),
        in_specs=[pl.BlockSpec((tb, dk), lambda b, h, t: (b * nt + t, q0 + h)),
                  pl.BlockSpec((tb, dk), lambda b, h, t: (b * nt + t, k0 + h)),
                  pl.BlockSpec((tb, dv), lambda b, h, t: (b * nt + t, v0 + h)),
                  pl.BlockSpec((tb, dv), lambda b, h, t: (b * nt + t, g0 + h)),
                  pl.BlockSpec((tb, LANES), lambda b, h, t: (b * nt + t, 0)),
                  pl.BlockSpec((LANES, dk), lambda b, h, t: (0, h)),
                  pl.BlockSpec((1, dk), lambda b, h, t: (0, h)),
                  pl.BlockSpec((1, dv), lambda b, h, t: (0, h)),
                  _const_spec(tri.shape), _const_spec(masks.shape)],
        out_specs=pl.BlockSpec((tb, dv), lambda b, h, t: (b * nt + t, h)),
        scratch_shapes=[pltpu.VMEM((dv, dk), F32)],
        compiler_params=_cparams(("parallel", "parallel", "arbitrary")),
        name="gla",
    )(proj, proj, proj, proj, lr, w2p, b.reshape(1, -1), nw.reshape(1, -1), tri, masks)


def _pack_bf16_pairs(a):
    c = a.shape[1] // 2
    bits = lax.bitcast_convert_type(a.astype(BF16).astype(F32), U32)
    return (bits[:, :c] >> 16) | (bits[:, c:] & U32(0xFFFF0000))


def _unpack_bf16_pairs(w):
    lo = lax.bitcast_convert_type(w << 16, F32)
    hi = lax.bitcast_convert_type(w & U32(0xFFFF0000), F32)
    return jnp.concatenate([lo, hi], axis=1)


def _out_proj_kernel(oh_ref, og_ref, res_ref, w_ref, nw_ref, *rest, with_router):
    if with_router:
        rw_ref, h_ref, hn_ref, lg_ref = rest
    else:
        h_ref, hn_ref = rest

    kh = oh_ref.shape[1]
    acc = jnp.dot(oh_ref[...], w_ref[0:kh, :], preferred_element_type=F32)
    acc = acc + jnp.dot(og_ref[...], w_ref[kh:, :], preferred_element_type=F32)
    h = res_ref[...] + acc
    h_ref[...] = h
    hn = _rms_rows(h, nw_ref[...])
    hn_ref[...] = _pack_bf16_pairs(hn)
    if with_router:
        hn_hi = hn.astype(BF16)
        hn_lo = (hn - hn_hi.astype(F32)).astype(BF16)
        rw2 = rw_ref[...]
        l2 = jnp.dot(hn_hi, rw2, preferred_element_type=F32)
        lg_ref[...] = (l2[:, :LANES] + l2[:, LANES:]
                       + jnp.dot(hn_lo, rw2[:, :LANES], preferred_element_type=F32))


def _out_proj(o_h, o_g, res, w_out, layer, nw, router_pad=None):
    n, d = res.shape
    kh, kg = o_h.shape[1], o_g.shape[1]
    with_router = router_pad is not None
    tm = 256 if with_router else 512
    row = lambda i: (i, 0)
    fixed = lambda i: (0, 0)
    in_specs = [pl.BlockSpec((tm, kh), row), pl.BlockSpec((tm, kg), row), pl.BlockSpec((tm, d), row),
                pl.BlockSpec((None, kh + kg, d), lambda i: (layer, 0, 0), pipeline_mode=pl.Buffered(1)),
                pl.BlockSpec((1, d), fixed)]
    args = [o_h, o_g, res, w_out, nw.reshape(1, d)]
    out_shape = [jax.ShapeDtypeStruct((n, d), F32), jax.ShapeDtypeStruct((n, d // 2), U32)]
    out_specs = [pl.BlockSpec((tm, d), row), pl.BlockSpec((tm, d // 2), row)]
    if with_router:
        in_specs.append(pl.BlockSpec((d, 2 * LANES), fixed))
        args.append(router_pad)
        out_shape.append(jax.ShapeDtypeStruct((n, LANES), F32))
        out_specs.append(pl.BlockSpec((tm, LANES), row))
    return pl.pallas_call(
        functools.partial(_out_proj_kernel, with_router=with_router),
        out_shape=out_shape,
        grid=(n // tm,),
        in_specs=in_specs,
        out_specs=out_specs,
        compiler_params=_cparams(("parallel",)),
        name="out_proj",
    )(*args)


def _dispatch_kernel(zs_ref, dest_ref, hn_ref, xs_ref, zero_ref, sem, *, tb, block):
    i = pl.program_id(0)

    def zero_copy(e):
        start = pl.multiple_of(zs_ref[e] * block, block)
        return pltpu.make_async_copy(zero_ref, xs_ref.at[pl.ds(start, block)], sem.at[1])

    @pl.when(i == 0)
    def _():
        zero_ref[...] = jnp.zeros_like(zero_ref)
        for e in range(zs_ref.shape[0]):
            @pl.when(zs_ref[e] >= 0)
            def _():
                zero_copy(e).start()
        for e in range(zs_ref.shape[0]):
            @pl.when(zs_ref[e] >= 0)
            def _():
                zero_copy(e).wait()

    def body(r, carry):
        src = hn_ref.at[pl.ds(r, 1)]
        for k in range(TOP_K):
            pltpu.make_async_copy(src, xs_ref.at[pl.ds(dest_ref[0, TOP_K * r + k], 1)],
                                  sem.at[0]).start(priority=k % 2)
        return carry

    lax.fori_loop(0, tb, body, 0, unroll=8)
    for k in range(TOP_K):
        pltpu.make_async_copy(hn_ref, xs_ref.at[pl.ds(0, tb)], sem.at[0]).wait()


def _dispatch(hn_packed, dest, zero_start, cap, *, block, tb=512):
    n, dw = hn_packed.shape
    tb = min(tb, n)
    grid_spec = pltpu.PrefetchScalarGridSpec(
        num_scalar_prefetch=1,
        grid=(n // tb,),
        in_specs=[pl.BlockSpec((None, 1, TOP_K * tb), lambda i, zs: (i, 0, 0), memory_space=pltpu.SMEM),
                  pl.BlockSpec((tb, dw), lambda i, zs: (i, 0))],
        out_specs=pl.BlockSpec(memory_space=pl.ANY),
        scratch_shapes=[pltpu.VMEM((block, dw), U32), pltpu.SemaphoreType.DMA((2,))],
    )
    return pl.pallas_call(
        functools.partial(_dispatch_kernel, tb=tb, block=block),
        out_shape=jax.ShapeDtypeStruct((cap, dw), U32),
        grid_spec=grid_spec,
        compiler_params=_cparams(("arbitrary",)),
        name="dispatch",
    )(zero_start, dest.reshape(n // tb, 1, TOP_K * tb), hn_packed)


def _ffn_kernel(be_ref, nused_ref, nrows_ref, x_ref, wg_ref, wu_ref, wd_ref, o_ref, xb_ref, acc_ref, *, row_chunk):
    i = pl.program_id(0)
    f = pl.program_id(1)
    used = i < nused_ref[0]
    n_chunks = x_ref.shape[0] // row_chunk

    @pl.when(f == 0)
    def _():
        acc_ref[...] = jnp.zeros_like(acc_ref)

    @pl.when(jnp.logical_and(f == 0, used))
    def _():
        def body(c, carry):
            rows = pl.ds(pl.multiple_of(c * row_chunk, row_chunk), row_chunk)
            xb_ref[rows, :] = _unpack_bf16_pairs(x_ref[rows, :]).astype(BF16)
            return carry
        lax.fori_loop(0, n_chunks, body, 0)

    live = (nrows_ref[i] + row_chunk - 1) // row_chunk
    for nc in range(1, n_chunks + 1):
        @pl.when(jnp.logical_and(used, live == nc))
        def _():
            rows = pl.ds(0, nc * row_chunk)
            x = xb_ref[rows, :]
            g = jnp.dot(x, wg_ref[0].astype(BF16), preferred_element_type=F32)
            u = jnp.dot(x, wu_ref[0].astype(BF16), preferred_element_type=F32)
            a = (g * _sigmoid(g) * u).astype(BF16)
            acc_ref[rows, :] += jnp.dot(a, wd_ref[0].astype(BF16), preferred_element_type=F32)

    @pl.when(f == pl.num_programs(1) - 1)
    def _():
        def body(c, carry):
            rows = pl.ds(pl.multiple_of(c * row_chunk, row_chunk), row_chunk)
            o_ref[rows, :] = _pack_bf16_pairs(acc_ref[rows, :])
            return carry
        lax.fori_loop(0, n_chunks, body, 0)


def _ffn(x_packed, wg, wu, wd, block_e, nused, nrows, *, tm=1024, tf=512):
    r, dw = x_packed.shape
    d = 2 * dw
    ff = wg.shape[2]
    nf = ff // tf

    def f_eff(i, f, nu):
        return jnp.where(i < nu[0], f, nf - 1)

    grid_spec = pltpu.PrefetchScalarGridSpec(
        num_scalar_prefetch=3,
        grid=(r // tm, nf),
        in_specs=[
            pl.BlockSpec((tm, dw), lambda i, f, be, nu, nr: (i, 0), pipeline_mode=pl.Buffered(1)),
            pl.BlockSpec((1, d, tf), lambda i, f, be, nu, nr: (be[i], 0, f_eff(i, f, nu))),
            pl.BlockSpec((1, d, tf), lambda i, f, be, nu, nr: (be[i], 0, f_eff(i, f, nu))),
            pl.BlockSpec((1, tf, d), lambda i, f, be, nu, nr: (be[i], f_eff(i, f, nu), 0)),
        ],
        out_specs=pl.BlockSpec((tm, dw), lambda i, f, be, nu, nr: (i, 0)),
        scratch_shapes=[pltpu.VMEM((tm, d), BF16), pltpu.VMEM((tm, d), F32)],
    )
    return pl.pallas_call(
        functools.partial(_ffn_kernel, row_chunk=256),
        out_shape=jax.ShapeDtypeStruct((r, dw), U32),
        grid_spec=grid_spec,
        compiler_params=_cparams(("parallel", "arbitrary")),
        name="ffn",
    )(block_e, nused, nrows, x_packed, wg, wu, wd)


def _ple_tail(h2, p_ref, nw_ref, wg_ref, wp_ref, fw_ref, o_ref, final):
    hn = _rms_rows(h2, nw_ref[...]).astype(BF16)
    gate = _sigmoid(jnp.dot(hn, wg_ref[...], preferred_element_type=F32))
    pp = jnp.dot(p_ref[...].astype(BF16), wp_ref[...], preferred_element_type=F32)
    h3 = h2 + gate * pp
    if final:
        h3 = _rms_rows(h3, fw_ref[...])
    o_ref[...] = h3


def _ple_dense_kernel(h_ref, y_ref, p_ref, nw_ref, wg_ref, wp_ref, fw_ref, o_ref, *, final):
    h2 = h_ref[...] + _unpack_bf16_pairs(y_ref[...])
    _ple_tail(h2, p_ref, nw_ref, wg_ref, wp_ref, fw_ref, o_ref, final)


def _ple_moe_kernel(dcur_ref, dnext_ref, h_ref, ys_ref, tw_ref, p_ref, nw_ref, wg_ref, wp_ref, fw_ref, o_ref,
                    gbuf_ref, sem, *, final):
    i = pl.program_id(0)
    n_steps = pl.num_programs(0)
    tm = h_ref.shape[0]
    slot = i % 2

    def issue(d_ref, s):
        def body(r, carry):
            for k in range(TOP_K):
                pltpu.make_async_copy(ys_ref.at[pl.ds(d_ref[0, TOP_K * r + k], 1)],
                                      gbuf_ref.at[s, k, pl.ds(r, 1)], sem.at[s]).start(priority=k % 2)
            return carry
        lax.fori_loop(0, tm, body, 0, unroll=8)

    @pl.when(i == 0)
    def _():
        issue(dcur_ref, 0)

    @pl.when(i + 1 < n_steps)
    def _():
        issue(dnext_ref, 1 - slot)

    for k in range(TOP_K):
        pltpu.make_async_copy(ys_ref.at[pl.ds(0, tm)], gbuf_ref.at[slot, k], sem.at[slot]).wait()
    tw = tw_ref[...]
    y = tw[:, 0:1] * _unpack_bf16_pairs(gbuf_ref[slot, 0])
    for k in range(1, TOP_K):
        y = y + tw[:, k:k + 1] * _unpack_bf16_pairs(gbuf_ref[slot, k])
    _ple_tail(h_ref[...] + y, p_ref, nw_ref, wg_ref, wp_ref, fw_ref, o_ref, final)


def _ple(h1, y_packed, p, layer, nw, w_gate, w_proj, final_w, *, final, dest=None, top_w=None, tm=512):
    n, d = h1.shape
    pd = p.shape[2]
    dw = d // 2
    n_steps = n // tm
    row = lambda i: (i, 0)
    fixed = lambda i: (0, 0)
    tail_specs = [pl.BlockSpec((None, tm, pd), lambda i: (layer, i, 0)),
                  pl.BlockSpec((1, d), fixed),
                  pl.BlockSpec((None, d, d), lambda i: (layer, 0, 0), pipeline_mode=pl.Buffered(1)),
                  pl.BlockSpec((None, pd, d), lambda i: (layer, 0, 0), pipeline_mode=pl.Buffered(1)),
                  pl.BlockSpec((1, d), fixed)]
    tail_args = [p, nw.reshape(1, d), w_gate, w_proj, final_w.reshape(1, d)]
    scratch = []
    if dest is None:
        kern = functools.partial(_ple_dense_kernel, final=final)
        in_specs = [pl.BlockSpec((tm, d), row), pl.BlockSpec((tm, dw), row)] + tail_specs
        args = [h1, y_packed] + tail_args
    else:
        kern = functools.partial(_ple_moe_kernel, final=final)
        dest3 = dest.reshape(n_steps, 1, TOP_K * tm)
        smem = lambda imap: pl.BlockSpec((None, 1, TOP_K * tm), imap, memory_space=pltpu.SMEM)
        in_specs = [smem(lambda i: (i, 0, 0)), smem(lambda i: (jnp.minimum(i + 1, n_steps - 1), 0, 0)),
                    pl.BlockSpec((tm, d), row), pl.BlockSpec(memory_space=pl.ANY),
                    pl.BlockSpec((tm, TOP_K), row)] + tail_specs
        args = [dest3, dest3, h1, y_packed, top_w] + tail_args
        scratch = scratch + [pltpu.VMEM((2, TOP_K, tm, dw), U32), pltpu.SemaphoreType.DMA((2,))]
    return pl.pallas_call(
        kern,
        out_shape=jax.ShapeDtypeStruct((n, d), F32),
        grid=(n_steps,),
        in_specs=in_specs,
        out_specs=pl.BlockSpec((tm, d), row),
        scratch_shapes=scratch,
        compiler_params=_cparams(("arbitrary",)),
        name="ple",
    )(*args)


def _route(logits, block):
    n = logits.shape[0]
    top_logit, top_e = lax.top_k(logits, TOP_K)
    top_w = jax.nn.softmax(top_logit, axis=-1)
    flat_e = top_e.reshape(-1)
    onehot = (flat_e[:, None] == jnp.arange(N_EXPERTS, dtype=flat_e.dtype)[None, :]).astype(jnp.int32)
    csum = jnp.cumsum(onehot, axis=0)
    counts = csum[-1]
    rank = jnp.sum((csum - onehot) * onehot, axis=1)
    padded = (counts + block - 1) // block * block
    padded_end = jnp.cumsum(padded)
    padded_start = padded_end - padded
    dest = (padded_start[flat_e] + rank).astype(jnp.int32)
    n_blocks = -(-(n * TOP_K) // block) + N_EXPERTS
    blk = jnp.arange(n_blocks, dtype=jnp.int32)
    block_e = jnp.minimum(jnp.sum((blk[:, None] * block >= padded_end[None, :]).astype(jnp.int32), axis=1),
                          N_EXPERTS - 1)
    nused = (padded_end[-1] // block).astype(jnp.int32).reshape(1)
    block_e = jnp.where(jnp.arange(n_blocks) < nused[0], block_e, block_e[jnp.maximum(nused[0] - 1, 0)])
    nrows = jnp.clip(counts[block_e] - (blk * block - padded_start[block_e]), 0, block).astype(jnp.int32)
    tail = nused[0] + jnp.arange(N_EXPERTS, dtype=jnp.int32)
    zero_start = jnp.concatenate([jnp.where(padded > 0, padded_end // block - 1, -1),
                                  jnp.where(tail < n_blocks, tail, -1)]).astype(jnp.int32)
    return dest.reshape(n, TOP_K), top_w, block_e, nused, nrows, zero_start, n_blocks * block


def kernel(x, p, norm_mix_w, w_in, hgrn_lb_logits, hgrn_norm_w, gla_gate_w2, gla_gate_b, gla_norm_w, w_out,
           norm_ffn_w, dense_w_gate, dense_w_up, dense_w_down, moe_router, moe_w_gate, moe_w_up, moe_w_down,
           norm_ple_w, ple_w_gate, ple_w_proj, final_norm_w):
    batch, seq, d = x.shape
    depth = w_in.shape[0]
    n = batch * seq
    moe_block = 1024

    lbs = jnp.cumsum(jax.nn.softmax(hgrn_lb_logits.astype(F32), axis=0), axis=0)
    lbs = lbs - lbs[0]
    tri_np, masks_np = _level_tables()
    tri = jnp.asarray(tri_np, BF16)
    masks = jnp.asarray(masks_np, BF16)

    h = x.reshape(n, d)
    p3 = p.reshape(depth, n, -1)
    moe_wg = moe_w_gate.reshape((-1,) + moe_w_gate.shape[2:])
    moe_wu = moe_w_up.reshape((-1,) + moe_w_up.shape[2:])
    moe_wd = moe_w_down.reshape((-1,) + moe_w_down.shape[2:])
    w_in_t = jnp.swapaxes(w_in, 1, 2)
    w_out_bf = w_out.astype(BF16)
    ple_wg_bf = ple_w_gate.astype(BF16)
    ple_wp_bf = ple_w_proj.astype(BF16)
    for i in range(depth):
        proj, lr = _in_proj(h, norm_mix_w[i], w_in_t, i)
        o_h = _hgrn(proj, lbs[i], hgrn_norm_w[i], tri, masks, batch=batch)
        w2p = jnp.zeros((LANES, gla_gate_w2.shape[2]), F32).at[:GLA_GATE_RANK].set(gla_gate_w2[i]).astype(BF16)
        o_g = _gla(proj, lr, w2p, gla_gate_b[i], gla_norm_w[i], tri, masks, batch=batch)
        j = i // 2
        last = i == depth - 1
        if i % 2 == 0:
            h1, hn = _out_proj(o_h, o_g, h, w_out_bf, i, norm_ffn_w[i])
            n_blk = n // moe_block
            y = _ffn(hn, dense_w_gate, dense_w_up, dense_w_down,
                     jnp.full((n_blk,), j, jnp.int32), jnp.full((1,), n_blk, jnp.int32),
                     jnp.full((n_blk,), moe_block, jnp.int32), tm=moe_block)
            h = _ple(h1, y, p3, i, norm_ple_w[i], ple_wg_bf, ple_wp_bf, final_norm_w, final=last)
        else:
            rw = jnp.zeros((d, LANES), F32).at[:, :N_EXPERTS].set(moe_router[j])
            rw_hi = rw.astype(BF16)
            router2 = jnp.concatenate([rw_hi, (rw - rw_hi.astype(F32)).astype(BF16)], axis=1)
            h1, hn, logits = _out_proj(o_h, o_g, h, w_out_bf, i, norm_ffn_w[i], router2)
            dest, top_w, block_e, nused, nrows, zero_start, cap = _route(logits[:, :N_EXPERTS], moe_block)
            xs = _dispatch(hn, dest, zero_start, cap, block=moe_block)
            ys = _ffn(xs, moe_wg, moe_wu, moe_wd, block_e + j * N_EXPERTS, nused, nrows, tm=moe_block)
            h = _ple(h1, ys, p3, i, norm_ple_w[i], ple_wg_bf, ple_wp_bf, final_norm_w, final=last,
                     dest=dest, top_w=top_w)
    return h.reshape(batch, seq, d)
```

```python
import functools

import numpy as np
import jax
import jax.numpy as jnp
from jax import lax
from jax.experimental import pallas as pl
from jax.experimental.pallas import tpu as pltpu

F32 = jnp.float32
BF16 = jnp.bfloat16
U32 = jnp.uint32

EPS = 1e-6
LOG2_E = 1.4426950408889634
HGRN_HEADS = 8
HGRN_HEAD_DIM = 128
GLA_HEADS = 4
GLA_HEAD_K = 128
GLA_HEAD_V = 256
GLA_GATE_RANK = 16
GLA_GATE_TEMP = 16.0
N_EXPERTS = 8
TOP_K = 2

LANES = 128
CHUNK = 128
N_LEVELS = 7
VMEM_LIMIT = 56 * 1024 * 1024

_N_MAIN = 6144
_COL_GG = 6144
_COL_GLR = 7168


def _cparams(sem):
    return pltpu.CompilerParams(dimension_semantics=sem, vmem_limit_bytes=VMEM_LIMIT)


def _sigmoid(x):
    return 1.0 / (1.0 + jnp.exp(-x))


def _rms_rows(x, w):
    ms = jnp.mean(x * x, axis=-1, keepdims=True)
    return x * lax.rsqrt(ms + EPS) * w


def _in_proj_kernel(x_ref, nw_ref, w_ref, wlr_ref, o_ref, lr_ref, xn_ref, *, row_chunk):
    tm = x_ref.shape[0]

    @pl.when(pl.program_id(1) == 0)
    def _():
        def body(c, carry):
            r = pl.multiple_of(c * row_chunk, row_chunk)
            xn_ref[pl.ds(r, row_chunk), :] = _rms_rows(x_ref[pl.ds(r, row_chunk), :], nw_ref[...]).astype(BF16)
            return carry
        lax.fori_loop(0, tm // row_chunk, body, 0)
        lr_ref[...] = lax.dot_general(xn_ref[...], wlr_ref[0].astype(BF16), _NT,
                                      preferred_element_type=F32).astype(lr_ref.dtype)

    o_ref[...] = lax.dot_general(xn_ref[...], w_ref[0].astype(BF16), _NT,
                                 preferred_element_type=F32).astype(o_ref.dtype)


def _in_proj(x, nw, w_t, layer, *, tm=1024, tn=1024):
    n, d = x.shape
    tm = min(tm, n)
    n_main = _N_MAIN // tn
    n_gg = (_COL_GLR - _COL_GG) // tn
    gg0 = _COL_GG + GLA_GATE_RANK

    def w_row(j):
        q, t = GLA_GATE_RANK, tn // GLA_GATE_RANK
        return q * jnp.where(j < n_main, j * t, gg0 // q + (j - n_main) * t)

    slab = lambda rows, imap: pl.BlockSpec((pl.Element(1), pl.Element(rows), pl.Element(d)), imap)
    return pl.pallas_call(
        functools.partial(_in_proj_kernel, row_chunk=128),
        out_shape=[jax.ShapeDtypeStruct((n, (n_main + n_gg) * tn), BF16), jax.ShapeDtypeStruct((n, LANES), BF16)],
        grid=(n // tm, n_main + n_gg),
        in_specs=[
            pl.BlockSpec((tm, d), lambda i, j: (i, 0)),
            pl.BlockSpec((1, d), lambda i, j: (0, 0)),
            slab(tn, lambda i, j: (layer, w_row(j), 0)),
            slab(LANES, lambda i, j: (layer, _COL_GG, 0)),
        ],
        out_specs=[pl.BlockSpec((tm, tn), lambda i, j: (i, j)), pl.BlockSpec((tm, LANES), lambda i, j: (i, 0))],
        scratch_shapes=[pltpu.VMEM((tm, d), BF16)],
        compiler_params=_cparams(("parallel", "arbitrary")),
        name="in_proj",
    )(x, nw.reshape(1, d), w_t, w_t)


def _level_tables():
    c = CHUNK
    idx = np.arange(c)
    masks = []
    s = c // 2
    while s >= 1:
        blk = idx // (2 * s)
        upper = (idx % (2 * s)) >= s
        masks.append(((blk[:, None] == blk[None, :]) & upper[:, None] & (~upper)[None, :]).astype(np.float32))
        s //= 2
    masks.append(np.eye(c, dtype=np.float32))
    tri = np.tril(np.ones((c, c), np.float32))
    return np.concatenate([tri, tri], axis=1), np.stack(masks)


def _neg_abs(d):
    return lax.bitcast_convert_type(lax.bitcast_convert_type(d, U32) | U32(0x80000000), F32)


def _level_exponents(g_cum):
    c, dk = g_cum.shape
    row = lax.broadcasted_iota(jnp.int32, (c, dk), 0)
    out = []
    s = c // 2
    while s >= 4:
        nb = c // (2 * s)
        mid = g_cum.reshape(nb, 2 * s, dk)[:, s - 1:s, :]
        mid = jnp.broadcast_to(mid, (nb, 2 * s, dk)).reshape(c, dk)
        out.append(_neg_abs(g_cum - mid))
        s //= 2
    up1 = pltpu.roll(g_cum, 1, 0)
    up2 = pltpu.roll(g_cum, 2, 0)
    dn1 = pltpu.roll(g_cum, c - 1, 0)
    r4 = row % 4
    mid2 = jnp.where(r4 == 0, dn1, jnp.where(r4 == 1, g_cum, jnp.where(r4 == 2, up1, up2)))
    out.append(_neg_abs(g_cum - mid2))
    mid1 = jnp.where(row % 2 == 1, up1, g_cum)
    out.append(_neg_abs(g_cum - mid1))
    return out


_NT = (((1,), (1,)), ((), ()))
_TN = (((0,), (0,)), ((), ()))


def _chunk_prep(g, tri_ref):
    g_hi = g.astype(BF16)
    g_lo = (g - g_hi.astype(F32)).astype(BF16)
    return jnp.dot(tri_ref[...], jnp.concatenate([g_hi, g_lo], axis=0), preferred_element_type=F32) * LOG2_E


def _chunk_local(q, k, v_bf, g_cum, masks_ref):
    c = CHUNK
    q_bf = q.astype(BF16)
    k_bf = k.astype(BF16)
    a = masks_ref[N_LEVELS] * lax.dot_general(q_bf, k_bf, _NT, preferred_element_type=F32).astype(BF16)
    for l, ex in enumerate(_level_exponents(g_cum)):
        e = jnp.exp2(ex).astype(BF16)
        s = lax.dot_general(q_bf * e, k_bf * e, _NT, preferred_element_type=F32)
        a = a + masks_ref[l] * s.astype(BF16)
    g_last = g_cum[c - 1:c, :]
    o_intra = jnp.dot(a, v_bf, preferred_element_type=F32)
    q_dec = (q * jnp.exp2(g_cum)).astype(BF16)
    k_dec = (k * jnp.exp2(g_last - g_cum)).astype(BF16)
    kv = lax.dot_general(v_bf, k_dec, _TN, preferred_element_type=F32)
    return o_intra, q_dec, kv, jnp.exp2(g_last)


def _head_norm(o, w):
    ms = jnp.mean(o * o, axis=-1, keepdims=True)
    return o * lax.rsqrt(ms + EPS) * w


def _scan_block(n_chunks, gates_fn, gate_fn, nw_ref, tri_ref, masks_ref, o_ref, st_ref):
    gates = [gates_fn(ci) for ci in range(n_chunks)]
    cums = [_chunk_prep(g, tri_ref) for (_, _, _, g) in gates]
    parts = [_chunk_local(q, k, v_bf, g_cum, masks_ref) for (q, k, v_bf, _), g_cum in zip(gates, cums)]
    st = st_ref[...]
    for ci, (o_intra, q_dec, kv, decay) in enumerate(parts):
        o = o_intra + lax.dot_general(q_dec, st.astype(BF16), _NT, preferred_element_type=F32)
        rows = pl.ds(ci * CHUNK, CHUNK)
        o_ref[rows, :] = (_head_norm(o, nw_ref[...]) * gate_fn(rows)).astype(o_ref.dtype)
        st = st * decay + kv
    st_ref[...] = st


def _hgrn_kernel(q_ref, f_ref, i_ref, g_ref, lb_ref, nw_ref, tri_ref, masks_ref, o_ref, st_ref):
    @pl.when(pl.program_id(2) == 0)
    def _():
        st_ref[...] = jnp.zeros_like(st_ref)

    lb = lb_ref[...]

    def local(ci):
        rows = pl.ds(ci * CHUNK, CHUNK)
        fgate = lb + (1.0 - lb) * _sigmoid(f_ref[rows, :].astype(F32))
        g = jnp.log(jnp.maximum(fgate, 1e-38))
        k = 1.0 - fgate
        hq = q_ref[rows, :].astype(F32)
        return hq * _sigmoid(hq), k, i_ref[rows, :], g

    def gate(rows):
        return _sigmoid(g_ref[rows, :].astype(F32))

    _scan_block(q_ref.shape[0] // CHUNK, local, gate, nw_ref, tri_ref, masks_ref, o_ref, st_ref)


def _gla_kernel(q_ref, k_ref, v_ref, g_ref, lr_ref, w2_ref, b_ref, nw_ref, tri_ref, masks_ref, o_ref, st_ref):
    @pl.when(pl.program_id(2) == 0)
    def _():
        st_ref[...] = jnp.zeros_like(st_ref)

    def local(ci):
        rows = pl.ds(ci * CHUNK, CHUNK)
        u = jnp.dot(lr_ref[rows, :], w2_ref[...], preferred_element_type=F32) + b_ref[...]
        g = (jnp.minimum(u, 0.0) - jnp.log(1.0 + jnp.exp(-jnp.abs(u)))) * (1.0 / GLA_GATE_TEMP)
        q = q_ref[rows, :].astype(F32) * (GLA_HEAD_K ** -0.5)
        return q, k_ref[rows, :].astype(F32), v_ref[rows, :], g

    def gate(rows):
        gg = g_ref[rows, :].astype(F32)
        return gg * _sigmoid(gg)

    _scan_block(q_ref.shape[0] // CHUNK, local, gate, nw_ref, tri_ref, masks_ref, o_ref, st_ref)


def _const_spec(shape):
    nd = len(shape)
    return pl.BlockSpec(shape, lambda b, h, t: (0,) * nd)


def _hgrn(proj, lb, nw, tri, masks, *, batch, tb=2048):
    n = proj.shape[0]
    nt = n // batch // tb
    dk = HGRN_HEAD_DIM
    hh = HGRN_HEADS

    def col(off):
        return pl.BlockSpec((tb, dk), lambda b, h, t: (b * nt + t, off + h))

    return pl.pallas_call(
        _hgrn_kernel,
        out_shape=jax.ShapeDtypeStruct((n, hh * dk), BF16),
        grid=(batch, hh, nt),
        in_specs=[col(0), col(hh), col(2 * hh), col(3 * hh),
                  pl.BlockSpec((1, dk), lambda b, h, t: (0, h)),
                  pl.BlockSpec((1, dk), lambda b, h, t: (0, h)),
                  _const_spec(tri.shape), _const_spec(masks.shape)],
        out_specs=pl.BlockSpec((tb, dk), lambda b, h, t: (b * nt + t, h)),
        scratch_shapes=[pltpu.VMEM((dk, dk), F32)],
        compiler_params=_cparams(("parallel", "parallel", "arbitrary")),
        name="hgrn",
    )(proj, proj, proj, proj, lb.reshape(1, -1), nw.reshape(1, -1), tri, masks)


def _gla(proj, lr, w2p, b, nw, tri, masks, *, batch, tb=1024):
    n = proj.shape[0]
    nt = n // batch // tb
    dk, dv, hh = GLA_HEAD_K, GLA_HEAD_V, GLA_HEADS
    q0 = 4096 // dk
    k0 = q0 + hh
    v0 = 5120 // dv
    g0 = _COL_GG // dv
    return pl.pallas_call(
        _gla_kernel,
        out_shape=jax.ShapeDtypeStruct((n, hh * dv), BF16),
        grid=(batch, hh, nt),
        in_specs=[pl.BlockSpec((tb, dk), lambda b, h, t: (b * nt + t, q0 + h)),
                  pl.BlockSpec((tb, dk), lambda b, h, t: (b * nt + t, k0 + h)),
                  pl.BlockSpec((tb, dv), lambda b, h, t: (b * nt + t, v0 + h)),
                  pl.BlockSpec((tb, dv), lambda b, h, t: (b * nt + t, g0 + h)),
                  pl.BlockSpec((tb, LANES), lambda b, h, t: (b * nt + t, 0)),
                  pl.BlockSpec((LANES, dk), lambda b, h, t: (0, h)),
                  pl.BlockSpec((1, dk), lambda b, h, t: (0, h)),
                  pl.BlockSpec((1, dv), lambda b, h, t: (0, h)),
                  _const_spec(tri.shape), _const_spec(masks.shape)],
        out_specs=pl.BlockSpec((tb, dv), lambda b, h, t: (b * nt + t, h)),
        scratch_shapes=[pltpu.VMEM((dv, dk), F32)],
        compiler_params=_cparams(("parallel", "parallel", "arbitrary")),
        name="gla",
    )(proj, proj, proj, proj, lr, w2p, b.reshape(1, -1), nw.reshape(1, -1), tri, masks)


def _out_proj_kernel(oh_ref, og_ref, res_ref, w_ref, nw_ref, *rest, with_router):
    if with_router:
        rw_ref, h_ref, hn_ref, lg_ref = rest
    else:
        h_ref, hn_ref = rest

    kh = oh_ref.shape[1]
    acc = jnp.dot(oh_ref[...], w_ref[0:kh, :], preferred_element_type=F32)
    acc = acc + jnp.dot(og_ref[...], w_ref[kh:, :], preferred_element_type=F32)
    h = res_ref[...] + acc
    h_ref[...] = h
    hn = _rms_rows(h, nw_ref[...])
    hn_ref[...] = hn
    if with_router:
        hn_hi = hn.astype(BF16)
        hn_lo = (hn - hn_hi.astype(F32)).astype(BF16)
        rw2 = rw_ref[...]
        l2 = jnp.dot(hn_hi, rw2, preferred_element_type=F32)
        lg_ref[...] = (l2[:, :LANES] + l2[:, LANES:]
                       + jnp.dot(hn_lo, rw2[:, :LANES], preferred_element_type=F32))


def _out_proj(o_h, o_g, res, w_out, layer, nw, router_pad=None):
    n, d = res.shape
    kh, kg = o_h.shape[1], o_g.shape[1]
    with_router = router_pad is not None
    tm = 256 if with_router else 512
    row = lambda i: (i, 0)
    fixed = lambda i: (0, 0)
    in_specs = [pl.BlockSpec((tm, kh), row), pl.BlockSpec((tm, kg), row), pl.BlockSpec((tm, d), row),
                pl.BlockSpec((None, kh + kg, d), lambda i: (layer, 0, 0), pipeline_mode=pl.Buffered(1)),
                pl.BlockSpec((1, d), fixed)]
    args = [o_h, o_g, res, w_out, nw.reshape(1, d)]
    out_shape = [jax.ShapeDtypeStruct((n, d), F32), jax.ShapeDtypeStruct((n, d), F32)]
    out_specs = [pl.BlockSpec((tm, d), row), pl.BlockSpec((tm, d), row)]
    if with_router:
        in_specs.append(pl.BlockSpec((d, 2 * LANES), fixed))
        args.append(router_pad)
        out_shape.append(jax.ShapeDtypeStruct((n, LANES), F32))
        out_specs.append(pl.BlockSpec((tm, LANES), row))
    return pl.pallas_call(
        functools.partial(_out_proj_kernel, with_router=with_router),
        out_shape=out_shape,
        grid=(n // tm,),
        in_specs=in_specs,
        out_specs=out_specs,
        compiler_params=_cparams(("parallel",)),
        name="out_proj",
    )(*args)


def _dispatch_kernel(zs_ref, dest_ref, hn_ref, xs_ref, zero_ref, sem, *, tb, block):
    i = pl.program_id(0)

    def zero_copy(e):
        start = pl.multiple_of(zs_ref[e] * block, block)
        return pltpu.make_async_copy(zero_ref, xs_ref.at[pl.ds(start, block)], sem.at[1])

    @pl.when(i == 0)
    def _():
        zero_ref[...] = jnp.zeros_like(zero_ref)
        for e in range(zs_ref.shape[0]):
            @pl.when(zs_ref[e] >= 0)
            def _():
                zero_copy(e).start()
        for e in range(zs_ref.shape[0]):
            @pl.when(zs_ref[e] >= 0)
            def _():
                zero_copy(e).wait()

    def body(r, carry):
        src = hn_ref.at[pl.ds(r, 1)]
        for k in range(TOP_K):
            pltpu.make_async_copy(src, xs_ref.at[pl.ds(dest_ref[0, TOP_K * r + k], 1)],
                                  sem.at[0]).start(priority=k % 2)
        return carry

    lax.fori_loop(0, tb, body, 0, unroll=8)
    for k in range(TOP_K):
        pltpu.make_async_copy(hn_ref, xs_ref.at[pl.ds(0, tb)], sem.at[0]).wait()


def _dispatch(hn, dest, zero_start, cap, *, block, tb=512):
    n, dw = hn.shape
    tb = min(tb, n)
    grid_spec = pltpu.PrefetchScalarGridSpec(
        num_scalar_prefetch=1,
        grid=(n // tb,),
        in_specs=[pl.BlockSpec((None, 1, TOP_K * tb), lambda i, zs: (i, 0, 0), memory_space=pltpu.SMEM),
                  pl.BlockSpec((tb, dw), lambda i, zs: (i, 0))],
        out_specs=pl.BlockSpec(memory_space=pl.ANY),
        scratch_shapes=[pltpu.VMEM((block, dw), hn.dtype), pltpu.SemaphoreType.DMA((2,))],
    )
    return pl.pallas_call(
        functools.partial(_dispatch_kernel, tb=tb, block=block),
        out_shape=jax.ShapeDtypeStruct((cap, dw), hn.dtype),
        grid_spec=grid_spec,
        compiler_params=_cparams(("arbitrary",)),
        name="dispatch",
    )(zero_start, dest.reshape(n // tb, 1, TOP_K * tb), hn)


def _ffn_kernel(be_ref, nused_ref, nrows_ref, x_ref, wg_ref, wu_ref, wd_ref, o_ref, xb_ref, *, row_chunk):
    i = pl.program_id(0)
    f = pl.program_id(1)
    used = i < nused_ref[0]
    n_chunks = x_ref.shape[0] // row_chunk

    @pl.when(f == 0)
    def _():
        o_ref[...] = jnp.zeros_like(o_ref)

    @pl.when(jnp.logical_and(f == 0, used))
    def _():
        def body(c, carry):
            rows = pl.ds(pl.multiple_of(c * row_chunk, row_chunk), row_chunk)
            xb_ref[rows, :] = x_ref[rows, :].astype(BF16)
            return carry
        lax.fori_loop(0, n_chunks, body, 0)

    live = (nrows_ref[i] + row_chunk - 1) // row_chunk
    for nc in range(1, n_chunks + 1):
        @pl.when(jnp.logical_and(used, live == nc))
        def _():
            rows = pl.ds(0, nc * row_chunk)
            x = xb_ref[rows, :]
            g = jnp.dot(x, wg_ref[0].astype(BF16), preferred_element_type=F32)
            u = jnp.dot(x, wu_ref[0].astype(BF16), preferred_element_type=F32)
            a = (g * _sigmoid(g) * u).astype(BF16)
            o_ref[rows, :] += jnp.dot(a, wd_ref[0].astype(BF16), preferred_element_type=F32)


def _ffn(x, wg, wu, wd, block_e, nused, nrows, *, tm=1024, tf=256):
    r, d = x.shape
    ff = wg.shape[2]
    nf = ff // tf

    def f_eff(i, f, nu):
        return jnp.where(i < nu[0], f, nf - 1)

    grid_spec = pltpu.PrefetchScalarGridSpec(
        num_scalar_prefetch=3,
        grid=(r // tm, nf),
        in_specs=[
            pl.BlockSpec((tm, d), lambda i, f, be, nu, nr: (i, 0), pipeline_mode=pl.Buffered(1)),
            pl.BlockSpec((1, d, tf), lambda i, f, be, nu, nr: (be[i], 0, f_eff(i, f, nu))),
            pl.BlockSpec((1, d, tf), lambda i, f, be, nu, nr: (be[i], 0, f_eff(i, f, nu))),
            pl.BlockSpec((1, tf, d), lambda i, f, be, nu, nr: (be[i], f_eff(i, f, nu), 0)),
        ],
        out_specs=pl.BlockSpec((tm, d), lambda i, f, be, nu, nr: (i, 0)),
        scratch_shapes=[pltpu.VMEM((tm, d), BF16)],
    )
    return pl.pallas_call(
        functools.partial(_ffn_kernel, row_chunk=256),
        out_shape=jax.ShapeDtypeStruct((r, d), F32),
        grid_spec=grid_spec,
        compiler_params=_cparams(("parallel", "arbitrary")),
        name="ffn",
    )(block_e, nused, nrows, x, wg, wu, wd)


def _ple_tail(h2, p_ref, nw_ref, wg_ref, wp_ref, fw_ref, o_ref, final):
    hn = _rms_rows(h2, nw_ref[...]).astype(BF16)
    gate = _sigmoid(jnp.dot(hn, wg_ref[...], preferred_element_type=F32))
    pp = jnp.dot(p_ref[...].astype(BF16), wp_ref[...], preferred_element_type=F32)
    h3 = h2 + gate * pp
    if final:
        h3 = _rms_rows(h3, fw_ref[...])
    o_ref[...] = h3


def _ple_dense_kernel(h_ref, y_ref, p_ref, nw_ref, wg_ref, wp_ref, fw_ref, o_ref, *, final):
    h2 = h_ref[...] + y_ref[...]
    _ple_tail(h2, p_ref, nw_ref, wg_ref, wp_ref, fw_ref, o_ref, final)


def _ple_moe_kernel(dcur_ref, dnext_ref, h_ref, ys_ref, tw_ref, p_ref, nw_ref, wg_ref, wp_ref, fw_ref, o_ref,
                    gbuf_ref, sem, *, final):
    i = pl.program_id(0)
    n_steps = pl.num_programs(0)
    tm = h_ref.shape[0]
    slot = i % 2

    def issue(d_ref, s):
        def body(r, carry):
            for k in range(TOP_K):
                pltpu.make_async_copy(ys_ref.at[pl.ds(d_ref[0, TOP_K * r + k], 1)],
                                      gbuf_ref.at[s, k, pl.ds(r, 1)], sem.at[s]).start(priority=k % 2)
            return carry
        lax.fori_loop(0, tm, body, 0, unroll=8)

    @pl.when(i == 0)
    def _():
        issue(dcur_ref, 0)

    @pl.when(i + 1 < n_steps)
    def _():
        issue(dnext_ref, 1 - slot)

    for k in range(TOP_K):
        pltpu.make_async_copy(ys_ref.at[pl.ds(0, tm)], gbuf_ref.at[slot, k], sem.at[slot]).wait()
    tw = tw_ref[...]
    y = tw[:, 0:1] * gbuf_ref[slot, 0]
    for k in range(1, TOP_K):
        y = y + tw[:, k:k + 1] * gbuf_ref[slot, k]
    _ple_tail(h_ref[...] + y, p_ref, nw_ref, wg_ref, wp_ref, fw_ref, o_ref, final)


def _ple(h1, y, p, layer, nw, w_gate, w_proj, final_w, *, final, dest=None, top_w=None):
    n, d = h1.shape
    pd = p.shape[2]
    tm = 512 if dest is None else 256
    n_steps = n // tm
    row = lambda i: (i, 0)
    fixed = lambda i: (0, 0)
    tail_specs = [pl.BlockSpec((None, tm, pd), lambda i: (layer, i, 0)),
                  pl.BlockSpec((1, d), fixed),
                  pl.BlockSpec((None, d, d), lambda i: (layer, 0, 0), pipeline_mode=pl.Buffered(1)),
                  pl.BlockSpec((None, pd, d), lambda i: (layer, 0, 0), pipeline_mode=pl.Buffered(1)),
                  pl.BlockSpec((1, d), fixed)]
    tail_args = [p, nw.reshape(1, d), w_gate, w_proj, final_w.reshape(1, d)]
    scratch = []
    if dest is None:
        kern = functools.partial(_ple_dense_kernel, final=final)
        in_specs = [pl.BlockSpec((tm, d), row), pl.BlockSpec((tm, d), row)] + tail_specs
        args = [h1, y] + tail_args
    else:
        kern = functools.partial(_ple_moe_kernel, final=final)
        dest3 = dest.reshape(n_steps, 1, TOP_K * tm)
        smem = lambda imap: pl.BlockSpec((None, 1, TOP_K * tm), imap, memory_space=pltpu.SMEM)
        in_specs = [smem(lambda i: (i, 0, 0)), smem(lambda i: (jnp.minimum(i + 1, n_steps - 1), 0, 0)),
                    pl.BlockSpec((tm, d), row), pl.BlockSpec(memory_space=pl.ANY),
                    pl.BlockSpec((tm, TOP_K), row)] + tail_specs
        args = [dest3, dest3, h1, y, top_w] + tail_args
        scratch = scratch + [pltpu.VMEM((2, TOP_K, tm, d), F32), pltpu.SemaphoreType.DMA((2,))]
    return pl.pallas_call(
        kern,
        out_shape=jax.ShapeDtypeStruct((n, d), F32),
        grid=(n_steps,),
        in_specs=in_specs,
        out_specs=pl.BlockSpec((tm, d), row),
        scratch_shapes=scratch,
        compiler_params=_cparams(("arbitrary",)),
        name="ple",
    )(*args)


def _route(logits, block):
    n = logits.shape[0]
    top_logit, top_e = lax.top_k(logits, TOP_K)
    top_w = jax.nn.softmax(top_logit, axis=-1)
    flat_e = top_e.reshape(-1)
    onehot = (flat_e[:, None] == jnp.arange(N_EXPERTS, dtype=flat_e.dtype)[None, :]).astype(jnp.int32)
    csum = jnp.cumsum(onehot, axis=0)
    counts = csum[-1]
    rank = jnp.sum((csum - onehot) * onehot, axis=1)
    padded = (counts + block - 1) // block * block
    padded_end = jnp.cumsum(padded)
    padded_start = padded_end - padded
    dest = (padded_start[flat_e] + rank).astype(jnp.int32)
    n_blocks = -(-(n * TOP_K) // block) + N_EXPERTS
    blk = jnp.arange(n_blocks, dtype=jnp.int32)
    block_e = jnp.minimum(jnp.sum((blk[:, None] * block >= padded_end[None, :]).astype(jnp.int32), axis=1),
                          N_EXPERTS - 1)
    nused = (padded_end[-1] // block).astype(jnp.int32).reshape(1)
    block_e = jnp.where(jnp.arange(n_blocks) < nused[0], block_e, block_e[jnp.maximum(nused[0] - 1, 0)])
    nrows = jnp.clip(counts[block_e] - (blk * block - padded_start[block_e]), 0, block).astype(jnp.int32)
    tail = nused[0] + jnp.arange(N_EXPERTS, dtype=jnp.int32)
    zero_start = jnp.concatenate([jnp.where(padded > 0, padded_end // block - 1, -1),
                                  jnp.where(tail < n_blocks, tail, -1)]).astype(jnp.int32)
    return dest.reshape(n, TOP_K), top_w, block_e, nused, nrows, zero_start, n_blocks * block


def kernel(x, p, norm_mix_w, w_in, hgrn_lb_logits, hgrn_norm_w, gla_gate_w2, gla_gate_b, gla_norm_w, w_out,
           norm_ffn_w, dense_w_gate, dense_w_up, dense_w_down, moe_router, moe_w_gate, moe_w_up, moe_w_down,
           norm_ple_w, ple_w_gate, ple_w_proj, final_norm_w):
    batch, seq, d = x.shape
    depth = w_in.shape[0]
    n = batch * seq
    moe_block = 1024

    lbs = jnp.cumsum(jax.nn.softmax(hgrn_lb_logits.astype(F32), axis=0), axis=0)
    lbs = lbs - lbs[0]
    tri_np, masks_np = _level_tables()
    tri = jnp.asarray(tri_np, BF16)
    masks = jnp.asarray(masks_np, BF16)

    h = x.reshape(n, d)
    p3 = p.reshape(depth, n, -1)
    moe_wg = moe_w_gate.reshape((-1,) + moe_w_gate.shape[2:])
    moe_wu = moe_w_up.reshape((-1,) + moe_w_up.shape[2:])
    moe_wd = moe_w_down.reshape((-1,) + moe_w_down.shape[2:])
    w_in_t = jnp.swapaxes(w_in, 1, 2)
    w_out_bf = w_out.astype(BF16)
    ple_wg_bf = ple_w_gate.astype(BF16)
    ple_wp_bf = ple_w_proj.astype(BF16)
    for i in range(depth):
        proj, lr = _in_proj(h, norm_mix_w[i], w_in_t, i)
        o_h = _hgrn(proj, lbs[i], hgrn_norm_w[i], tri, masks, batch=batch)
        w2p = jnp.zeros((LANES, gla_gate_w2.shape[2]), F32).at[:GLA_GATE_RANK].set(gla_gate_w2[i]).astype(BF16)
        o_g = _gla(proj, lr, w2p, gla_gate_b[i], gla_norm_w[i], tri, masks, batch=batch)
        j = i // 2
        last = i == depth - 1
        if i % 2 == 0:
            h1, hn = _out_proj(o_h, o_g, h, w_out_bf, i, norm_ffn_w[i])
            n_blk = n // moe_block
            y = _ffn(hn, dense_w_gate, dense_w_up, dense_w_down,
                     jnp.full((n_blk,), j, jnp.int32), jnp.full((1,), n_blk, jnp.int32),
                     jnp.full((n_blk,), moe_block, jnp.int32), tm=moe_block)
            h = _ple(h1, y, p3, i, norm_ple_w[i], ple_wg_bf, ple_wp_bf, final_norm_w, final=last)
        else:
            rw = jnp.zeros((d, LANES), F32).at[:, :N_EXPERTS].set(moe_router[j])
            rw_hi = rw.astype(BF16)
            router2 = jnp.concatenate([rw_hi, (rw - rw_hi.astype(F32)).astype(BF16)], axis=1)
            h1, hn, logits = _out_proj(o_h, o_g, h, w_out_bf, i, norm_ffn_w[i], router2)
            dest, top_w, block_e, nused, nrows, zero_start, cap = _route(logits[:, :N_EXPERTS], moe_block)
            xs = _dispatch(hn, dest, zero_start, cap, block=moe_block)
            ys = _ffn(xs, moe_wg, moe_wu, moe_wd, block_e + j * N_EXPERTS, nused, nrows, tm=moe_block)
            h = _ple(h1, ys, p3, i, norm_ple_w[i], ple_wg_bf, ple_wp_bf, final_norm_w, final=last,
                     dest=dest, top_w=top_w)
    return h.reshape(batch, seq, d)
```

```python
import functools

import numpy as np
import jax
import jax.numpy as jnp
from jax import lax
from jax.experimental import pallas as pl
from jax.experimental.pallas import tpu as pltpu

F32 = jnp.float32
BF16 = jnp.bfloat16
U32 = jnp.uint32

EPS = 1e-6
LOG2_E = 1.4426950408889634
HGRN_HEADS = 8
HGRN_HEAD_DIM = 128
GLA_HEADS = 4
GLA_HEAD_K = 128
GLA_HEAD_V = 256
GLA_GATE_RANK = 16
GLA_GATE_TEMP = 16.0
N_EXPERTS = 8
TOP_K = 2

LANES = 128
CHUNK = 128
N_LEVELS = 7
VMEM_LIMIT = 56 * 1024 * 1024

_N_MAIN = 6144
_COL_GG = 6144
_COL_GLR = 7168


def _cparams(sem):
    return pltpu.CompilerParams(dimension_semantics=sem, vmem_limit_bytes=VMEM_LIMIT)


def _sigmoid(x):
    return 1.0 / (1.0 + jnp.exp(-x))


def _rms_rows(x, w):
    ms = jnp.mean(x * x, axis=-1, keepdims=True)
    return x * lax.rsqrt(ms + EPS) * w


def _in_proj_kernel(x_ref, nw_ref, w_ref, wlr_ref, o_ref, lr_ref, xn_ref, *, row_chunk):
    tm = x_ref.shape[0]

    @pl.when(pl.program_id(1) == 0)
    def _():
        def body(c, carry):
            r = pl.multiple_of(c * row_chunk, row_chunk)
            xn_ref[pl.ds(r, row_chunk), :] = _rms_rows(x_ref[pl.ds(r, row_chunk), :], nw_ref[...]).astype(BF16)
            return carry
        lax.fori_loop(0, tm // row_chunk, body, 0)
        lr_ref[...] = lax.dot_general(xn_ref[...], wlr_ref[0].astype(BF16), _NT,
                                      preferred_element_type=F32).astype(lr_ref.dtype)

    o_ref[...] = lax.dot_general(xn_ref[...], w_ref[0].astype(BF16), _NT,
                                 preferred_element_type=F32).astype(o_ref.dtype)


def _in_proj(x, nw, w_t, layer, *, tm=1024, tn=1024):
    n, d = x.shape
    tm = min(tm, n)
    n_main = _N_MAIN // tn
    n_gg = (_COL_GLR - _COL_GG) // tn
    gg0 = _COL_GG + GLA_GATE_RANK

    def w_row(j):
        q, t = GLA_GATE_RANK, tn // GLA_GATE_RANK
        return q * jnp.where(j < n_main, j * t, gg0 // q + (j - n_main) * t)

    slab = lambda rows, imap: pl.BlockSpec((pl.Element(1), pl.Element(rows), pl.Element(d)), imap)
    return pl.pallas_call(
        functools.partial(_in_proj_kernel, row_chunk=128),
        out_shape=[jax.ShapeDtypeStruct((n, (n_main + n_gg) * tn), BF16), jax.ShapeDtypeStruct((n, LANES), BF16)],
        grid=(n // tm, n_main + n_gg),
        in_specs=[
            pl.BlockSpec((tm, d), lambda i, j: (i, 0)),
            pl.BlockSpec((1, d), lambda i, j: (0, 0)),
            slab(tn, lambda i, j: (layer, w_row(j), 0)),
            slab(LANES, lambda i, j: (layer, _COL_GG, 0)),
        ],
        out_specs=[pl.BlockSpec((tm, tn), lambda i, j: (i, j)), pl.BlockSpec((tm, LANES), lambda i, j: (i, 0))],
        scratch_shapes=[pltpu.VMEM((tm, d), BF16)],
        compiler_params=_cparams(("parallel", "arbitrary")),
        name="in_proj",
    )(x, nw.reshape(1, d), w_t, w_t)


def _level_tables():
    c = CHUNK
    idx = np.arange(c)
    masks = []
    s = c // 2
    while s >= 1:
        blk = idx // (2 * s)
        upper = (idx % (2 * s)) >= s
        masks.append(((blk[:, None] == blk[None, :]) & upper[:, None] & (~upper)[None, :]).astype(np.float32))
        s //= 2
    masks.append(np.eye(c, dtype=np.float32))
    tri = np.tril(np.ones((c, c), np.float32))
    return np.concatenate([tri, tri], axis=1), np.stack(masks)


def _neg_abs(d):
    return lax.bitcast_convert_type(lax.bitcast_convert_type(d, U32) | U32(0x80000000), F32)


def _level_exponents(g_cum):
    c, dk = g_cum.shape
    row = lax.broadcasted_iota(jnp.int32, (c, dk), 0)
    out = []
    s = c // 2
    while s >= 4:
        nb = c // (2 * s)
        mid = g_cum.reshape(nb, 2 * s, dk)[:, s - 1:s, :]
        mid = jnp.broadcast_to(mid, (nb, 2 * s, dk)).reshape(c, dk)
        out.append(_neg_abs(g_cum - mid))
        s //= 2
    up1 = pltpu.roll(g_cum, 1, 0)
    up2 = pltpu.roll(g_cum, 2, 0)
    dn1 = pltpu.roll(g_cum, c - 1, 0)
    r4 = row % 4
    mid2 = jnp.where(r4 == 0, dn1, jnp.where(r4 == 1, g_cum, jnp.where(r4 == 2, up1, up2)))
    out.append(_neg_abs(g_cum - mid2))
    mid1 = jnp.where(row % 2 == 1, up1, g_cum)
    out.append(_neg_abs(g_cum - mid1))
    return out


_NT = (((1,), (1,)), ((), ()))
_TN = (((0,), (0,)), ((), ()))


def _chunk_prep(g, tri_ref):
    g_hi = g.astype(BF16)
    g_lo = (g - g_hi.astype(F32)).astype(BF16)
    return jnp.dot(tri_ref[...], jnp.concatenate([g_hi, g_lo], axis=0), preferred_element_type=F32) * LOG2_E


def _chunk_local(q, k, v_bf, g_cum, masks_ref):
    c = CHUNK
    q_bf = q.astype(BF16)
    k_bf = k.astype(BF16)
    a = masks_ref[N_LEVELS] * lax.dot_general(q_bf, k_bf, _NT, preferred_element_type=F32).astype(BF16)
    for l, ex in enumerate(_level_exponents(g_cum)):
        e = jnp.exp2(ex).astype(BF16)
        s = lax.dot_general(q_bf * e, k_bf * e, _NT, preferred_element_type=F32)
        a = a + masks_ref[l] * s.astype(BF16)
    g_last = g_cum[c - 1:c, :]
    o_intra = jnp.dot(a, v_bf, preferred_element_type=F32)
    q_dec = (q * jnp.exp2(g_cum)).astype(BF16)
    k_dec = (k * jnp.exp2(g_last - g_cum)).astype(BF16)
    kv = lax.dot_general(v_bf, k_dec, _TN, preferred_element_type=F32)
    return o_intra, q_dec, kv, jnp.exp2(g_last)


def _head_norm(o, w):
    ms = jnp.mean(o * o, axis=-1, keepdims=True)
    return o * lax.rsqrt(ms + EPS) * w


def _scan_block(n_chunks, gates_fn, gate_fn, nw_ref, tri_ref, masks_ref, o_ref, st_ref):
    gates = [gates_fn(ci) for ci in range(n_chunks)]
    cums = [_chunk_prep(g, tri_ref) for (_, _, _, g) in gates]
    parts = [_chunk_local(q, k, v_bf, g_cum, masks_ref) for (q, k, v_bf, _), g_cum in zip(gates, cums)]
    st = st_ref[...]
    for ci, (o_intra, q_dec, kv, decay) in enumerate(parts):
        o = o_intra + lax.dot_general(q_dec, st.astype(BF16), _NT, preferred_element_type=F32)
        rows = pl.ds(ci * CHUNK, CHUNK)
        o_ref[rows, :] = (_head_norm(o, nw_ref[...]) * gate_fn(rows)).astype(o_ref.dtype)
        st = st * decay + kv
    st_ref[...] = st


def _hgrn_kernel(q_ref, f_ref, i_ref, g_ref, lb_ref, nw_ref, tri_ref, masks_ref, o_ref, st_ref):
    @pl.when(pl.program_id(2) == 0)
    def _():
        st_ref[...] = jnp.zeros_like(st_ref)

    lb = lb_ref[...]

    def local(ci):
        rows = pl.ds(ci * CHUNK, CHUNK)
        fgate = lb + (1.0 - lb) * _sigmoid(f_ref[rows, :].astype(F32))
        g = jnp.log(jnp.maximum(fgate, 1e-38))
        k = 1.0 - fgate
        hq = q_ref[rows, :].astype(F32)
        return hq * _sigmoid(hq), k, i_ref[rows, :], g

    def gate(rows):
        return _sigmoid(g_ref[rows, :].astype(F32))

    _scan_block(q_ref.shape[0] // CHUNK, local, gate, nw_ref, tri_ref, masks_ref, o_ref, st_ref)


def _gla_kernel(q_ref, k_ref, v_ref, g_ref, lr_ref, w2_ref, b_ref, nw_ref, tri_ref, masks_ref, o_ref, st_ref):
    @pl.when(pl.program_id(2) == 0)
    def _():
        st_ref[...] = jnp.zeros_like(st_ref)

    def local(ci):
        rows = pl.ds(ci * CHUNK, CHUNK)
        u = jnp.dot(lr_ref[rows, :], w2_ref[...], preferred_element_type=F32) + b_ref[...]
        g = (jnp.minimum(u, 0.0) - jnp.log(1.0 + jnp.exp(-jnp.abs(u)))) * (1.0 / GLA_GATE_TEMP)
        q = q_ref[rows, :].astype(F32) * (GLA_HEAD_K ** -0.5)
        return q, k_ref[rows, :].astype(F32), v_ref[rows, :], g

    def gate(rows):
        gg = g_ref[rows, :].astype(F32)
        return gg * _sigmoid(gg)

    _scan_block(q_ref.shape[0] // CHUNK, local, gate, nw_ref, tri_ref, masks_ref, o_ref, st_ref)


def _const_spec(shape):
    nd = len(shape)
    return pl.BlockSpec(shape, lambda b, h, t: (0,) * nd)


def _hgrn(proj, lb, nw, tri, masks, *, batch, tb=2048):
    n = proj.shape[0]
    nt = n // batch // tb
    dk = HGRN_HEAD_DIM
    hh = HGRN_HEADS

    def col(off):
        return pl.BlockSpec((tb, dk), lambda b, h, t: (b * nt + t, off + h))

    return pl.pallas_call(
        _hgrn_kernel,
        out_shape=jax.ShapeDtypeStruct((n, hh * dk), BF16),
        grid=(batch, hh, nt),
        in_specs=[col(0), col(hh), col(2 * hh), col(3 * hh),
                  pl.BlockSpec((1, dk), lambda b, h, t: (0, h)),
                  pl.BlockSpec((1, dk), lambda b, h, t: (0, h)),
                  _const_spec(tri.shape), _const_spec(masks.shape)],
        out_specs=pl.BlockSpec((tb, dk), lambda b, h, t: (b * nt + t, h)),
        scratch_shapes=[pltpu.VMEM((dk, dk), F32)],
        compiler_params=_cparams(("parallel", "parallel", "arbitrary")),
        name="hgrn",
    )(proj, proj, proj, proj, lb.reshape(1, -1), nw.reshape(1, -1), tri, masks)


def _gla(proj, lr, w2p, b, nw, tri, masks, *, batch, tb=1024):
    n = proj.shape[0]
    nt = n // batch // tb
    dk, dv, hh = GLA_HEAD_K, GLA_HEAD_V, GLA_HEADS
    q0 = 4096 // dk
    k0 = q0 + hh
    v0 = 5120 // dv
    g0 = _COL_GG // dv
    return pl.pallas_call(
        _gla_kernel,
        out_shape=jax.ShapeDtypeStruct((n, hh * dv), BF16),
        grid=(batch, hh, nt),
        in_specs=[pl.BlockSpec((tb, dk), lambda b, h, t: (b * nt + t, q0 + h)),
                  pl.BlockSpec((tb, dk), lambda b, h, t: (b * nt + t, k0 + h)),
                  pl.BlockSpec((tb, dv), lambda b, h, t: (b * nt + t, v0 + h)),
                  pl.BlockSpec((tb, dv), lambda b, h, t: (b * nt + t, g0 + h)),
                  pl.BlockSpec((tb, LANES), lambda b, h, t: (b * nt + t, 0)),
                  pl.BlockSpec((LANES, dk), lambda b, h, t: (0, h)),
                  pl.BlockSpec((1, dk), lambda b, h, t: (0, h)),
                  pl.BlockSpec((1, dv), lambda b, h, t: (0, h)),
                  _const_spec(tri.shape), _const_spec(masks.shape)],
        out_specs=pl.BlockSpec((tb, dv), lambda b, h, t: (b * nt + t, h)),
        scratch_shapes=[pltpu.VMEM((dv, dk), F32)],
        compiler_params=_cparams(("parallel", "parallel", "arbitrary")),
        name="gla",
    )(proj, proj, proj, proj, lr, w2p, b.reshape(1, -1), nw.reshape(1, -1), tri, masks)


def _out_proj_kernel(oh_ref, og_ref, res_ref, w_ref, nw_ref, *rest, with_router):
    if with_router:
        rw_ref, h_ref, hn_ref, lg_ref = rest
    else:
        h_ref, hn_ref = rest

    kh = oh_ref.shape[1]
    acc = jnp.dot(oh_ref[...], w_ref[0:kh, :], preferred_element_type=F32)
    acc = acc + jnp.dot(og_ref[...], w_ref[kh:, :], preferred_element_type=F32)
    h = res_ref[...] + acc
    h_ref[...] = h
    hn = _rms_rows(h, nw_ref[...])
    hn_ref[...] = hn
    if with_router:
        hn_hi = hn.astype(BF16)
        hn_lo = (hn - hn_hi.astype(F32)).astype(BF16)
        rw2 = rw_ref[...]
        l2 = jnp.dot(hn_hi, rw2, preferred_element_type=F32)
        lg_ref[...] = (l2[:, :LANES] + l2[:, LANES:]
                       + jnp.dot(hn_lo, rw2[:, :LANES], preferred_element_type=F32))


def _out_proj(o_h, o_g, res, w_out, layer, nw, router_pad=None):
    n, d = res.shape
    kh, kg = o_h.shape[1], o_g.shape[1]
    with_router = router_pad is not None
    tm = 256 if with_router else 512
    row = lambda i: (i, 0)
    fixed = lambda i: (0, 0)
    in_specs = [pl.BlockSpec((tm, kh), row), pl.BlockSpec((tm, kg), row), pl.BlockSpec((tm, d), row),
                pl.BlockSpec((None, kh + kg, d), lambda i: (layer, 0, 0), pipeline_mode=pl.Buffered(1)),
                pl.BlockSpec((1, d), fixed)]
    args = [o_h, o_g, res, w_out, nw.reshape(1, d)]
    out_shape = [jax.ShapeDtypeStruct((n, d), F32), jax.ShapeDtypeStruct((n, d), F32)]
    out_specs = [pl.BlockSpec((tm, d), row), pl.BlockSpec((tm, d), row)]
    if with_router:
        in_specs.append(pl.BlockSpec((d, 2 * LANES), fixed))
        args.append(router_pad)
        out_shape.append(jax.ShapeDtypeStruct((n, LANES), F32))
        out_specs.append(pl.BlockSpec((tm, LANES), row))
    return pl.pallas_call(
        functools.partial(_out_proj_kernel, with_router=with_router),
        out_shape=out_shape,
        grid=(n // tm,),
        in_specs=in_specs,
        out_specs=out_specs,
        compiler_params=_cparams(("parallel",)),
        name="out_proj",
    )(*args)


def _dispatch_kernel(zs_ref, dest_ref, hn_ref, xs_ref, zero_ref, sem, *, tb, block):
    i = pl.program_id(0)

    def zero_copy(e):
        start = pl.multiple_of(zs_ref[e] * block, block)
        return pltpu.make_async_copy(zero_ref, xs_ref.at[pl.ds(start, block)], sem.at[1])

    @pl.when(i == 0)
    def _():
        zero_ref[...] = jnp.zeros_like(zero_ref)
        for e in range(zs_ref.shape[0]):
            @pl.when(zs_ref[e] >= 0)
            def _():
                zero_copy(e).start()
        for e in range(zs_ref.shape[0]):
            @pl.when(zs_ref[e] >= 0)
            def _():
                zero_copy(e).wait()

    def body(r, carry):
        src = hn_ref.at[pl.ds(r, 1)]
        for k in range(TOP_K):
            pltpu.make_async_copy(src, xs_ref.at[pl.ds(dest_ref[0, TOP_K * r + k], 1)],
                                  sem.at[0]).start(priority=k % 2)
        return carry

    lax.fori_loop(0, tb, body, 0, unroll=8)
    for k in range(TOP_K):
        pltpu.make_async_copy(hn_ref, xs_ref.at[pl.ds(0, tb)], sem.at[0]).wait()


def _dispatch(hn, dest, zero_start, cap, *, block, tb=512):
    n, dw = hn.shape
    tb = min(tb, n)
    grid_spec = pltpu.PrefetchScalarGridSpec(
        num_scalar_prefetch=1,
        grid=(n // tb,),
        in_specs=[pl.BlockSpec((None, 1, TOP_K * tb), lambda i, zs: (i, 0, 0), memory_space=pltpu.SMEM),
                  pl.BlockSpec((tb, dw), lambda i, zs: (i, 0))],
        out_specs=pl.BlockSpec(memory_space=pl.ANY),
        scratch_shapes=[pltpu.VMEM((block, dw), hn.dtype), pltpu.SemaphoreType.DMA((2,))],
    )
    return pl.pallas_call(
        functools.partial(_dispatch_kernel, tb=tb, block=block),
        out_shape=jax.ShapeDtypeStruct((cap, dw), hn.dtype),
        grid_spec=grid_spec,
        compiler_params=_cparams(("arbitrary",)),
        name="dispatch",
    )(zero_start, dest.reshape(n // tb, 1, TOP_K * tb), hn)


def _ffn_kernel(be_ref, nused_ref, nrows_ref, x_ref, wg_ref, wu_ref, wd_ref, o_ref, xb_ref, *, row_chunk):
    i = pl.program_id(0)
    f = pl.program_id(1)
    used = i < nused_ref[0]
    n_chunks = x_ref.shape[0] // row_chunk

    @pl.when(f == 0)
    def _():
        o_ref[...] = jnp.zeros_like(o_ref)

    @pl.when(jnp.logical_and(f == 0, used))
    def _():
        def body(c, carry):
            rows = pl.ds(pl.multiple_of(c * row_chunk, row_chunk), row_chunk)
            xb_ref[rows, :] = x_ref[rows, :].astype(BF16)
            return carry
        lax.fori_loop(0, n_chunks, body, 0)

    live = (nrows_ref[i] + row_chunk - 1) // row_chunk
    for nc in range(1, n_chunks + 1):
        @pl.when(jnp.logical_and(used, live == nc))
        def _():
            rows = pl.ds(0, nc * row_chunk)
            x = xb_ref[rows, :]
            g = jnp.dot(x, wg_ref[0].astype(BF16), preferred_element_type=F32)
            u = jnp.dot(x, wu_ref[0].astype(BF16), preferred_element_type=F32)
            a = (g * _sigmoid(g) * u).astype(BF16)
            o_ref[rows, :] += jnp.dot(a, wd_ref[0].astype(BF16), preferred_element_type=F32)


def _ffn(x, wg, wu, wd, block_e, nused, nrows, *, tm=1024, tf=512):
    r, d = x.shape
    ff = wg.shape[2]
    nf = ff // tf

    def f_eff(i, f, nu):
        return jnp.where(i < nu[0], f, nf - 1)

    grid_spec = pltpu.PrefetchScalarGridSpec(
        num_scalar_prefetch=3,
        grid=(r // tm, nf),
        in_specs=[
            pl.BlockSpec((tm, d), lambda i, f, be, nu, nr: (i, 0), pipeline_mode=pl.Buffered(1)),
            pl.BlockSpec((1, d, tf), lambda i, f, be, nu, nr: (be[i], 0, f_eff(i, f, nu))),
            pl.BlockSpec((1, d, tf), lambda i, f, be, nu, nr: (be[i], 0, f_eff(i, f, nu))),
            pl.BlockSpec((1, tf, d), lambda i, f, be, nu, nr: (be[i], f_eff(i, f, nu), 0)),
        ],
        out_specs=pl.BlockSpec((tm, d), lambda i, f, be, nu, nr: (i, 0), pipeline_mode=pl.Buffered(1)),
        scratch_shapes=[pltpu.VMEM((tm, d), BF16)],
    )
    return pl.pallas_call(
        functools.partial(_ffn_kernel, row_chunk=256),
        out_shape=jax.ShapeDtypeStruct((r, d), F32),
        grid_spec=grid_spec,
        compiler_params=_cparams(("parallel", "arbitrary")),
        name="ffn",
    )(block_e, nused, nrows, x, wg, wu, wd)


def _ple_tail(h2, p_ref, nw_ref, wg_ref, wp_ref, fw_ref, o_ref, final):
    hn = _rms_rows(h2, nw_ref[...]).astype(BF16)
    gate = _sigmoid(jnp.dot(hn, wg_ref[...], preferred_element_type=F32))
    pp = jnp.dot(p_ref[...].astype(BF16), wp_ref[...], preferred_element_type=F32)
    h3 = h2 + gate * pp
    if final:
        h3 = _rms_rows(h3, fw_ref[...])
    o_ref[...] = h3


def _ple_dense_kernel(h_ref, y_ref, p_ref, nw_ref, wg_ref, wp_ref, fw_ref, o_ref, *, final):
    h2 = h_ref[...] + y_ref[...]
    _ple_tail(h2, p_ref, nw_ref, wg_ref, wp_ref, fw_ref, o_ref, final)


def _ple_moe_kernel(dcur_ref, dnext_ref, h_ref, ys_ref, tw_ref, p_ref, nw_ref, wg_ref, wp_ref, fw_ref, o_ref,
                    gbuf_ref, sem, *, final):
    i = pl.program_id(0)
    n_steps = pl.num_programs(0)
    tm = h_ref.shape[0]
    slot = i % 2

    def issue(d_ref, s):
        def body(r, carry):
            for k in range(TOP_K):
                pltpu.make_async_copy(ys_ref.at[pl.ds(d_ref[0, TOP_K * r + k], 1)],
                                      gbuf_ref.at[s, k, pl.ds(r, 1)], sem.at[s]).start(priority=k % 2)
            return carry
        lax.fori_loop(0, tm, body, 0, unroll=8)

    @pl.when(i == 0)
    def _():
        issue(dcur_ref, 0)

    @pl.when(i + 1 < n_steps)
    def _():
        issue(dnext_ref, 1 - slot)

    for k in range(TOP_K):
        pltpu.make_async_copy(ys_ref.at[pl.ds(0, tm)], gbuf_ref.at[slot, k], sem.at[slot]).wait()
    tw = tw_ref[...]
    y = tw[:, 0:1] * gbuf_ref[slot, 0]
    for k in range(1, TOP_K):
        y = y + tw[:, k:k + 1] * gbuf_ref[slot, k]
    _ple_tail(h_ref[...] + y, p_ref, nw_ref, wg_ref, wp_ref, fw_ref, o_ref, final)


def _ple(h1, y, p, layer, nw, w_gate, w_proj, final_w, *, final, dest=None, top_w=None):
    n, d = h1.shape
    pd = p.shape[2]
    tm = 512 if dest is None else 256
    n_steps = n // tm
    row = lambda i: (i, 0)
    fixed = lambda i: (0, 0)
    tail_specs = [pl.BlockSpec((None, tm, pd), lambda i: (layer, i, 0)),
                  pl.BlockSpec((1, d), fixed),
                  pl.BlockSpec((None, d, d), lambda i: (layer, 0, 0), pipeline_mode=pl.Buffered(1)),
                  pl.BlockSpec((None, pd, d), lambda i: (layer, 0, 0), pipeline_mode=pl.Buffered(1)),
                  pl.BlockSpec((1, d), fixed)]
    tail_args = [p, nw.reshape(1, d), w_gate, w_proj, final_w.reshape(1, d)]
    scratch = []
    if dest is None:
        kern = functools.partial(_ple_dense_kernel, final=final)
        in_specs = [pl.BlockSpec((tm, d), row), pl.BlockSpec((tm, d), row)] + tail_specs
        args = [h1, y] + tail_args
    else:
        kern = functools.partial(_ple_moe_kernel, final=final)
        dest3 = dest.reshape(n_steps, 1, TOP_K * tm)
        smem = lambda imap: pl.BlockSpec((None, 1, TOP_K * tm), imap, memory_space=pltpu.SMEM)
        in_specs = [smem(lambda i: (i, 0, 0)), smem(lambda i: (jnp.minimum(i + 1, n_steps - 1), 0, 0)),
                    pl.BlockSpec((tm, d), row), pl.BlockSpec(memory_space=pl.ANY),
                    pl.BlockSpec((tm, TOP_K), row)] + tail_specs
        args = [dest3, dest3, h1, y, top_w] + tail_args
        scratch = scratch + [pltpu.VMEM((2, TOP_K, tm, d), F32), pltpu.SemaphoreType.DMA((2,))]
    return pl.pallas_call(
        kern,
        out_shape=jax.ShapeDtypeStruct((n, d), F32),
        grid=(n_steps,),
        in_specs=in_specs,
        out_specs=pl.BlockSpec((tm, d), row),
        scratch_shapes=scratch,
        compiler_params=_cparams(("arbitrary",)),
        name="ple",
    )(*args)


def _route(logits, block):
    n = logits.shape[0]
    top_logit, top_e = lax.top_k(logits, TOP_K)
    top_w = jax.nn.softmax(top_logit, axis=-1)
    flat_e = top_e.reshape(-1)
    onehot = (flat_e[:, None] == jnp.arange(N_EXPERTS, dtype=flat_e.dtype)[None, :]).astype(jnp.int32)
    csum = jnp.cumsum(onehot, axis=0)
    counts = csum[-1]
    rank = jnp.sum((csum - onehot) * onehot, axis=1)
    padded = (counts + block - 1) // block * block
    padded_end = jnp.cumsum(padded)
    padded_start = padded_end - padded
    dest = (padded_start[flat_e] + rank).astype(jnp.int32)
    n_blocks = -(-(n * TOP_K) // block) + N_EXPERTS
    blk = jnp.arange(n_blocks, dtype=jnp.int32)
    block_e = jnp.minimum(jnp.sum((blk[:, None] * block >= padded_end[None, :]).astype(jnp.int32), axis=1),
                          N_EXPERTS - 1)
    nused = (padded_end[-1] // block).astype(jnp.int32).reshape(1)
    block_e = jnp.where(jnp.arange(n_blocks) < nused[0], block_e, block_e[jnp.maximum(nused[0] - 1, 0)])
    nrows = jnp.clip(counts[block_e] - (blk * block - padded_start[block_e]), 0, block).astype(jnp.int32)
    tail = nused[0] + jnp.arange(N_EXPERTS, dtype=jnp.int32)
    zero_start = jnp.concatenate([jnp.where(padded > 0, padded_end // block - 1, -1),
                                  jnp.where(tail < n_blocks, tail, -1)]).astype(jnp.int32)
    return dest.reshape(n, TOP_K), top_w, block_e, nused, nrows, zero_start, n_blocks * block


def kernel(x, p, norm_mix_w, w_in, hgrn_lb_logits, hgrn_norm_w, gla_gate_w2, gla_gate_b, gla_norm_w, w_out,
           norm_ffn_w, dense_w_gate, dense_w_up, dense_w_down, moe_router, moe_w_gate, moe_w_up, moe_w_down,
           norm_ple_w, ple_w_gate, ple_w_proj, final_norm_w):
    batch, seq, d = x.shape
    depth = w_in.shape[0]
    n = batch * seq
    moe_block = 1024

    lbs = jnp.cumsum(jax.nn.softmax(hgrn_lb_logits.astype(F32), axis=0), axis=0)
    lbs = lbs - lbs[0]
    tri_np, masks_np = _level_tables()
    tri = jnp.asarray(tri_np, BF16)
    masks = jnp.asarray(masks_np, BF16)

    h = x.reshape(n, d)
    p3 = p.reshape(depth, n, -1)
    moe_wg = moe_w_gate.reshape((-1,) + moe_w_gate.shape[2:])
    moe_wu = moe_w_up.reshape((-1,) + moe_w_up.shape[2:])
    moe_wd = moe_w_down.reshape((-1,) + moe_w_down.shape[2:])
    w_in_t = jnp.swapaxes(w_in, 1, 2)
    w_out_bf = w_out.astype(BF16)
    ple_wg_bf = ple_w_gate.astype(BF16)
    ple_wp_bf = ple_w_proj.astype(BF16)
    for i in range(depth):
        proj, lr = _in_proj(h, norm_mix_w[i], w_in_t, i)
        o_h = _hgrn(proj, lbs[i], hgrn_norm_w[i], tri, masks, batch=batch)
        w2p = jnp.zeros((LANES, gla_gate_w2.shape[2]), F32).at[:GLA_GATE_RANK].set(gla_gate_w2[i]).astype(BF16)
        o_g = _gla(proj, lr, w2p, gla_gate_b[i], gla_norm_w[i], tri, masks, batch=batch)
        j = i // 2
        last = i == depth - 1
        if i % 2 == 0:
            h1, hn = _out_proj(o_h, o_g, h, w_out_bf, i, norm_ffn_w[i])
            n_blk = n // moe_block
            y = _ffn(hn, dense_w_gate, dense_w_up, dense_w_down,
                     jnp.full((n_blk,), j, jnp.int32), jnp.full((1,), n_blk, jnp.int32),
                     jnp.full((n_blk,), moe_block, jnp.int32), tm=moe_block)
            h = _ple(h1, y, p3, i, norm_ple_w[i], ple_wg_bf, ple_wp_bf, final_norm_w, final=last)
        else:
            rw = jnp.zeros((d, LANES), F32).at[:, :N_EXPERTS].set(moe_router[j])
            rw_hi = rw.astype(BF16)
            router2 = jnp.concatenate([rw_hi, (rw - rw_hi.astype(F32)).astype(BF16)], axis=1)
            h1, hn, logits = _out_proj(o_h, o_g, h, w_out_bf, i, norm_ffn_w[i], router2)
            dest, top_w, block_e, nused, nrows, zero_start, cap = _route(logits[:, :N_EXPERTS], moe_block)
            xs = _dispatch(hn, dest, zero_start, cap, block=moe_block)
            ys = _ffn(xs, moe_wg, moe_wu, moe_wd, block_e + j * N_EXPERTS, nused, nrows, tm=moe_block)
            h = _ple(h1, ys, p3, i, norm_ple_w[i], ple_wg_bf, ple_wp_bf, final_norm_w, final=last,
                     dest=dest, top_w=top_w)
    return h.reshape(batch, seq, d)
```

```python
import functools

import numpy as np
import jax
import jax.numpy as jnp
from jax import lax
from jax.experimental import pallas as pl
from jax.experimental.pallas import tpu as pltpu

F32 = jnp.float32
BF16 = jnp.bfloat16

EPS = 1e-6
LOG2_E = 1.4426950408889634
HGRN_HEADS = 8
HGRN_HEAD_DIM = 128
GLA_HEADS = 4
GLA_HEAD_K = 128
GLA_HEAD_V = 256
GLA_GATE_RANK = 16
GLA_GATE_TEMP = 16.0
N_EXPERTS = 8
TOP_K = 2

LANES = 128
CHUNK = 128
N_LEVELS = 7
VMEM_LIMIT = 56 * 1024 * 1024

_N_MAIN = 6144
_COL_GG = 6144
_COL_GLR = 7168


def _cparams(sem):
    return pltpu.CompilerParams(dimension_semantics=sem, vmem_limit_bytes=VMEM_LIMIT)


def _sigmoid(x):
    return 1.0 / (1.0 + jnp.exp(-x))


def _rms_rows(x, w):
    ms = jnp.mean(x * x, axis=-1, keepdims=True)
    return x * lax.rsqrt(ms + EPS) * w


def _in_proj_kernel(x_ref, nw_ref, w_ref, wlr_ref, o_ref, lr_ref, xn_ref, *, row_chunk):
    tm = x_ref.shape[0]

    @pl.when(pl.program_id(1) == 0)
    def _():
        def body(c, carry):
            r = pl.multiple_of(c * row_chunk, row_chunk)
            xn_ref[pl.ds(r, row_chunk), :] = _rms_rows(x_ref[pl.ds(r, row_chunk), :], nw_ref[...]).astype(BF16)
            return carry
        lax.fori_loop(0, tm // row_chunk, body, 0)
        lr_ref[...] = lax.dot_general(xn_ref[...], wlr_ref[0].astype(BF16), _NT,
                                      preferred_element_type=F32).astype(lr_ref.dtype)

    o_ref[...] = lax.dot_general(xn_ref[...], w_ref[0].astype(BF16), _NT,
                                 preferred_element_type=F32).astype(o_ref.dtype)


def _in_proj(x, nw, w_t, layer, *, tm=1024, tn=1024):
    n, d = x.shape
    tm = min(tm, n)
    n_main = _N_MAIN // tn
    n_gg = (_COL_GLR - _COL_GG) // tn
    gg0 = _COL_GG + GLA_GATE_RANK

    def w_row(j):
        q, t = GLA_GATE_RANK, tn // GLA_GATE_RANK
        return q * jnp.where(j < n_main, j * t, gg0 // q + (j - n_main) * t)

    slab = lambda rows, imap: pl.BlockSpec((pl.Element(1), pl.Element(rows), pl.Element(d)), imap)
    return pl.pallas_call(
        functools.partial(_in_proj_kernel, row_chunk=128),
        out_shape=[jax.ShapeDtypeStruct((n, (n_main + n_gg) * tn), BF16), jax.ShapeDtypeStruct((n, LANES), BF16)],
        grid=(n // tm, n_main + n_gg),
        in_specs=[
            pl.BlockSpec((tm, d), lambda i, j: (i, 0)),
            pl.BlockSpec((1, d), lambda i, j: (0, 0)),
            slab(tn, lambda i, j: (layer, w_row(j), 0)),
            slab(LANES, lambda i, j: (layer, _COL_GG, 0)),
        ],
        out_specs=[pl.BlockSpec((tm, tn), lambda i, j: (i, j)), pl.BlockSpec((tm, LANES), lambda i, j: (i, 0))],
        scratch_shapes=[pltpu.VMEM((tm, d), BF16)],
        compiler_params=_cparams(("parallel", "arbitrary")),
        name="in_proj",
    )(x, nw.reshape(1, d), w_t, w_t)


def _level_tables():
    c = CHUNK
    idx = np.arange(c)
    masks = []
    s = c // 2
    while s >= 1:
        blk = idx // (2 * s)
        upper = (idx % (2 * s)) >= s
        masks.append(((blk[:, None] == blk[None, :]) & upper[:, None] & (~upper)[None, :]).astype(np.float32))
        s //= 2
    masks.append(np.eye(c, dtype=np.float32))
    tri = np.tril(np.ones((c, c), np.float32))
    return np.concatenate([tri, tri], axis=1), np.stack(masks)


def _neg_abs(d):
    return -jnp.abs(d)


def _level_exponents(g_cum):
    c, dk = g_cum.shape
    row = lax.broadcasted_iota(jnp.int32, (c, dk), 0)
    out = []
    s = c // 2
    while s >= 4:
        nb = c // (2 * s)
        mid = g_cum.reshape(nb, 2 * s, dk)[:, s - 1:s, :]
        mid = jnp.broadcast_to(mid, (nb, 2 * s, dk)).reshape(c, dk)
        out.append(_neg_abs(g_cum - mid))
        s //= 2
    up1 = pltpu.roll(g_cum, 1, 0)
    up2 = pltpu.roll(g_cum, 2, 0)
    dn1 = pltpu.roll(g_cum, c - 1, 0)
    r4 = row % 4
    mid2 = jnp.where(r4 == 0, dn1, jnp.where(r4 == 1, g_cum, jnp.where(r4 == 2, up1, up2)))
    out.append(_neg_abs(g_cum - mid2))
    mid1 = jnp.where(row % 2 == 1, up1, g_cum)
    out.append(_neg_abs(g_cum - mid1))
    return out


_NT = (((1,), (1,)), ((), ()))
_TN = (((0,), (0,)), ((), ()))


def _chunk_prep(g, tri_ref):
    g_hi = g.astype(BF16)
    g_lo = (g - g_hi.astype(F32)).astype(BF16)
    return jnp.dot(tri_ref[...], jnp.concatenate([g_hi, g_lo], axis=0), preferred_element_type=F32) * LOG2_E


def _chunk_local(q, k, v_bf, g_cum, masks_ref):
    c = CHUNK
    q_bf = q.astype(BF16)
    k_bf = k.astype(BF16)
    a = masks_ref[N_LEVELS] * lax.dot_general(q_bf, k_bf, _NT, preferred_element_type=F32).astype(BF16)
    for l, ex in enumerate(_level_exponents(g_cum)):
        e = jnp.exp2(ex).astype(BF16)
        s = lax.dot_general(q_bf * e, k_bf * e, _NT, preferred_element_type=F32)
        a = a + masks_ref[l] * s.astype(BF16)
    g_last = g_cum[c - 1:c, :]
    o_intra = jnp.dot(a, v_bf, preferred_element_type=F32)
    q_dec = (q * jnp.exp2(g_cum)).astype(BF16)
    k_dec = (k * jnp.exp2(g_last - g_cum)).astype(BF16)
    kv = lax.dot_general(v_bf, k_dec, _TN, preferred_element_type=F32)
    return o_intra, q_dec, kv, jnp.exp2(g_last)


def _head_norm(o, w):
    ms = jnp.mean(o * o, axis=-1, keepdims=True)
    return o * lax.rsqrt(ms + EPS) * w


def _scan_block(n_chunks, gates_fn, gate_fn, nw_ref, tri_ref, masks_ref, o_ref, st_ref):
    gates = [gates_fn(ci) for ci in range(n_chunks)]
    cums = [_chunk_prep(g, tri_ref) for (_, _, _, g) in gates]
    parts = [_chunk_local(q, k, v_bf, g_cum, masks_ref) for (q, k, v_bf, _), g_cum in zip(gates, cums)]
    st = st_ref[...]
    for ci, (o_intra, q_dec, kv, decay) in enumerate(parts):
        o = o_intra + lax.dot_general(q_dec, st.astype(BF16), _NT, preferred_element_type=F32)
        rows = pl.ds(ci * CHUNK, CHUNK)
        o_ref[rows, :] = (_head_norm(o, nw_ref[...]) * gate_fn(rows)).astype(o_ref.dtype)
        st = st * decay + kv
    st_ref[...] = st


def _hgrn_kernel(q_ref, f_ref, i_ref, g_ref, lb_ref, nw_ref, tri_ref, masks_ref, o_ref, st_ref):
    @pl.when(pl.program_id(2) == 0)
    def _():
        st_ref[...] = jnp.zeros_like(st_ref)

    lb = lb_ref[...]

    def local(ci):
        rows = pl.ds(ci * CHUNK, CHUNK)
        fgate = lb + (1.0 - lb) * _sigmoid(f_ref[rows, :].astype(F32))
        g = jnp.log(jnp.maximum(fgate, 1e-38))
        k = 1.0 - fgate
        hq = q_ref[rows, :].astype(F32)
        return hq * _sigmoid(hq), k, i_ref[rows, :], g

    def gate(rows):
        return _sigmoid(g_ref[rows, :].astype(F32))

    _scan_block(q_ref.shape[0] // CHUNK, local, gate, nw_ref, tri_ref, masks_ref, o_ref, st_ref)


def _gla_kernel(q_ref, k_ref, v_ref, g_ref, lr_ref, w2_ref, b_ref, nw_ref, tri_ref, masks_ref, o_ref, st_ref):
    @pl.when(pl.program_id(2) == 0)
    def _():
        st_ref[...] = jnp.zeros_like(st_ref)

    def local(ci):
        rows = pl.ds(ci * CHUNK, CHUNK)
        u = jnp.dot(lr_ref[rows, :], w2_ref[...], preferred_element_type=F32) + b_ref[...]
        g = (jnp.minimum(u, 0.0) - jnp.log(1.0 + jnp.exp(-jnp.abs(u)))) * (1.0 / GLA_GATE_TEMP)
        q = q_ref[rows, :].astype(F32) * (GLA_HEAD_K ** -0.5)
        return q, k_ref[rows, :].astype(F32), v_ref[rows, :], g

    def gate(rows):
        gg = g_ref[rows, :].astype(F32)
        return gg * _sigmoid(gg)

    _scan_block(q_ref.shape[0] // CHUNK, local, gate, nw_ref, tri_ref, masks_ref, o_ref, st_ref)


def _const_spec(shape):
    nd = len(shape)
    return pl.BlockSpec(shape, lambda b, h, t: (0,) * nd)


def _hgrn(proj, lb, nw, tri, masks, *, batch, tb=2048):
    n = proj.shape[0]
    nt = n // batch // tb
    dk = HGRN_HEAD_DIM
    hh = HGRN_HEADS

    def col(off):
        return pl.BlockSpec((tb, dk), lambda b, h, t: (b * nt + t, off + h))

    return pl.pallas_call(
        _hgrn_kernel,
        out_shape=jax.ShapeDtypeStruct((n, hh * dk), BF16),
        grid=(batch, hh, nt),
        in_specs=[col(0), col(hh), col(2 * hh), col(3 * hh),
                  pl.BlockSpec((1, dk), lambda b, h, t: (0, h)),
                  pl.BlockSpec((1, dk), lambda b, h, t: (0, h)),
                  _const_spec(tri.shape), _const_spec(masks.shape)],
        out_specs=pl.BlockSpec((tb, dk), lambda b, h, t: (b * nt + t, h)),
        scratch_shapes=[pltpu.VMEM((dk, dk), F32)],
        compiler_params=_cparams(("parallel", "parallel", "arbitrary")),
        name="hgrn",
    )(proj, proj, proj, proj, lb.reshape(1, -1), nw.reshape(1, -1), tri, masks)


def _gla(proj, lr, w2p, b, nw, tri, masks, *, batch, tb=1024):
    n = proj.shape[0]
    nt = n // batch // tb
    dk, dv, hh = GLA_HEAD_K, GLA_HEAD_V, GLA_HEADS
    q0 = 4096 // dk
    k0 = q0 + hh
    v0 = 5120 // dv
    g0 = _COL_GG // dv
    return pl.pallas_call(
        _gla_kernel,
        out_shape=jax.ShapeDtypeStruct((n, hh * dv), BF16),
        grid=(batch, hh, nt),
        in_specs=[pl.BlockSpec((tb, dk), lambda b, h, t: (b * nt + t, q0 + h)),
                  pl.BlockSpec((tb, dk), lambda b, h, t: (b * nt + t, k0 + h)),
                  pl.BlockSpec((tb, dv), lambda b, h, t: (b * nt + t, v0 + h)),
                  pl.BlockSpec((tb, dv), lambda b, h, t: (b * nt + t, g0 + h)),
                  pl.BlockSpec((tb, LANES), lambda b, h, t: (b * nt + t, 0)),
                  pl.BlockSpec((LANES, dk), lambda b, h, t: (0, h)),
                  pl.BlockSpec((1, dk), lambda b, h, t: (0, h)),
                  pl.BlockSpec((1, dv), lambda b, h, t: (0, h)),
                  _const_spec(tri.shape), _const_spec(masks.shape)],
        out_specs=pl.BlockSpec((tb, dv), lambda b, h, t: (b * nt + t, h)),
        scratch_shapes=[pltpu.VMEM((dv, dk), F32)],
        compiler_params=_cparams(("parallel", "parallel", "arbitrary")),
        name="gla",
    )(proj, proj, proj, proj, lr, w2p, b.reshape(1, -1), nw.reshape(1, -1), tri, masks)


def _out_proj_kernel(oh_ref, og_ref, res_ref, w_ref, nw_ref, *rest, with_router):
    if with_router:
        rw_ref, h_ref, hn_ref, lg_ref = rest
    else:
        h_ref, hn_ref = rest

    kh = oh_ref.shape[1]
    acc = jnp.dot(oh_ref[...], w_ref[0:kh, :], preferred_element_type=F32)
    acc = acc + jnp.dot(og_ref[...], w_ref[kh:, :], preferred_element_type=F32)
    h = res_ref[...] + acc
    h_ref[...] = h
    hn = _rms_rows(h, nw_ref[...])
    hn_ref[...] = hn
    if with_router:
        hn_hi = hn.astype(BF16)
        hn_lo = (hn - hn_hi.astype(F32)).astype(BF16)
        rw2 = rw_ref[...]
        l2 = jnp.dot(hn_hi, rw2, preferred_element_type=F32)
        lg_ref[...] = (l2[:, :LANES] + l2[:, LANES:]
                       + jnp.dot(hn_lo, rw2[:, :LANES], preferred_element_type=F32))


def _out_proj(o_h, o_g, res, w_out, layer, nw, router_pad=None):
    n, d = res.shape
    kh, kg = o_h.shape[1], o_g.shape[1]
    with_router = router_pad is not None
    tm = 256 if with_router else 512
    row = lambda i: (i, 0)
    fixed = lambda i: (0, 0)
    in_specs = [pl.BlockSpec((tm, kh), row), pl.BlockSpec((tm, kg), row), pl.BlockSpec((tm, d), row),
                pl.BlockSpec((None, kh + kg, d), lambda i: (layer, 0, 0), pipeline_mode=pl.Buffered(1)),
                pl.BlockSpec((1, d), fixed)]
    args = [o_h, o_g, res, w_out, nw.reshape(1, d)]
    out_shape = [jax.ShapeDtypeStruct((n, d), F32), jax.ShapeDtypeStruct((n, d), F32)]
    out_specs = [pl.BlockSpec((tm, d), row), pl.BlockSpec((tm, d), row)]
    if with_router:
        in_specs.append(pl.BlockSpec((d, 2 * LANES), fixed))
        args.append(router_pad)
        out_shape.append(jax.ShapeDtypeStruct((n, LANES), F32))
        out_specs.append(pl.BlockSpec((tm, LANES), row))
    return pl.pallas_call(
        functools.partial(_out_proj_kernel, with_router=with_router),
        out_shape=out_shape,
        grid=(n // tm,),
        in_specs=in_specs,
        out_specs=out_specs,
        compiler_params=_cparams(("parallel",)),
        name="out_proj",
    )(*args)


def _dispatch_kernel(zs_ref, dest_ref, hn_ref, xs_ref, zero_ref, sem, *, tb, block):
    i = pl.program_id(0)

    def zero_copy(e):
        start = pl.multiple_of(zs_ref[e] * block, block)
        return pltpu.make_async_copy(zero_ref, xs_ref.at[pl.ds(start, block)], sem.at[1])

    @pl.when(i == 0)
    def _():
        zero_ref[...] = jnp.zeros_like(zero_ref)
        for e in range(zs_ref.shape[0]):
            @pl.when(zs_ref[e] >= 0)
            def _():
                zero_copy(e).start()
        for e in range(zs_ref.shape[0]):
            @pl.when(zs_ref[e] >= 0)
            def _():
                zero_copy(e).wait()

    def body(r, carry):
        src = hn_ref.at[pl.ds(r, 1)]
        for k in range(TOP_K):
            pltpu.make_async_copy(src, xs_ref.at[pl.ds(dest_ref[0, TOP_K * r + k], 1)],
                                  sem.at[0]).start(priority=k % 2)
        return carry

    lax.fori_loop(0, tb, body, 0, unroll=8)
    for k in range(TOP_K):
        pltpu.make_async_copy(hn_ref, xs_ref.at[pl.ds(0, tb)], sem.at[0]).wait()


def _dispatch(hn, dest, zero_start, cap, *, block, tb=512):
    n, dw = hn.shape
    tb = min(tb, n)
    grid_spec = pltpu.PrefetchScalarGridSpec(
        num_scalar_prefetch=1,
        grid=(n // tb,),
        in_specs=[pl.BlockSpec((None, 1, TOP_K * tb), lambda i, zs: (i, 0, 0), memory_space=pltpu.SMEM),
                  pl.BlockSpec((tb, dw), lambda i, zs: (i, 0))],
        out_specs=pl.BlockSpec(memory_space=pl.ANY),
        scratch_shapes=[pltpu.VMEM((block, dw), hn.dtype), pltpu.SemaphoreType.DMA((2,))],
    )
    return pl.pallas_call(
        functools.partial(_dispatch_kernel, tb=tb, block=block),
        out_shape=jax.ShapeDtypeStruct((cap, dw), hn.dtype),
        grid_spec=grid_spec,
        compiler_params=_cparams(("arbitrary",)),
        name="dispatch",
    )(zero_start, dest.reshape(n // tb, 1, TOP_K * tb), hn)


def _ffn_kernel(be_ref, nused_ref, nrows_ref, x_ref, wg_ref, wu_ref, wd_ref, o_ref, xb_ref, *, row_chunk):
    i = pl.program_id(0)
    f = pl.program_id(1)
    used = i < nused_ref[0]
    n_chunks = x_ref.shape[0] // row_chunk

    @pl.when(f == 0)
    def _():
        o_ref[...] = jnp.zeros_like(o_ref)

    @pl.when(jnp.logical_and(f == 0, used))
    def _():
        def body(c, carry):
            rows = pl.ds(pl.multiple_of(c * row_chunk, row_chunk), row_chunk)
            xb_ref[rows, :] = x_ref[rows, :].astype(BF16)
            return carry
        lax.fori_loop(0, n_chunks, body, 0)

    live = (nrows_ref[i] + row_chunk - 1) // row_chunk
    for nc in range(1, n_chunks + 1):
        @pl.when(jnp.logical_and(used, live == nc))
        def _():
            rows = pl.ds(0, nc * row_chunk)
            x = xb_ref[rows, :]
            g = jnp.dot(x, wg_ref[0].astype(BF16), preferred_element_type=F32)
            u = jnp.dot(x, wu_ref[0].astype(BF16), preferred_element_type=F32)
            a = (g * _sigmoid(g) * u).astype(BF16)
            o_ref[rows, :] += jnp.dot(a, wd_ref[0].astype(BF16), preferred_element_type=F32)


def _ffn(x, wg, wu, wd, block_e, nused, nrows, *, tm=1024, tf=512):
    r, d = x.shape
    ff = wg.shape[2]
    nf = ff // tf

    def f_eff(i, f, nu):
        return jnp.where(i < nu[0], f, nf - 1)

    grid_spec = pltpu.PrefetchScalarGridSpec(
        num_scalar_prefetch=3,
        grid=(r // tm, nf),
        in_specs=[
            pl.BlockSpec((tm, d), lambda i, f, be, nu, nr: (i, 0), pipeline_mode=pl.Buffered(1)),
            pl.BlockSpec((1, d, tf), lambda i, f, be, nu, nr: (be[i], 0, f_eff(i, f, nu))),
            pl.BlockSpec((1, d, tf), lambda i, f, be, nu, nr: (be[i], 0, f_eff(i, f, nu))),
            pl.BlockSpec((1, tf, d), lambda i, f, be, nu, nr: (be[i], f_eff(i, f, nu), 0)),
        ],
        out_specs=pl.BlockSpec((tm, d), lambda i, f, be, nu, nr: (i, 0), pipeline_mode=pl.Buffered(1)),
        scratch_shapes=[pltpu.VMEM((tm, d), BF16)],
    )
    return pl.pallas_call(
        functools.partial(_ffn_kernel, row_chunk=256),
        out_shape=jax.ShapeDtypeStruct((r, d), F32),
        grid_spec=grid_spec,
        compiler_params=_cparams(("parallel", "arbitrary")),
        name="ffn",
    )(block_e, nused, nrows, x, wg, wu, wd)


def _ple_tail(h2, p_ref, nw_ref, wg_ref, wp_ref, fw_ref, o_ref, final):
    hn = _rms_rows(h2, nw_ref[...]).astype(BF16)
    gate = _sigmoid(jnp.dot(hn, wg_ref[...], preferred_element_type=F32))
    pp = jnp.dot(p_ref[...].astype(BF16), wp_ref[...], preferred_element_type=F32)
    h3 = h2 + gate * pp
    if final:
        h3 = _rms_rows(h3, fw_ref[...])
    o_ref[...] = h3


def _ple_dense_kernel(h_ref, y_ref, p_ref, nw_ref, wg_ref, wp_ref, fw_ref, o_ref, *, final):
    h2 = h_ref[...] + y_ref[...]
    _ple_tail(h2, p_ref, nw_ref, wg_ref, wp_ref, fw_ref, o_ref, final)


def _ple_moe_kernel(dcur_ref, dnext_ref, h_ref, ys_ref, tw_ref, p_ref, nw_ref, wg_ref, wp_ref, fw_ref, o_ref,
                    gbuf_ref, sem, *, final):
    i = pl.program_id(0)
    n_steps = pl.num_programs(0)
    tm = h_ref.shape[0]
    slot = i % 2

    def issue(d_ref, s):
        def body(r, carry):
            for k in range(TOP_K):
                pltpu.make_async_copy(ys_ref.at[pl.ds(d_ref[0, TOP_K * r + k], 1)],
                                      gbuf_ref.at[s, k, pl.ds(r, 1)], sem.at[s]).start(priority=k % 2)
            return carry
        lax.fori_loop(0, tm, body, 0, unroll=8)

    @pl.when(i == 0)
    def _():
        issue(dcur_ref, 0)

    @pl.when(i + 1 < n_steps)
    def _():
        issue(dnext_ref, 1 - slot)

    for k in range(TOP_K):
        pltpu.make_async_copy(ys_ref.at[pl.ds(0, tm)], gbuf_ref.at[slot, k], sem.at[slot]).wait()
    tw = tw_ref[...]
    y = tw[:, 0:1] * gbuf_ref[slot, 0]
    for k in range(1, TOP_K):
        y = y + tw[:, k:k + 1] * gbuf_ref[slot, k]
    _ple_tail(h_ref[...] + y, p_ref, nw_ref, wg_ref, wp_ref, fw_ref, o_ref, final)


def _ple(h1, y, p, layer, nw, w_gate, w_proj, final_w, *, final, dest=None, top_w=None):
    n, d = h1.shape
    pd = p.shape[2]
    tm = 512 if dest is None else 256
    n_steps = n // tm
    row = lambda i: (i, 0)
    fixed = lambda i: (0, 0)
    tail_specs = [pl.BlockSpec((None, tm, pd), lambda i: (layer, i, 0)),
                  pl.BlockSpec((1, d), fixed),
                  pl.BlockSpec((None, d, d), lambda i: (layer, 0, 0), pipeline_mode=pl.Buffered(1)),
                  pl.BlockSpec((None, pd, d), lambda i: (layer, 0, 0), pipeline_mode=pl.Buffered(1)),
                  pl.BlockSpec((1, d), fixed)]
    tail_args = [p, nw.reshape(1, d), w_gate, w_proj, final_w.reshape(1, d)]
    scratch = []
    if dest is None:
        kern = functools.partial(_ple_dense_kernel, final=final)
        in_specs = [pl.BlockSpec((tm, d), row), pl.BlockSpec((tm, d), row)] + tail_specs
        args = [h1, y] + tail_args
    else:
        kern = functools.partial(_ple_moe_kernel, final=final)
        dest3 = dest.reshape(n_steps, 1, TOP_K * tm)
        smem = lambda imap: pl.BlockSpec((None, 1, TOP_K * tm), imap, memory_space=pltpu.SMEM)
        in_specs = [smem(lambda i: (i, 0, 0)), smem(lambda i: (jnp.minimum(i + 1, n_steps - 1), 0, 0)),
                    pl.BlockSpec((tm, d), row), pl.BlockSpec(memory_space=pl.ANY),
                    pl.BlockSpec((tm, TOP_K), row)] + tail_specs
        args = [dest3, dest3, h1, y, top_w] + tail_args
        scratch = scratch + [pltpu.VMEM((2, TOP_K, tm, d), F32), pltpu.SemaphoreType.DMA((2,))]
    return pl.pallas_call(
        kern,
        out_shape=jax.ShapeDtypeStruct((n, d), F32),
        grid=(n_steps,),
        in_specs=in_specs,
        out_specs=pl.BlockSpec((tm, d), row),
        scratch_shapes=scratch,
        compiler_params=_cparams(("arbitrary",)),
        name="ple",
    )(*args)


def _route(logits, block):
    n = logits.shape[0]
    top_logit, top_e = lax.top_k(logits, TOP_K)
    top_w = jax.nn.softmax(top_logit, axis=-1)
    flat_e = top_e.reshape(-1)
    onehot = (flat_e[:, None] == jnp.arange(N_EXPERTS, dtype=flat_e.dtype)[None, :]).astype(jnp.int32)
    csum = jnp.cumsum(onehot, axis=0)
    counts = csum[-1]
    rank = jnp.sum((csum - onehot) * onehot, axis=1)
    padded = (counts + block - 1) // block * block
    padded_end = jnp.cumsum(padded)
    padded_start = padded_end - padded
    dest = (padded_start[flat_e] + rank).astype(jnp.int32)
    n_blocks = -(-(n * TOP_K) // block) + N_EXPERTS
    blk = jnp.arange(n_blocks, dtype=jnp.int32)
    block_e = jnp.minimum(jnp.sum((blk[:, None] * block >= padded_end[None, :]).astype(jnp.int32), axis=1),
                          N_EXPERTS - 1)
    nused = (padded_end[-1] // block).astype(jnp.int32).reshape(1)
    block_e = jnp.where(jnp.arange(n_blocks) < nused[0], block_e, block_e[jnp.maximum(nused[0] - 1, 0)])
    nrows = jnp.clip(counts[block_e] - (blk * block - padded_start[block_e]), 0, block).astype(jnp.int32)
    tail = nused[0] + jnp.arange(N_EXPERTS, dtype=jnp.int32)
    zero_start = jnp.concatenate([jnp.where(padded > 0, padded_end // block - 1, -1),
                                  jnp.where(tail < n_blocks, tail, -1)]).astype(jnp.int32)
    return dest.reshape(n, TOP_K), top_w, block_e, nused, nrows, zero_start, n_blocks * block


def kernel(x, p, norm_mix_w, w_in, hgrn_lb_logits, hgrn_norm_w, gla_gate_w2, gla_gate_b, gla_norm_w, w_out,
           norm_ffn_w, dense_w_gate, dense_w_up, dense_w_down, moe_router, moe_w_gate, moe_w_up, moe_w_down,
           norm_ple_w, ple_w_gate, ple_w_proj, final_norm_w):
    batch, seq, d = x.shape
    depth = w_in.shape[0]
    n = batch * seq
    moe_block = 1024

    lbs = jnp.cumsum(jax.nn.softmax(hgrn_lb_logits.astype(F32), axis=0), axis=0)
    lbs = lbs - lbs[0]
    tri_np, masks_np = _level_tables()
    tri = jnp.asarray(tri_np, BF16)
    masks = jnp.asarray(masks_np, BF16)

    h = x.reshape(n, d)
    p3 = p.reshape(depth, n, -1)
    moe_wg = moe_w_gate.reshape((-1,) + moe_w_gate.shape[2:])
    moe_wu = moe_w_up.reshape((-1,) + moe_w_up.shape[2:])
    moe_wd = moe_w_down.reshape((-1,) + moe_w_down.shape[2:])
    w_in_t = jnp.swapaxes(w_in, 1, 2)
    w_out_bf = w_out.astype(BF16)
    ple_wg_bf = ple_w_gate.astype(BF16)
    ple_wp_bf = ple_w_proj.astype(BF16)
    for i in range(depth):
        proj, lr = _in_proj(h, norm_mix_w[i], w_in_t, i)
        o_h = _hgrn(proj, lbs[i], hgrn_norm_w[i], tri, masks, batch=batch)
        w2p = jnp.zeros((LANES, gla_gate_w2.shape[2]), F32).at[:GLA_GATE_RANK].set(gla_gate_w2[i]).astype(BF16)
        o_g = _gla(proj, lr, w2p, gla_gate_b[i], gla_norm_w[i], tri, masks, batch=batch)
        j = i // 2
        last = i == depth - 1
        if i % 2 == 0:
            h1, hn = _out_proj(o_h, o_g, h, w_out_bf, i, norm_ffn_w[i])
            n_blk = n // moe_block
            y = _ffn(hn, dense_w_gate, dense_w_up, dense_w_down,
                     jnp.full((n_blk,), j, jnp.int32), jnp.full((1,), n_blk, jnp.int32),
                     jnp.full((n_blk,), moe_block, jnp.int32), tm=moe_block)
            h = _ple(h1, y, p3, i, norm_ple_w[i], ple_wg_bf, ple_wp_bf, final_norm_w, final=last)
        else:
            rw = jnp.zeros((d, LANES), F32).at[:, :N_EXPERTS].set(moe_router[j])
            rw_hi = rw.astype(BF16)
            router2 = jnp.concatenate([rw_hi, (rw - rw_hi.astype(F32)).astype(BF16)], axis=1)
            h1, hn, logits = _out_proj(o_h, o_g, h, w_out_bf, i, norm_ffn_w[i], router2)
            dest, top_w, block_e, nused, nrows, zero_start, cap = _route(logits[:, :N_EXPERTS], moe_block)
            xs = _dispatch(hn, dest, zero_start, cap, block=moe_block)
            ys = _ffn(xs, moe_wg, moe_wu, moe_wd, block_e + j * N_EXPERTS, nused, nrows, tm=moe_block)
            h = _ple(h1, ys, p3, i, norm_ple_w[i], ple_wg_bf, ple_wp_bf, final_norm_w, final=last,
                     dest=dest, top_w=top_w)
    return h.reshape(batch, seq, d)
```

```python
import functools

import numpy as np
import jax
import jax.numpy as jnp
from jax import lax
from jax.experimental import pallas as pl
from jax.experimental.pallas import tpu as pltpu

F32 = jnp.float32
BF16 = jnp.bfloat16

EPS = 1e-6
LOG2_E = 1.4426950408889634
HGRN_HEADS = 8
HGRN_HEAD_DIM = 128
GLA_HEADS = 4
GLA_HEAD_K = 128
GLA_HEAD_V = 256
GLA_GATE_RANK = 16
GLA_GATE_TEMP = 16.0
N_EXPERTS = 8
TOP_K = 2

LANES = 128
CHUNK = 128
N_LEVELS = 7
VMEM_LIMIT = 56 * 1024 * 1024

_N_MAIN = 6144
_COL_GG = 6144
_COL_GLR = 7168


def _cparams(sem):
    return pltpu.CompilerParams(dimension_semantics=sem, vmem_limit_bytes=VMEM_LIMIT)


def _sigmoid(x):
    return 1.0 / (1.0 + jnp.exp(-x))


def _rms_rows(x, w):
    ms = jnp.mean(x * x, axis=-1, keepdims=True)
    return x * lax.rsqrt(ms + EPS) * w


def _in_proj_kernel(x_ref, nw_ref, w_ref, wlr_ref, o_ref, lr_ref, xn_ref, *, row_chunk):
    tm = x_ref.shape[0]

    @pl.when(pl.program_id(1) == 0)
    def _():
        def body(c, carry):
            r = pl.multiple_of(c * row_chunk, row_chunk)
            xn_ref[pl.ds(r, row_chunk), :] = _rms_rows(x_ref[pl.ds(r, row_chunk), :], nw_ref[...]).astype(BF16)
            return carry
        lax.fori_loop(0, tm // row_chunk, body, 0)
        lr_ref[...] = lax.dot_general(xn_ref[...], wlr_ref[0].astype(BF16), _NT,
                                      preferred_element_type=F32).astype(lr_ref.dtype)

    o_ref[...] = lax.dot_general(xn_ref[...], w_ref[0].astype(BF16), _NT,
                                 preferred_element_type=F32).astype(o_ref.dtype)


def _in_proj(x, nw, w_t, layer, *, tm=1024, tn=1024):
    n, d = x.shape
    tm = min(tm, n)
    n_main = _N_MAIN // tn
    n_gg = (_COL_GLR - _COL_GG) // tn
    gg0 = _COL_GG + GLA_GATE_RANK

    def w_row(j):
        q, t = GLA_GATE_RANK, tn // GLA_GATE_RANK
        return q * jnp.where(j < n_main, j * t, gg0 // q + (j - n_main) * t)

    slab = lambda rows, imap: pl.BlockSpec((pl.Element(1), pl.Element(rows), pl.Element(d)), imap)
    return pl.pallas_call(
        functools.partial(_in_proj_kernel, row_chunk=128),
        out_shape=[jax.ShapeDtypeStruct((n, (n_main + n_gg) * tn), BF16), jax.ShapeDtypeStruct((n, LANES), BF16)],
        grid=(n // tm, n_main + n_gg),
        in_specs=[
            pl.BlockSpec((tm, d), lambda i, j: (i, 0)),
            pl.BlockSpec((1, d), lambda i, j: (0, 0)),
            slab(tn, lambda i, j: (layer, w_row(j), 0)),
            slab(LANES, lambda i, j: (layer, _COL_GG, 0)),
        ],
        out_specs=[pl.BlockSpec((tm, tn), lambda i, j: (i, j)), pl.BlockSpec((tm, LANES), lambda i, j: (i, 0))],
        scratch_shapes=[pltpu.VMEM((tm, d), BF16)],
        compiler_params=_cparams(("parallel", "arbitrary")),
        name="in_proj",
    )(x, nw.reshape(1, d), w_t, w_t)


def _level_tables():
    c = CHUNK
    idx = np.arange(c)
    masks, signs = [], []
    s = c // 2
    while s >= 1:
        blk = idx // (2 * s)
        upper = (idx % (2 * s)) >= s
        masks.append(((blk[:, None] == blk[None, :]) & upper[:, None] & (~upper)[None, :]).astype(np.float32))
        signs.append(np.broadcast_to(np.where(upper, 1.0, -1.0)[:, None], (c, LANES)).astype(np.float32))
        s //= 2
    masks.append(np.eye(c, dtype=np.float32))
    tri = np.tril(np.ones((c, c), np.float32))
    return np.concatenate([tri, tri], axis=1), np.stack(masks), np.stack(signs)


def _level_exponents(g_cum, signs_ref):
    c, dk = g_cum.shape
    row = lax.broadcasted_iota(jnp.int32, (c, dk), 0)
    out = []
    s = c // 2
    while s >= 4:
        nb = c // (2 * s)
        mid = g_cum.reshape(nb, 2 * s, dk)[:, s - 1:s, :]
        mid = jnp.broadcast_to(mid, (nb, 2 * s, dk)).reshape(c, dk)
        out.append((g_cum - mid) * signs_ref[len(out)])
        s //= 2
    up1 = pltpu.roll(g_cum, 1, 0)
    up2 = pltpu.roll(g_cum, 2, 0)
    dn1 = pltpu.roll(g_cum, c - 1, 0)
    r4 = row % 4
    mid2 = jnp.where(r4 == 0, dn1, jnp.where(r4 == 1, g_cum, jnp.where(r4 == 2, up1, up2)))
    out.append((g_cum - mid2) * signs_ref[len(out)])
    mid1 = jnp.where(row % 2 == 1, up1, g_cum)
    out.append((g_cum - mid1) * signs_ref[len(out)])
    return out


_NT = (((1,), (1,)), ((), ()))
_TN = (((0,), (0,)), ((), ()))


def _chunk_prep(g, tri_ref):
    g_hi = g.astype(BF16)
    g_lo = (g - g_hi.astype(F32)).astype(BF16)
    return jnp.dot(tri_ref[...], jnp.concatenate([g_hi, g_lo], axis=0), preferred_element_type=F32) * LOG2_E


def _chunk_local(q, k, v_bf, g_cum, masks_ref, signs_ref):
    c = CHUNK
    q_bf = q.astype(BF16)
    k_bf = k.astype(BF16)
    a = masks_ref[N_LEVELS] * lax.dot_general(q_bf, k_bf, _NT, preferred_element_type=F32).astype(BF16)
    for l, ex in enumerate(_level_exponents(g_cum, signs_ref)):
        e = jnp.exp2(ex).astype(BF16)
        s = lax.dot_general(q_bf * e, k_bf * e, _NT, preferred_element_type=F32)
        a = a + masks_ref[l] * s.astype(BF16)
    g_last = g_cum[c - 1:c, :]
    o_intra = jnp.dot(a, v_bf, preferred_element_type=F32)
    q_dec = (q * jnp.exp2(g_cum)).astype(BF16)
    k_dec = (k * jnp.exp2(g_last - g_cum)).astype(BF16)
    kv = lax.dot_general(v_bf, k_dec, _TN, preferred_element_type=F32)
    return o_intra, q_dec, kv, jnp.exp2(g_last)


def _head_norm(o, w):
    ms = jnp.mean(o * o, axis=-1, keepdims=True)
    return o * lax.rsqrt(ms + EPS) * w


def _scan_block(n_chunks, gates_fn, gate_fn, nw_ref, tri_ref, masks_ref, signs_ref, o_ref, st_ref):
    gates = [gates_fn(ci) for ci in range(n_chunks)]
    cums = [_chunk_prep(g, tri_ref) for (_, _, _, g) in gates]
    parts = [_chunk_local(q, k, v_bf, g_cum, masks_ref, signs_ref) for (q, k, v_bf, _), g_cum in zip(gates, cums)]
    st = st_ref[...]
    for ci, (o_intra, q_dec, kv, decay) in enumerate(parts):
        o = o_intra + lax.dot_general(q_dec, st.astype(BF16), _NT, preferred_element_type=F32)
        rows = pl.ds(ci * CHUNK, CHUNK)
        o_ref[rows, :] = (_head_norm(o, nw_ref[...]) * gate_fn(rows)).astype(o_ref.dtype)
        st = st * decay + kv
    st_ref[...] = st


def _hgrn_kernel(q_ref, f_ref, i_ref, g_ref, lb_ref, nw_ref, tri_ref, masks_ref, signs_ref, o_ref, st_ref):
    @pl.when(pl.program_id(2) == 0)
    def _():
        st_ref[...] = jnp.zeros_like(st_ref)

    lb = lb_ref[...]

    def local(ci):
        rows = pl.ds(ci * CHUNK, CHUNK)
        fgate = lb + (1.0 - lb) * _sigmoid(f_ref[rows, :].astype(F32))
        g = jnp.log(jnp.maximum(fgate, 1e-38))
        k = 1.0 - fgate
        hq = q_ref[rows, :].astype(F32)
        return hq * _sigmoid(hq), k, i_ref[rows, :], g

    def gate(rows):
        return _sigmoid(g_ref[rows, :].astype(F32))

    _scan_block(q_ref.shape[0] // CHUNK, local, gate, nw_ref, tri_ref, masks_ref, signs_ref, o_ref, st_ref)


def _gla_kernel(q_ref, k_ref, v_ref, g_ref, lr_ref, w2_ref, b_ref, nw_ref, tri_ref, masks_ref, signs_ref, o_ref,
                st_ref):
    @pl.when(pl.program_id(2) == 0)
    def _():
        st_ref[...] = jnp.zeros_like(st_ref)

    def local(ci):
        rows = pl.ds(ci * CHUNK, CHUNK)
        u = jnp.dot(lr_ref[rows, :], w2_ref[...], preferred_element_type=F32) + b_ref[...]
        g = (jnp.minimum(u, 0.0) - jnp.log(1.0 + jnp.exp(-jnp.abs(u)))) * (1.0 / GLA_GATE_TEMP)
        q = q_ref[rows, :].astype(F32) * (GLA_HEAD_K ** -0.5)
        return q, k_ref[rows, :].astype(F32), v_ref[rows, :], g

    def gate(rows):
        gg = g_ref[rows, :].astype(F32)
        return gg * _sigmoid(gg)

    _scan_block(q_ref.shape[0] // CHUNK, local, gate, nw_ref, tri_ref, masks_ref, signs_ref, o_ref, st_ref)


def _const_spec(shape):
    nd = len(shape)
    return pl.BlockSpec(shape, lambda b, h, t: (0,) * nd)


def _hgrn(proj, lb, nw, tri, masks, signs, *, batch, tb=2048):
    n = proj.shape[0]
    nt = n // batch // tb
    dk = HGRN_HEAD_DIM
    hh = HGRN_HEADS

    def col(off):
        return pl.BlockSpec((tb, dk), lambda b, h, t: (b * nt + t, off + h))

    return pl.pallas_call(
        _hgrn_kernel,
        out_shape=jax.ShapeDtypeStruct((n, hh * dk), BF16),
        grid=(batch, hh, nt),
        in_specs=[col(0), col(hh), col(2 * hh), col(3 * hh),
                  pl.BlockSpec((1, dk), lambda b, h, t: (0, h)),
                  pl.BlockSpec((1, dk), lambda b, h, t: (0, h)),
                  _const_spec(tri.shape), _const_spec(masks.shape), _const_spec(signs.shape)],
        out_specs=pl.BlockSpec((tb, dk), lambda b, h, t: (b * nt + t, h)),
        scratch_shapes=[pltpu.VMEM((dk, dk), F32)],
        compiler_params=_cparams(("parallel", "parallel", "arbitrary")),
        name="hgrn",
    )(proj, proj, proj, proj, lb.reshape(1, -1), nw.reshape(1, -1), tri, masks, signs)


def _gla(proj, lr, w2p, b, nw, tri, masks, signs, *, batch, tb=1024):
    n = proj.shape[0]
    nt = n // batch // tb
    dk, dv, hh = GLA_HEAD_K, GLA_HEAD_V, GLA_HEADS
    q0 = 4096 // dk
    k0 = q0 + hh
    v0 = 5120 // dv
    g0 = _COL_GG // dv
    return pl.pallas_call(
        _gla_kernel,
        out_shape=jax.ShapeDtypeStruct((n, hh * dv), BF16),
        grid=(batch, hh, nt),
        in_specs=[pl.BlockSpec((tb, dk), lambda b, h, t: (b * nt + t, q0 + h)),
                  pl.BlockSpec((tb, dk), lambda b, h, t: (b * nt + t, k0 + h)),
                  pl.BlockSpec((tb, dv), lambda b, h, t: (b * nt + t, v0 + h)),
                  pl.BlockSpec((tb, dv), lambda b, h, t: (b * nt + t, g0 + h)),
                  pl.BlockSpec((tb, LANES), lambda b, h, t: (b * nt + t, 0)),
                  pl.BlockSpec((LANES, dk), lambda b, h, t: (0, h)),
                  pl.BlockSpec((1, dk), lambda b, h, t: (0, h)),
                  pl.BlockSpec((1, dv), lambda b, h, t: (0, h)),
                  _const_spec(tri.shape), _const_spec(masks.shape), _const_spec(signs.shape)],
        out_specs=pl.BlockSpec((tb, dv), lambda b, h, t: (b * nt + t, h)),
        scratch_shapes=[pltpu.VMEM((dv, dk), F32)],
        compiler_params=_cparams(("parallel", "parallel", "arbitrary")),
        name="gla",
    )(proj, proj, proj, proj, lr, w2p, b.reshape(1, -1), nw.reshape(1, -1), tri, masks, signs)


def _out_proj_kernel(oh_ref, og_ref, res_ref, w_ref, nw_ref, *rest, with_router):
    if with_router:
        rw_ref, h_ref, hn_ref, lg_ref = rest
    else:
        h_ref, hn_ref = rest

    kh = oh_ref.shape[1]
    acc = jnp.dot(oh_ref[...], w_ref[0:kh, :], preferred_element_type=F32)
    acc = acc + jnp.dot(og_ref[...], w_ref[kh:, :], preferred_element_type=F32)
    h = res_ref[...] + acc
    h_ref[...] = h
    hn = _rms_rows(h, nw_ref[...])
    hn_ref[...] = hn
    if with_router:
        hn_hi = hn.astype(BF16)
        hn_lo = (hn - hn_hi.astype(F32)).astype(BF16)
        rw2 = rw_ref[...]
        l2 = jnp.dot(hn_hi, rw2, preferred_element_type=F32)
        lg_ref[...] = (l2[:, :LANES] + l2[:, LANES:]
                       + jnp.dot(hn_lo, rw2[:, :LANES], preferred_element_type=F32))


def _out_proj(o_h, o_g, res, w_out, layer, nw, router_pad=None):
    n, d = res.shape
    kh, kg = o_h.shape[1], o_g.shape[1]
    with_router = router_pad is not None
    tm = 256 if with_router else 512
    row = lambda i: (i, 0)
    fixed = lambda i: (0, 0)
    in_specs = [pl.BlockSpec((tm, kh), row), pl.BlockSpec((tm, kg), row), pl.BlockSpec((tm, d), row),
                pl.BlockSpec((None, kh + kg, d), lambda i: (layer, 0, 0), pipeline_mode=pl.Buffered(1)),
                pl.BlockSpec((1, d), fixed)]
    args = [o_h, o_g, res, w_out, nw.reshape(1, d)]
    out_shape = [jax.ShapeDtypeStruct((n, d), F32), jax.ShapeDtypeStruct((n, d), F32)]
    out_specs = [pl.BlockSpec((tm, d), row), pl.BlockSpec((tm, d), row)]
    if with_router:
        in_specs.append(pl.BlockSpec((d, 2 * LANES), fixed))
        args.append(router_pad)
        out_shape.append(jax.ShapeDtypeStruct((n, LANES), F32))
        out_specs.append(pl.BlockSpec((tm, LANES), row))
    return pl.pallas_call(
        functools.partial(_out_proj_kernel, with_router=with_router),
        out_shape=out_shape,
        grid=(n // tm,),
        in_specs=in_specs,
        out_specs=out_specs,
        compiler_params=_cparams(("parallel",)),
        name="out_proj",
    )(*args)


def _dispatch_kernel(zs_ref, dest_ref, hn_ref, xs_ref, zero_ref, sem, *, tb, block):
    i = pl.program_id(0)

    def zero_copy(e):
        start = pl.multiple_of(zs_ref[e] * block, block)
        return pltpu.make_async_copy(zero_ref, xs_ref.at[pl.ds(start, block)], sem.at[1])

    @pl.when(i == 0)
    def _():
        zero_ref[...] = jnp.zeros_like(zero_ref)
        for e in range(zs_ref.shape[0]):
            @pl.when(zs_ref[e] >= 0)
            def _():
                zero_copy(e).start()
        for e in range(zs_ref.shape[0]):
            @pl.when(zs_ref[e] >= 0)
            def _():
                zero_copy(e).wait()

    def body(r, carry):
        src = hn_ref.at[pl.ds(r, 1)]
        for k in range(TOP_K):
            pltpu.make_async_copy(src, xs_ref.at[pl.ds(dest_ref[0, TOP_K * r + k], 1)],
                                  sem.at[0]).start(priority=k % 2)
        return carry

    lax.fori_loop(0, tb, body, 0, unroll=8)
    for k in range(TOP_K):
        pltpu.make_async_copy(hn_ref, xs_ref.at[pl.ds(0, tb)], sem.at[0]).wait()


def _dispatch(hn, dest, zero_start, cap, *, block, tb=512):
    n, dw = hn.shape
    tb = min(tb, n)
    grid_spec = pltpu.PrefetchScalarGridSpec(
        num_scalar_prefetch=1,
        grid=(n // tb,),
        in_specs=[pl.BlockSpec((None, 1, TOP_K * tb), lambda i, zs: (i, 0, 0), memory_space=pltpu.SMEM),
                  pl.BlockSpec((tb, dw), lambda i, zs: (i, 0))],
        out_specs=pl.BlockSpec(memory_space=pl.ANY),
        scratch_shapes=[pltpu.VMEM((block, dw), hn.dtype), pltpu.SemaphoreType.DMA((2,))],
    )
    return pl.pallas_call(
        functools.partial(_dispatch_kernel, tb=tb, block=block),
        out_shape=jax.ShapeDtypeStruct((cap, dw), hn.dtype),
        grid_spec=grid_spec,
        compiler_params=_cparams(("arbitrary",)),
        name="dispatch",
    )(zero_start, dest.reshape(n // tb, 1, TOP_K * tb), hn)


def _ffn_kernel(be_ref, nused_ref, nrows_ref, x_ref, wg_ref, wu_ref, wd_ref, o_ref, xb_ref, *, row_chunk):
    i = pl.program_id(0)
    f = pl.program_id(1)
    used = i < nused_ref[0]
    n_chunks = x_ref.shape[0] // row_chunk

    @pl.when(f == 0)
    def _():
        o_ref[...] = jnp.zeros_like(o_ref)

    @pl.when(jnp.logical_and(f == 0, used))
    def _():
        def body(c, carry):
            rows = pl.ds(pl.multiple_of(c * row_chunk, row_chunk), row_chunk)
            xb_ref[rows, :] = x_ref[rows, :].astype(BF16)
            return carry
        lax.fori_loop(0, n_chunks, body, 0)

    live = (nrows_ref[i] + row_chunk - 1) // row_chunk
    for nc in range(1, n_chunks + 1):
        @pl.when(jnp.logical_and(used, live == nc))
        def _():
            rows = pl.ds(0, nc * row_chunk)
            x = xb_ref[rows, :]
            g = jnp.dot(x, wg_ref[0].astype(BF16), preferred_element_type=F32)
            u = jnp.dot(x, wu_ref[0].astype(BF16), preferred_element_type=F32)
            a = (g * _sigmoid(g) * u).astype(BF16)
            o_ref[rows, :] += jnp.dot(a, wd_ref[0].astype(BF16), preferred_element_type=F32)


def _ffn(x, wg, wu, wd, block_e, nused, nrows, *, tm=1024, tf=512):
    r, d = x.shape
    ff = wg.shape[2]
    nf = ff // tf

    def f_eff(i, f, nu):
        return jnp.where(i < nu[0], f, nf - 1)

    grid_spec = pltpu.PrefetchScalarGridSpec(
        num_scalar_prefetch=3,
        grid=(r // tm, nf),
        in_specs=[
            pl.BlockSpec((tm, d), lambda i, f, be, nu, nr: (i, 0), pipeline_mode=pl.Buffered(1)),
            pl.BlockSpec((1, d, tf), lambda i, f, be, nu, nr: (be[i], 0, f_eff(i, f, nu))),
            pl.BlockSpec((1, d, tf), lambda i, f, be, nu, nr: (be[i], 0, f_eff(i, f, nu))),
            pl.BlockSpec((1, tf, d), lambda i, f, be, nu, nr: (be[i], f_eff(i, f, nu), 0)),
        ],
        out_specs=pl.BlockSpec((tm, d), lambda i, f, be, nu, nr: (i, 0), pipeline_mode=pl.Buffered(1)),
        scratch_shapes=[pltpu.VMEM((tm, d), BF16)],
    )
    return pl.pallas_call(
        functools.partial(_ffn_kernel, row_chunk=256),
        out_shape=jax.ShapeDtypeStruct((r, d), F32),
        grid_spec=grid_spec,
        compiler_params=_cparams(("parallel", "arbitrary")),
        name="ffn",
    )(block_e, nused, nrows, x, wg, wu, wd)


def _ple_tail(h2, p_ref, nw_ref, wg_ref, wp_ref, fw_ref, o_ref, final):
    hn = _rms_rows(h2, nw_ref[...]).astype(BF16)
    gate = _sigmoid(jnp.dot(hn, wg_ref[...], preferred_element_type=F32))
    pp = jnp.dot(p_ref[...].astype(BF16), wp_ref[...], preferred_element_type=F32)
    h3 = h2 + gate * pp
    if final:
        h3 = _rms_rows(h3, fw_ref[...])
    o_ref[...] = h3


def _ple_dense_kernel(h_ref, y_ref, p_ref, nw_ref, wg_ref, wp_ref, fw_ref, o_ref, *, final):
    h2 = h_ref[...] + y_ref[...]
    _ple_tail(h2, p_ref, nw_ref, wg_ref, wp_ref, fw_ref, o_ref, final)


def _ple_moe_kernel(dcur_ref, dnext_ref, h_ref, ys_ref, tw_ref, p_ref, nw_ref, wg_ref, wp_ref, fw_ref, o_ref,
                    gbuf_ref, sem, *, final):
    i = pl.program_id(0)
    n_steps = pl.num_programs(0)
    tm = h_ref.shape[0]
    slot = i % 2

    def issue(d_ref, s):
        def body(r, carry):
            for k in range(TOP_K):
                pltpu.make_async_copy(ys_ref.at[pl.ds(d_ref[0, TOP_K * r + k], 1)],
                                      gbuf_ref.at[s, k, pl.ds(r, 1)], sem.at[s]).start(priority=k % 2)
            return carry
        lax.fori_loop(0, tm, body, 0, unroll=8)

    @pl.when(i == 0)
    def _():
        issue(dcur_ref, 0)

    @pl.when(i + 1 < n_steps)
    def _():
        issue(dnext_ref, 1 - slot)

    for k in range(TOP_K):
        pltpu.make_async_copy(ys_ref.at[pl.ds(0, tm)], gbuf_ref.at[slot, k], sem.at[slot]).wait()
    tw = tw_ref[...]
    y = tw[:, 0:1] * gbuf_ref[slot, 0]
    for k in range(1, TOP_K):
        y = y + tw[:, k:k + 1] * gbuf_ref[slot, k]
    _ple_tail(h_ref[...] + y, p_ref, nw_ref, wg_ref, wp_ref, fw_ref, o_ref, final)


def _ple(h1, y, p, layer, nw, w_gate, w_proj, final_w, *, final, dest=None, top_w=None):
    n, d = h1.shape
    pd = p.shape[2]
    tm = 512 if dest is None else 256
    n_steps = n // tm
    row = lambda i: (i, 0)
    fixed = lambda i: (0, 0)
    tail_specs = [pl.BlockSpec((None, tm, pd), lambda i: (layer, i, 0)),
                  pl.BlockSpec((1, d), fixed),
                  pl.BlockSpec((None, d, d), lambda i: (layer, 0, 0), pipeline_mode=pl.Buffered(1)),
                  pl.BlockSpec((None, pd, d), lambda i: (layer, 0, 0), pipeline_mode=pl.Buffered(1)),
                  pl.BlockSpec((1, d), fixed)]
    tail_args = [p, nw.reshape(1, d), w_gate, w_proj, final_w.reshape(1, d)]
    scratch = []
    if dest is None:
        kern = functools.partial(_ple_dense_kernel, final=final)
        in_specs = [pl.BlockSpec((tm, d), row), pl.BlockSpec((tm, d), row)] + tail_specs
        args = [h1, y] + tail_args
    else:
        kern = functools.partial(_ple_moe_kernel, final=final)
        dest3 = dest.reshape(n_steps, 1, TOP_K * tm)
        smem = lambda imap: pl.BlockSpec((None, 1, TOP_K * tm), imap, memory_space=pltpu.SMEM)
        in_specs = [smem(lambda i: (i, 0, 0)), smem(lambda i: (jnp.minimum(i + 1, n_steps - 1), 0, 0)),
                    pl.BlockSpec((tm, d), row), pl.BlockSpec(memory_space=pl.ANY),
                    pl.BlockSpec((tm, TOP_K), row)] + tail_specs
        args = [dest3, dest3, h1, y, top_w] + tail_args
        scratch = scratch + [pltpu.VMEM((2, TOP_K, tm, d), F32), pltpu.SemaphoreType.DMA((2,))]
    return pl.pallas_call(
        kern,
        out_shape=jax.ShapeDtypeStruct((n, d), F32),
        grid=(n_steps,),
        in_specs=in_specs,
        out_specs=pl.BlockSpec((tm, d), row),
        scratch_shapes=scratch,
        compiler_params=_cparams(("arbitrary",)),
        name="ple",
    )(*args)


def _route(logits, block):
    n = logits.shape[0]
    top_logit, top_e = lax.top_k(logits, TOP_K)
    top_w = jax.nn.softmax(top_logit, axis=-1)
    flat_e = top_e.reshape(-1)
    onehot = (flat_e[:, None] == jnp.arange(N_EXPERTS, dtype=flat_e.dtype)[None, :]).astype(jnp.int32)
    csum = jnp.cumsum(onehot, axis=0)
    counts = csum[-1]
    rank = jnp.sum((csum - onehot) * onehot, axis=1)
    padded = (counts + block - 1) // block * block
    padded_end = jnp.cumsum(padded)
    padded_start = padded_end - padded
    dest = (padded_start[flat_e] + rank).astype(jnp.int32)
    n_blocks = -(-(n * TOP_K) // block) + N_EXPERTS
    blk = jnp.arange(n_blocks, dtype=jnp.int32)
    block_e = jnp.minimum(jnp.sum((blk[:, None] * block >= padded_end[None, :]).astype(jnp.int32), axis=1),
                          N_EXPERTS - 1)
    nused = (padded_end[-1] // block).astype(jnp.int32).reshape(1)
    block_e = jnp.where(jnp.arange(n_blocks) < nused[0], block_e, block_e[jnp.maximum(nused[0] - 1, 0)])
    nrows = jnp.clip(counts[block_e] - (blk * block - padded_start[block_e]), 0, block).astype(jnp.int32)
    tail = nused[0] + jnp.arange(N_EXPERTS, dtype=jnp.int32)
    zero_start = jnp.concatenate([jnp.where(padded > 0, padded_end // block - 1, -1),
                                  jnp.where(tail < n_blocks, tail, -1)]).astype(jnp.int32)
    return dest.reshape(n, TOP_K), top_w, block_e, nused, nrows, zero_start, n_blocks * block


def kernel(x, p, norm_mix_w, w_in, hgrn_lb_logits, hgrn_norm_w, gla_gate_w2, gla_gate_b, gla_norm_w, w_out,
           norm_ffn_w, dense_w_gate, dense_w_up, dense_w_down, moe_router, moe_w_gate, moe_w_up, moe_w_down,
           norm_ple_w, ple_w_gate, ple_w_proj, final_norm_w):
    batch, seq, d = x.shape
    depth = w_in.shape[0]
    n = batch * seq
    moe_block = 1024

    lbs = jnp.cumsum(jax.nn.softmax(hgrn_lb_logits.astype(F32), axis=0), axis=0)
    lbs = lbs - lbs[0]
    tri_np, masks_np, signs_np = _level_tables()
    tri = jnp.asarray(tri_np, BF16)
    masks = jnp.asarray(masks_np, BF16)
    signs = jnp.asarray(signs_np, F32)

    h = x.reshape(n, d)
    p3 = p.reshape(depth, n, -1)
    moe_wg = moe_w_gate.reshape((-1,) + moe_w_gate.shape[2:])
    moe_wu = moe_w_up.reshape((-1,) + moe_w_up.shape[2:])
    moe_wd = moe_w_down.reshape((-1,) + moe_w_down.shape[2:])
    w_in_t = jnp.swapaxes(w_in, 1, 2)
    w_out_bf = w_out.astype(BF16)
    ple_wg_bf = ple_w_gate.astype(BF16)
    ple_wp_bf = ple_w_proj.astype(BF16)
    for i in range(depth):
        proj, lr = _in_proj(h, norm_mix_w[i], w_in_t, i)
        o_h = _hgrn(proj, lbs[i], hgrn_norm_w[i], tri, masks, signs, batch=batch)
        w2p = jnp.zeros((LANES, gla_gate_w2.shape[2]), F32).at[:GLA_GATE_RANK].set(gla_gate_w2[i]).astype(BF16)
        o_g = _gla(proj, lr, w2p, gla_gate_b[i], gla_norm_w[i], tri, masks, signs, batch=batch)
        j = i // 2
        last = i == depth - 1
        if i % 2 == 0:
            h1, hn = _out_proj(o_h, o_g, h, w_out_bf, i, norm_ffn_w[i])
            n_blk = n // moe_block
            y = _ffn(hn, dense_w_gate, dense_w_up, dense_w_down,
                     jnp.full((n_blk,), j, jnp.int32), jnp.full((1,), n_blk, jnp.int32),
                     jnp.full((n_blk,), moe_block, jnp.int32), tm=moe_block)
            h = _ple(h1, y, p3, i, norm_ple_w[i], ple_wg_bf, ple_wp_bf, final_norm_w, final=last)
        else:
            rw = jnp.zeros((d, LANES), F32).at[:, :N_EXPERTS].set(moe_router[j])
            rw_hi = rw.astype(BF16)
            router2 = jnp.concatenate([rw_hi, (rw - rw_hi.astype(F32)).astype(BF16)], axis=1)
            h1, hn, logits = _out_proj(o_h, o_g, h, w_out_bf, i, norm_ffn_w[i], router2)
            dest, top_w, block_e, nused, nrows, zero_start, cap = _route(logits[:, :N_EXPERTS], moe_block)
            xs = _dispatch(hn, dest, zero_start, cap, block=moe_block)
            ys = _ffn(xs, moe_wg, moe_wu, moe_wd, block_e + j * N_EXPERTS, nused, nrows, tm=moe_block)
            h = _ple(h1, ys, p3, i, norm_ple_w[i], ple_wg_bf, ple_wp_bf, final_norm_w, final=last,
                     dest=dest, top_w=top_w)
    return h.reshape(batch, seq, d)
```

```python
import functools

import numpy as np
import jax
import jax.numpy as jnp
from jax import lax
from jax.experimental import pallas as pl
from jax.experimental.pallas import tpu as pltpu

F32 = jnp.float32
BF16 = jnp.bfloat16

EPS = 1e-6
LOG2_E = 1.4426950408889634
HGRN_HEADS = 8
HGRN_HEAD_DIM = 128
GLA_HEADS = 4
GLA_HEAD_K = 128
GLA_HEAD_V = 256
GLA_GATE_RANK = 16
GLA_GATE_TEMP = 16.0
N_EXPERTS = 8
TOP_K = 2

LANES = 128
CHUNK = 128
N_LEVELS = 7
VMEM_LIMIT = 56 * 1024 * 1024

_N_MAIN = 6144
_COL_GG = 6144
_COL_GLR = 7168


def _cparams(sem):
    return pltpu.CompilerParams(dimension_semantics=sem, vmem_limit_bytes=VMEM_LIMIT)


def _sigmoid(x):
    return 1.0 / (1.0 + jnp.exp(-x))


def _rms_rows(x, w):
    ms = jnp.mean(x * x, axis=-1, keepdims=True)
    return x * lax.rsqrt(ms + EPS) * w


def _in_proj_kernel(x_ref, nw_ref, w_ref, wlr_ref, o_ref, lr_ref, xn_ref, *, row_chunk):
    tm = x_ref.shape[0]

    @pl.when(pl.program_id(1) == 0)
    def _():
        def body(c, carry):
            r = pl.multiple_of(c * row_chunk, row_chunk)
            xn_ref[pl.ds(r, row_chunk), :] = _rms_rows(x_ref[pl.ds(r, row_chunk), :], nw_ref[...]).astype(BF16)
            return carry
        lax.fori_loop(0, tm // row_chunk, body, 0)
        lr_ref[...] = lax.dot_general(xn_ref[...], wlr_ref[0].astype(BF16), _NT,
                                      preferred_element_type=F32).astype(lr_ref.dtype)

    o_ref[...] = lax.dot_general(xn_ref[...], w_ref[0].astype(BF16), _NT,
                                 preferred_element_type=F32).astype(o_ref.dtype)


def _in_proj(x, nw, w_t, layer, *, tm=1024, tn=1024):
    n, d = x.shape
    tm = min(tm, n)
    n_main = _N_MAIN // tn
    n_gg = (_COL_GLR - _COL_GG) // tn
    gg0 = _COL_GG + GLA_GATE_RANK

    def w_row(j):
        q, t = GLA_GATE_RANK, tn // GLA_GATE_RANK
        return q * jnp.where(j < n_main, j * t, gg0 // q + (j - n_main) * t)

    slab = lambda rows, imap: pl.BlockSpec((pl.Element(1), pl.Element(rows), pl.Element(d)), imap)
    return pl.pallas_call(
        functools.partial(_in_proj_kernel, row_chunk=128),
        out_shape=[jax.ShapeDtypeStruct((n, (n_main + n_gg) * tn), BF16), jax.ShapeDtypeStruct((n, LANES), BF16)],
        grid=(n // tm, n_main + n_gg),
        in_specs=[
            pl.BlockSpec((tm, d), lambda i, j: (i, 0)),
            pl.BlockSpec((1, d), lambda i, j: (0, 0)),
            slab(tn, lambda i, j: (layer, w_row(j), 0)),
            slab(LANES, lambda i, j: (layer, _COL_GG, 0)),
        ],
        out_specs=[pl.BlockSpec((tm, tn), lambda i, j: (i, j)), pl.BlockSpec((tm, LANES), lambda i, j: (i, 0))],
        scratch_shapes=[pltpu.VMEM((tm, d), BF16)],
        compiler_params=_cparams(("parallel", "arbitrary")),
        name="in_proj",
    )(x, nw.reshape(1, d), w_t, w_t)


def _level_tables():
    c = CHUNK
    idx = np.arange(c)
    masks, signs = [], []
    s = c // 2
    while s >= 1:
        blk = idx // (2 * s)
        upper = (idx % (2 * s)) >= s
        masks.append(((blk[:, None] == blk[None, :]) & upper[:, None] & (~upper)[None, :]).astype(np.float32))
        signs.append(np.broadcast_to(np.where(upper, 1.0, -1.0)[:, None], (c, LANES)).astype(np.float32))
        s //= 2
    masks.append(np.eye(c, dtype=np.float32))
    tri = np.tril(np.ones((c, c), np.float32))
    return np.concatenate([tri, tri], axis=1), np.stack(masks), np.stack(signs)


def _level_exponents(g_cum, signs_ref):
    c, dk = g_cum.shape
    row = lax.broadcasted_iota(jnp.int32, (c, dk), 0)
    out = []
    s = c // 2
    while s >= 4:
        nb = c // (2 * s)
        mid = g_cum.reshape(nb, 2 * s, dk)[:, s - 1:s, :]
        mid = jnp.broadcast_to(mid, (nb, 2 * s, dk)).reshape(c, dk)
        out.append((g_cum - mid) * signs_ref[len(out)])
        s //= 2
    up1 = pltpu.roll(g_cum, 1, 0)
    up2 = pltpu.roll(g_cum, 2, 0)
    dn1 = pltpu.roll(g_cum, c - 1, 0)
    r4 = row % 4
    mid2 = jnp.where(r4 == 0, dn1, jnp.where(r4 == 1, g_cum, jnp.where(r4 == 2, up1, up2)))
    out.append((g_cum - mid2) * signs_ref[len(out)])
    mid1 = jnp.where(row % 2 == 1, up1, g_cum)
    out.append((g_cum - mid1) * signs_ref[len(out)])
    return out


_NT = (((1,), (1,)), ((), ()))
_TN = (((0,), (0,)), ((), ()))


def _chunk_prep(g, tri_ref):
    g_hi = g.astype(BF16)
    g_lo = (g - g_hi.astype(F32)).astype(BF16)
    return jnp.dot(tri_ref[...], jnp.concatenate([g_hi, g_lo], axis=0), preferred_element_type=F32) * LOG2_E


def _chunk_local(q, k, v_bf, g_cum, masks_ref, signs_ref):
    c = CHUNK
    q_bf = q.astype(BF16)
    k_bf = k.astype(BF16)
    a = masks_ref[N_LEVELS] * lax.dot_general(q_bf, k_bf, _NT, preferred_element_type=F32).astype(BF16)
    for l, ex in enumerate(_level_exponents(g_cum, signs_ref)):
        e = jnp.exp2(ex).astype(BF16)
        s = lax.dot_general(q_bf * e, k_bf * e, _NT, preferred_element_type=F32)
        a = a + masks_ref[l] * s.astype(BF16)
    g_last = g_cum[c - 1:c, :]
    o_intra = jnp.dot(a, v_bf, preferred_element_type=F32)
    q_dec = (q * jnp.exp2(g_cum)).astype(BF16)
    k_dec = (k * jnp.exp2(g_last - g_cum)).astype(BF16)
    kv = lax.dot_general(v_bf, k_dec, _TN, preferred_element_type=F32)
    return o_intra, q_dec, kv, jnp.exp2(g_last)


def _head_norm(o, w):
    ms = jnp.mean(o * o, axis=-1, keepdims=True)
    return o * lax.rsqrt(ms + EPS) * w


def _scan_block(n_chunks, gates_fn, gate_fn, nw_ref, tri_ref, masks_ref, signs_ref, o_ref, st_ref):
    gates = [gates_fn(ci) for ci in range(n_chunks)]
    cums = [_chunk_prep(g, tri_ref) for (_, _, _, g) in gates]
    parts = [_chunk_local(q, k, v_bf, g_cum, masks_ref, signs_ref) for (q, k, v_bf, _), g_cum in zip(gates, cums)]
    st = st_ref[...]
    for ci, (o_intra, q_dec, kv, decay) in enumerate(parts):
        o = o_intra + lax.dot_general(q_dec, st.astype(BF16), _NT, preferred_element_type=F32)
        rows = pl.ds(ci * CHUNK, CHUNK)
        o_ref[rows, :] = (_head_norm(o, nw_ref[...]) * gate_fn(rows)).astype(o_ref.dtype)
        st = st * decay + kv
    st_ref[...] = st


def _hgrn_kernel(q_ref, f_ref, i_ref, g_ref, lb_ref, nw_ref, tri_ref, masks_ref, signs_ref, o_ref, st_ref):
    @pl.when(pl.program_id(2) == 0)
    def _():
        st_ref[...] = jnp.zeros_like(st_ref)

    lb = lb_ref[...]

    def local(ci):
        rows = pl.ds(ci * CHUNK, CHUNK)
        fgate = lb + (1.0 - lb) * _sigmoid(f_ref[rows, :].astype(F32))
        g = jnp.log(jnp.maximum(fgate, 1e-38))
        k = 1.0 - fgate
        hq = q_ref[rows, :].astype(F32)
        return hq * _sigmoid(hq), k, i_ref[rows, :], g

    def gate(rows):
        return _sigmoid(g_ref[rows, :].astype(F32))

    _scan_block(q_ref.shape[0] // CHUNK, local, gate, nw_ref, tri_ref, masks_ref, signs_ref, o_ref, st_ref)


def _gla_kernel(q_ref, k_ref, v_ref, g_ref, lr_ref, w2_ref, b_ref, nw_ref, tri_ref, masks_ref, signs_ref, o_ref,
                st_ref):
    @pl.when(pl.program_id(2) == 0)
    def _():
        st_ref[...] = jnp.zeros_like(st_ref)

    def local(ci):
        rows = pl.ds(ci * CHUNK, CHUNK)
        u = jnp.dot(lr_ref[rows, :], w2_ref[...], preferred_element_type=F32) + b_ref[...]
        g = (jnp.minimum(u, 0.0) - jnp.log(1.0 + jnp.exp(-jnp.abs(u)))) * (1.0 / GLA_GATE_TEMP)
        q = q_ref[rows, :].astype(F32) * (GLA_HEAD_K ** -0.5)
        return q, k_ref[rows, :].astype(F32), v_ref[rows, :], g

    def gate(rows):
        gg = g_ref[rows, :].astype(F32)
        return gg * _sigmoid(gg)

    _scan_block(q_ref.shape[0] // CHUNK, local, gate, nw_ref, tri_ref, masks_ref, signs_ref, o_ref, st_ref)


def _const_spec(shape):
    nd = len(shape)
    return pl.BlockSpec(shape, lambda b, h, t: (0,) * nd)


def _hgrn(proj, lb, nw, tri, masks, signs, *, batch, tb=2048):
    n = proj.shape[0]
    nt = n // batch // tb
    dk = HGRN_HEAD_DIM
    hh = HGRN_HEADS

    def col(off):
        return pl.BlockSpec((tb, dk), lambda b, h, t: (b * nt + t, off + h))

    return pl.pallas_call(
        _hgrn_kernel,
        out_shape=jax.ShapeDtypeStruct((n, hh * dk), BF16),
        grid=(batch, hh, nt),
        in_specs=[col(0), col(hh), col(2 * hh), col(3 * hh),
                  pl.BlockSpec((1, dk), lambda b, h, t: (0, h)),
                  pl.BlockSpec((1, dk), lambda b, h, t: (0, h)),
                  _const_spec(tri.shape), _const_spec(masks.shape), _const_spec(signs.shape)],
        out_specs=pl.BlockSpec((tb, dk), lambda b, h, t: (b * nt + t, h)),
        scratch_shapes=[pltpu.VMEM((dk, dk), F32)],
        compiler_params=_cparams(("parallel", "parallel", "arbitrary")),
        name="hgrn",
    )(proj, proj, proj, proj, lb.reshape(1, -1), nw.reshape(1, -1), tri, masks, signs)


def _gla(proj, lr, w2p, b, nw, tri, masks, signs, *, batch, tb=1024):
    n = proj.shape[0]
    nt = n // batch // tb
    dk, dv, hh = GLA_HEAD_K, GLA_HEAD_V, GLA_HEADS
    q0 = 4096 // dk
    k0 = q0 + hh
    v0 = 5120 // dv
    g0 = _COL_GG // dv
    return pl.pallas_call(
        _gla_kernel,
        out_shape=jax.ShapeDtypeStruct((n, hh * dv), BF16),
        grid=(batch, hh, nt),
        in_specs=[pl.BlockSpec((tb, dk), lambda b, h, t: (b * nt + t, q0 + h)),
                  pl.BlockSpec((tb, dk), lambda b, h, t: (b * nt + t, k0 + h)),
                  pl.BlockSpec((tb, dv), lambda b, h, t: (b * nt + t, v0 + h)),
                  pl.BlockSpec((tb, dv), lambda b, h, t: (b * nt + t, g0 + h)),
                  pl.BlockSpec((tb, LANES), lambda b, h, t: (b * nt + t, 0)),
                  pl.BlockSpec((LANES, dk), lambda b, h, t: (0, h)),
                  pl.BlockSpec((1, dk), lambda b, h, t: (0, h)),
                  pl.BlockSpec((1, dv), lambda b, h, t: (0, h)),
                  _const_spec(tri.shape), _const_spec(masks.shape), _const_spec(signs.shape)],
        out_specs=pl.BlockSpec((tb, dv), lambda b, h, t: (b * nt + t, h)),
        scratch_shapes=[pltpu.VMEM((dv, dk), F32)],
        compiler_params=_cparams(("parallel", "parallel", "arbitrary")),
        name="gla",
    )(proj, proj, proj, proj, lr, w2p, b.reshape(1, -1), nw.reshape(1, -1), tri, masks, signs)


def _out_proj_kernel(oh_ref, og_ref, res_ref, w_ref, nw_ref, *rest, with_router):
    if with_router:
        rw_ref, h_ref, hn_ref, lg_ref = rest
    else:
        h_ref, hn_ref = rest

    kh = oh_ref.shape[1]
    acc = jnp.dot(oh_ref[...], w_ref[0:kh, :], preferred_element_type=F32)
    acc = acc + jnp.dot(og_ref[...], w_ref[kh:, :], preferred_element_type=F32)
    h = res_ref[...] + acc
    h_ref[...] = h
    hn = _rms_rows(h, nw_ref[...])
    hn_ref[...] = hn.astype(hn_ref.dtype)
    if with_router:
        hn_hi = hn.astype(BF16)
        hn_lo = (hn - hn_hi.astype(F32)).astype(BF16)
        rw2 = rw_ref[...]
        l2 = jnp.dot(hn_hi, rw2, preferred_element_type=F32)
        lg_ref[...] = (l2[:, :LANES] + l2[:, LANES:]
                       + jnp.dot(hn_lo, rw2[:, :LANES], preferred_element_type=F32))


def _out_proj(o_h, o_g, res, w_out, layer, nw, router_pad=None, *, hn_dtype=F32):
    n, d = res.shape
    kh, kg = o_h.shape[1], o_g.shape[1]
    with_router = router_pad is not None
    tm = 256 if with_router else 512
    row = lambda i: (i, 0)
    fixed = lambda i: (0, 0)
    in_specs = [pl.BlockSpec((tm, kh), row), pl.BlockSpec((tm, kg), row), pl.BlockSpec((tm, d), row),
                pl.BlockSpec((None, kh + kg, d), lambda i: (layer, 0, 0), pipeline_mode=pl.Buffered(1)),
                pl.BlockSpec((1, d), fixed)]
    args = [o_h, o_g, res, w_out, nw.reshape(1, d)]
    out_shape = [jax.ShapeDtypeStruct((n, d), F32), jax.ShapeDtypeStruct((n, d), hn_dtype)]
    out_specs = [pl.BlockSpec((tm, d), row), pl.BlockSpec((tm, d), row)]
    if with_router:
        in_specs.append(pl.BlockSpec((d, 2 * LANES), fixed))
        args.append(router_pad)
        out_shape.append(jax.ShapeDtypeStruct((n, LANES), F32))
        out_specs.append(pl.BlockSpec((tm, LANES), row))
    return pl.pallas_call(
        functools.partial(_out_proj_kernel, with_router=with_router),
        out_shape=out_shape,
        grid=(n // tm,),
        in_specs=in_specs,
        out_specs=out_specs,
        compiler_params=_cparams(("parallel",)),
        name="out_proj",
    )(*args)


def _dispatch_kernel(zs_ref, dest_ref, hn_ref, xs_ref, zero_ref, sem, *, tb, block):
    i = pl.program_id(0)

    def zero_copy(e):
        start = pl.multiple_of(zs_ref[e] * block, block)
        return pltpu.make_async_copy(zero_ref, xs_ref.at[pl.ds(start, block)], sem.at[1])

    @pl.when(i == 0)
    def _():
        zero_ref[...] = jnp.zeros_like(zero_ref)
        for e in range(zs_ref.shape[0]):
            @pl.when(zs_ref[e] >= 0)
            def _():
                zero_copy(e).start()
        for e in range(zs_ref.shape[0]):
            @pl.when(zs_ref[e] >= 0)
            def _():
                zero_copy(e).wait()

    def body(r, carry):
        src = hn_ref.at[pl.ds(r, 1)]
        for k in range(TOP_K):
            pltpu.make_async_copy(src, xs_ref.at[pl.ds(dest_ref[0, TOP_K * r + k], 1)],
                                  sem.at[0]).start(priority=k % 2)
        return carry

    lax.fori_loop(0, tb, body, 0, unroll=8)
    for k in range(TOP_K):
        pltpu.make_async_copy(hn_ref, xs_ref.at[pl.ds(0, tb)], sem.at[0]).wait()


def _dispatch(hn, dest, zero_start, cap, *, block, tb=512):
    n, dw = hn.shape
    tb = min(tb, n)
    grid_spec = pltpu.PrefetchScalarGridSpec(
        num_scalar_prefetch=1,
        grid=(n // tb,),
        in_specs=[pl.BlockSpec((None, 1, TOP_K * tb), lambda i, zs: (i, 0, 0), memory_space=pltpu.SMEM),
                  pl.BlockSpec((tb, dw), lambda i, zs: (i, 0))],
        out_specs=pl.BlockSpec(memory_space=pl.ANY),
        scratch_shapes=[pltpu.VMEM((block, dw), hn.dtype), pltpu.SemaphoreType.DMA((2,))],
    )
    return pl.pallas_call(
        functools.partial(_dispatch_kernel, tb=tb, block=block),
        out_shape=jax.ShapeDtypeStruct((cap, dw), hn.dtype),
        grid_spec=grid_spec,
        compiler_params=_cparams(("arbitrary",)),
        name="dispatch",
    )(zero_start, dest.reshape(n // tb, 1, TOP_K * tb), hn)


def _ffn_kernel(be_ref, nused_ref, nrows_ref, x_ref, wg_ref, wu_ref, wd_ref, o_ref, *scratch, row_chunk):
    i = pl.program_id(0)
    f = pl.program_id(1)
    used = i < nused_ref[0]
    n_chunks = x_ref.shape[0] // row_chunk
    xb_ref = scratch[0] if scratch else x_ref

    @pl.when(f == 0)
    def _():
        o_ref[...] = jnp.zeros_like(o_ref)

    if scratch:
        @pl.when(jnp.logical_and(f == 0, used))
        def _():
            def body(c, carry):
                rows = pl.ds(pl.multiple_of(c * row_chunk, row_chunk), row_chunk)
                xb_ref[rows, :] = x_ref[rows, :].astype(BF16)
                return carry
            lax.fori_loop(0, n_chunks, body, 0)

    live = (nrows_ref[i] + row_chunk - 1) // row_chunk
    for nc in range(1, n_chunks + 1):
        @pl.when(jnp.logical_and(used, live == nc))
        def _():
            rows = pl.ds(0, nc * row_chunk)
            x = xb_ref[rows, :]
            g = jnp.dot(x, wg_ref[0].astype(BF16), preferred_element_type=F32)
            u = jnp.dot(x, wu_ref[0].astype(BF16), preferred_element_type=F32)
            a = (g * _sigmoid(g) * u).astype(BF16)
            o_ref[rows, :] += jnp.dot(a, wd_ref[0].astype(BF16), preferred_element_type=F32)


def _ffn(x, wg, wu, wd, block_e, nused, nrows, *, tm=1024, tf=512):
    r, d = x.shape
    cast_x = x.dtype != BF16
    ff = wg.shape[2]
    nf = ff // tf

    def f_eff(i, f, nu):
        return jnp.where(i < nu[0], f, nf - 1)

    grid_spec = pltpu.PrefetchScalarGridSpec(
        num_scalar_prefetch=3,
        grid=(r // tm, nf),
        in_specs=[
            pl.BlockSpec((tm, d), lambda i, f, be, nu, nr: (i, 0),
                         pipeline_mode=pl.Buffered(1) if cast_x else None),
            pl.BlockSpec((1, d, tf), lambda i, f, be, nu, nr: (be[i], 0, f_eff(i, f, nu))),
            pl.BlockSpec((1, d, tf), lambda i, f, be, nu, nr: (be[i], 0, f_eff(i, f, nu))),
            pl.BlockSpec((1, tf, d), lambda i, f, be, nu, nr: (be[i], f_eff(i, f, nu), 0)),
        ],
        out_specs=pl.BlockSpec((tm, d), lambda i, f, be, nu, nr: (i, 0), pipeline_mode=pl.Buffered(1)),
        scratch_shapes=[pltpu.VMEM((tm, d), BF16)] if cast_x else [],
    )
    return pl.pallas_call(
        functools.partial(_ffn_kernel, row_chunk=256),
        out_shape=jax.ShapeDtypeStruct((r, d), F32),
        grid_spec=grid_spec,
        compiler_params=_cparams(("parallel", "arbitrary")),
        name="ffn",
    )(block_e, nused, nrows, x, wg, wu, wd)


def _ple_tail(h2, p_ref, nw_ref, wg_ref, wp_ref, fw_ref, o_ref, final):
    hn = _rms_rows(h2, nw_ref[...]).astype(BF16)
    gate = _sigmoid(jnp.dot(hn, wg_ref[...], preferred_element_type=F32))
    pp = jnp.dot(p_ref[...].astype(BF16), wp_ref[...], preferred_element_type=F32)
    h3 = h2 + gate * pp
    if final:
        h3 = _rms_rows(h3, fw_ref[...])
    o_ref[...] = h3


def _ple_dense_kernel(h_ref, y_ref, p_ref, nw_ref, wg_ref, wp_ref, fw_ref, o_ref, *, final):
    h2 = h_ref[...] + y_ref[...]
    _ple_tail(h2, p_ref, nw_ref, wg_ref, wp_ref, fw_ref, o_ref, final)


def _ple_moe_kernel(dcur_ref, dnext_ref, h_ref, ys_ref, tw_ref, p_ref, nw_ref, wg_ref, wp_ref, fw_ref, o_ref,
                    gbuf_ref, sem, *, final):
    i = pl.program_id(0)
    n_steps = pl.num_programs(0)
    tm = h_ref.shape[0]
    slot = i % 2

    def issue(d_ref, s):
        def body(r, carry):
            for k in range(TOP_K):
                pltpu.make_async_copy(ys_ref.at[pl.ds(d_ref[0, TOP_K * r + k], 1)],
                                      gbuf_ref.at[s, k, pl.ds(r, 1)], sem.at[s]).start(priority=k % 2)
            return carry
        lax.fori_loop(0, tm, body, 0, unroll=8)

    @pl.when(i == 0)
    def _():
        issue(dcur_ref, 0)

    @pl.when(i + 1 < n_steps)
    def _():
        issue(dnext_ref, 1 - slot)

    for k in range(TOP_K):
        pltpu.make_async_copy(ys_ref.at[pl.ds(0, tm)], gbuf_ref.at[slot, k], sem.at[slot]).wait()
    tw = tw_ref[...]
    y = tw[:, 0:1] * gbuf_ref[slot, 0]
    for k in range(1, TOP_K):
        y = y + tw[:, k:k + 1] * gbuf_ref[slot, k]
    _ple_tail(h_ref[...] + y, p_ref, nw_ref, wg_ref, wp_ref, fw_ref, o_ref, final)


def _ple(h1, y, p, layer, nw, w_gate, w_proj, final_w, *, final, dest=None, top_w=None):
    n, d = h1.shape
    pd = p.shape[2]
    tm = 512 if dest is None else 256
    n_steps = n // tm
    row = lambda i: (i, 0)
    fixed = lambda i: (0, 0)
    tail_specs = [pl.BlockSpec((None, tm, pd), lambda i: (layer, i, 0)),
                  pl.BlockSpec((1, d), fixed),
                  pl.BlockSpec((None, d, d), lambda i: (layer, 0, 0), pipeline_mode=pl.Buffered(1)),
                  pl.BlockSpec((None, pd, d), lambda i: (layer, 0, 0), pipeline_mode=pl.Buffered(1)),
                  pl.BlockSpec((1, d), fixed)]
    tail_args = [p, nw.reshape(1, d), w_gate, w_proj, final_w.reshape(1, d)]
    scratch = []
    if dest is None:
        kern = functools.partial(_ple_dense_kernel, final=final)
        in_specs = [pl.BlockSpec((tm, d), row), pl.BlockSpec((tm, d), row)] + tail_specs
        args = [h1, y] + tail_args
    else:
        kern = functools.partial(_ple_moe_kernel, final=final)
        dest3 = dest.reshape(n_steps, 1, TOP_K * tm)
        smem = lambda imap: pl.BlockSpec((None, 1, TOP_K * tm), imap, memory_space=pltpu.SMEM)
        in_specs = [smem(lambda i: (i, 0, 0)), smem(lambda i: (jnp.minimum(i + 1, n_steps - 1), 0, 0)),
                    pl.BlockSpec((tm, d), row), pl.BlockSpec(memory_space=pl.ANY),
                    pl.BlockSpec((tm, TOP_K), row)] + tail_specs
        args = [dest3, dest3, h1, y, top_w] + tail_args
        scratch = scratch + [pltpu.VMEM((2, TOP_K, tm, d), F32), pltpu.SemaphoreType.DMA((2,))]
    return pl.pallas_call(
        kern,
        out_shape=jax.ShapeDtypeStruct((n, d), F32),
        grid=(n_steps,),
        in_specs=in_specs,
        out_specs=pl.BlockSpec((tm, d), row),
        scratch_shapes=scratch,
        compiler_params=_cparams(("arbitrary",)),
        name="ple",
    )(*args)


def _route(logits, block):
    n = logits.shape[0]
    top_logit, top_e = lax.top_k(logits, TOP_K)
    top_w = jax.nn.softmax(top_logit, axis=-1)
    flat_e = top_e.reshape(-1)
    onehot = (flat_e[:, None] == jnp.arange(N_EXPERTS, dtype=flat_e.dtype)[None, :]).astype(jnp.int32)
    csum = jnp.cumsum(onehot, axis=0)
    counts = csum[-1]
    rank = jnp.sum((csum - onehot) * onehot, axis=1)
    padded = (counts + block - 1) // block * block
    padded_end = jnp.cumsum(padded)
    padded_start = padded_end - padded
    dest = (padded_start[flat_e] + rank).astype(jnp.int32)
    n_blocks = -(-(n * TOP_K) // block) + N_EXPERTS
    blk = jnp.arange(n_blocks, dtype=jnp.int32)
    block_e = jnp.minimum(jnp.sum((blk[:, None] * block >= padded_end[None, :]).astype(jnp.int32), axis=1),
                          N_EXPERTS - 1)
    nused = (padded_end[-1] // block).astype(jnp.int32).reshape(1)
    block_e = jnp.where(jnp.arange(n_blocks) < nused[0], block_e, block_e[jnp.maximum(nused[0] - 1, 0)])
    nrows = jnp.clip(counts[block_e] - (blk * block - padded_start[block_e]), 0, block).astype(jnp.int32)
    tail = nused[0] + jnp.arange(N_EXPERTS, dtype=jnp.int32)
    zero_start = jnp.concatenate([jnp.where(padded > 0, padded_end // block - 1, -1),
                                  jnp.where(tail < n_blocks, tail, -1)]).astype(jnp.int32)
    return dest.reshape(n, TOP_K), top_w, block_e, nused, nrows, zero_start, n_blocks * block


def kernel(x, p, norm_mix_w, w_in, hgrn_lb_logits, hgrn_norm_w, gla_gate_w2, gla_gate_b, gla_norm_w, w_out,
           norm_ffn_w, dense_w_gate, dense_w_up, dense_w_down, moe_router, moe_w_gate, moe_w_up, moe_w_down,
           norm_ple_w, ple_w_gate, ple_w_proj, final_norm_w):
    batch, seq, d = x.shape
    depth = w_in.shape[0]
    n = batch * seq
    moe_block = 1024

    lbs = jnp.cumsum(jax.nn.softmax(hgrn_lb_logits.astype(F32), axis=0), axis=0)
    lbs = lbs - lbs[0]
    tri_np, masks_np, signs_np = _level_tables()
    tri = jnp.asarray(tri_np, BF16)
    masks = jnp.asarray(masks_np, BF16)
    signs = jnp.asarray(signs_np, F32)

    h = x.reshape(n, d)
    p3 = p.reshape(depth, n, -1)
    moe_wg = moe_w_gate.reshape((-1,) + moe_w_gate.shape[2:])
    moe_wu = moe_w_up.reshape((-1,) + moe_w_up.shape[2:])
    moe_wd = moe_w_down.reshape((-1,) + moe_w_down.shape[2:])
    w_in_t = jnp.swapaxes(w_in, 1, 2)
    w_out_bf = w_out.astype(BF16)
    ple_wg_bf = ple_w_gate.astype(BF16)
    ple_wp_bf = ple_w_proj.astype(BF16)
    for i in range(depth):
        proj, lr = _in_proj(h, norm_mix_w[i], w_in_t, i)
        o_h = _hgrn(proj, lbs[i], hgrn_norm_w[i], tri, masks, signs, batch=batch)
        w2p = jnp.zeros((LANES, gla_gate_w2.shape[2]), F32).at[:GLA_GATE_RANK].set(gla_gate_w2[i]).astype(BF16)
        o_g = _gla(proj, lr, w2p, gla_gate_b[i], gla_norm_w[i], tri, masks, signs, batch=batch)
        j = i // 2
        last = i == depth - 1
        if i % 2 == 0:
            h1, hn = _out_proj(o_h, o_g, h, w_out_bf, i, norm_ffn_w[i], hn_dtype=BF16)
            n_blk = n // moe_block
            y = _ffn(hn, dense_w_gate, dense_w_up, dense_w_down,
                     jnp.full((n_blk,), j, jnp.int32), jnp.full((1,), n_blk, jnp.int32),
                     jnp.full((n_blk,), moe_block, jnp.int32), tm=moe_block)
            h = _ple(h1, y, p3, i, norm_ple_w[i], ple_wg_bf, ple_wp_bf, final_norm_w, final=last)
        else:
            rw = jnp.zeros((d, LANES), F32).at[:, :N_EXPERTS].set(moe_router[j])
            rw_hi = rw.astype(BF16)
            router2 = jnp.concatenate([rw_hi, (rw - rw_hi.astype(F32)).astype(BF16)], axis=1)
            h1, hn, logits = _out_proj(o_h, o_g, h, w_out_bf, i, norm_ffn_w[i], router2)
            dest, top_w, block_e, nused, nrows, zero_start, cap = _route(logits[:, :N_EXPERTS], moe_block)
            xs = _dispatch(hn, dest, zero_start, cap, block=moe_block)
            ys = _ffn(xs, moe_wg, moe_wu, moe_wd, block_e + j * N_EXPERTS, nused, nrows, tm=moe_block)
            h = _ple(h1, ys, p3, i, norm_ple_w[i], ple_wg_bf, ple_wp_bf, final_norm_w, final=last,
                     dest=dest, top_w=top_w)
    return h.reshape(batch, seq, d)
```
